```python
import jax, jax.numpy as jnp
from jax import lax
import numpy as np

D_MODEL = 2048
BATCH = 1
SEQ = 8192
DEPTH = 1
DEC_BATCH = 32
DEC_SEQ = 1
PAST_LEN = 16384
PAGE_SIZE = 128

N_HEADS = 16
HEAD_DIM = D_MODEL // N_HEADS
N_KV_HEADS = 4
IDX_HEADS = 16
IDX_DIM = 64
TOPK_MAX = 256
D_CONV = D_MODEL
CONV_WIDTH = 3
D_FF = 5632
ROPE_THETA = 500000.0
ROT_DIV = 4
Q_BLOCK = 128
POOL_NUM = 5
POOL_DEN = 4
EPS = 1e-6

kernel_name = "macaron_gated_shortconv_dsa_decoder_step"


def in_proj_sizes():
    return (D_CONV, D_CONV, D_CONV,
            N_HEADS * HEAD_DIM, N_KV_HEADS * HEAD_DIM, N_KV_HEADS * HEAD_DIM,
            IDX_HEADS * IDX_DIM, IDX_DIM, IDX_HEADS,
            D_MODEL, D_MODEL)


def rmsnorm(x, g):
    xf = x.astype(jnp.float32)
    y = xf * lax.rsqrt(jnp.mean(xf * xf, axis=-1, keepdims=True) + EPS) * g.astype(jnp.float32)
    return y.astype(x.dtype)


def swiglu(x, w_gu, w_d):
    g, u = jnp.split(x @ w_gu, 2, axis=-1)
    return (jax.nn.silu(g) * u) @ w_d


def ffn_half(x, pre, post, w_gu, w_d):
    return x + 0.5 * rmsnorm(swiglu(rmsnorm(x, pre), w_gu, w_d), post)


def rope(x, pos):
    rot = x.shape[-1] // ROT_DIV
    half = rot // 2
    inv = ROPE_THETA ** (-jnp.arange(half, dtype=jnp.float32) / half)
    ang = pos.astype(jnp.float32)[:, None] * inv[None, :]
    cos = jnp.cos(ang)[:, None, :].astype(x.dtype)
    sin = jnp.sin(ang)[:, None, :].astype(x.dtype)
    x1 = x[..., :half]
    x2 = x[..., half:rot]
    return jnp.concatenate([x1 * cos - x2 * sin, x2 * cos + x1 * sin, x[..., rot:]], axis=-1)


def mixer_inputs(u, w_in, pos):
    offs = [int(o) for o in np.cumsum(in_proj_sizes())[:-1]]
    xin, b, c, q, k, v, iq, ik, iw, gc, ga = jnp.split(u @ w_in, offs, axis=-1)
    lead = u.shape[:-1]
    q = rope(q.reshape(*lead, N_HEADS, HEAD_DIM), pos)
    k = rope(k.reshape(*lead, N_KV_HEADS, HEAD_DIM), pos)
    v = v.reshape(*lead, N_KV_HEADS, HEAD_DIM)
    iq = rope(iq.reshape(*lead, IDX_HEADS, IDX_DIM), pos)
    ik = rope(ik[..., None, :], pos)[..., 0, :]
    return c * xin, b, q, k, v, iq, ik, iw, gc, ga


def causal_conv(z_ext, w_conv, T):
    out = z_ext[:, 0:T] * w_conv[0]
    for j in range(1, CONV_WIDTH):
        out = out + z_ext[:, j:j + T] * w_conv[j]
    return out


def indexer_topk(iq, iw, ik, qpos, kpos, topk):
    s = jnp.einsum('nthd,nsd->nths', iq, ik).astype(jnp.float32) * IDX_DIM ** -0.5
    score = jnp.einsum('nths,nth->nts', jax.nn.relu(s), iw.astype(jnp.float32) * IDX_HEADS ** -0.5)
    score = jnp.where(kpos[None, None, :] <= qpos[None, :, None], score, -jnp.inf)
    vals, idx = lax.top_k(score, topk)
    return idx, jnp.isfinite(vals)


def sparse_attend(q, k_sel, v_sel, valid):
    N, T = q.shape[:2]
    qg = q.reshape(N, T, N_KV_HEADS, N_HEADS // N_KV_HEADS, HEAD_DIM)
    logits = jnp.einsum('btngd,btknd->btngk', qg, k_sel).astype(jnp.float32) * HEAD_DIM ** -0.5
    logits = jnp.where(valid[:, :, None, None, :], logits, -jnp.inf)
    p = jax.nn.softmax(logits, axis=-1).astype(v_sel.dtype)
    o = jnp.einsum('btngk,btknd->btngd', p, v_sel)
    return o.reshape(N, T, N_HEADS * HEAD_DIM)


def gather_rows(rows, idx):
    return jax.vmap(lambda r, i: r[i])(rows, idx)


def prompt_attention(q, k, v, iq, ik, iw, pos):
    N, T = q.shape[:2]
    nb = T // Q_BLOCK
    topk = min(TOPK_MAX, T // 4)

    def to_blocks(a):
        return jnp.moveaxis(a.reshape(N, nb, Q_BLOCK, *a.shape[2:]), 1, 0)

    def block(args):
        q_b, iq_b, iw_b, pos_b = args
        idx, valid = indexer_topk(iq_b, iw_b, ik, pos_b, pos, topk)
        return sparse_attend(q_b, gather_rows(k, idx), gather_rows(v, idx), valid)

    out = lax.map(block, (to_blocks(q), to_blocks(iq), to_blocks(iw), pos.reshape(nb, Q_BLOCK)))
    return jnp.moveaxis(out, 0, 1).reshape(N, T, N_HEADS * HEAD_DIM)


def sample_attention(q, k, v, iq, ik, iw, pos, cache_k, cache_v, cache_ik, page_table):
    N, T = q.shape[:2]
    n_pages = page_table.shape[1]
    ps = cache_k.shape[1]
    past = n_pages * ps
    L = past + T
    topk = min(TOPK_MAX, L // 4)
    ik_past = cache_ik[page_table].reshape(N, past, IDX_DIM).astype(ik.dtype)
    ik_all = jnp.concatenate([ik_past, ik], axis=1)
    idx, valid = indexer_topk(iq, iw, ik_all, pos, jnp.arange(L, dtype=jnp.int32), topk)
    in_past = (idx < past)[..., None, None]
    pidx = jnp.minimum(idx, past - 1)
    phys = jnp.take_along_axis(page_table, (pidx // ps).reshape(N, -1), axis=1).reshape(pidx.shape)
    off = pidx % ps
    nidx = jnp.clip(idx - past, 0, T - 1)
    k_sel = jnp.where(in_past, cache_k[phys, off].astype(k.dtype), gather_rows(k, nidx))
    v_sel = jnp.where(in_past, cache_v[phys, off].astype(v.dtype), gather_rows(v, nidx))
    return sparse_attend(q, k_sel, v_sel, valid)


def merge(conv_y, attn_o, gc, ga, w_conv_out, w_attn_out, w_out):
    m = jax.nn.sigmoid(gc) * (conv_y @ w_conv_out) + jax.nn.sigmoid(ga) * (attn_o @ w_attn_out)
    return m @ w_out


def setup_inputs(seed: int = 0) -> dict:
    key = jax.random.key(seed)
    ks = jax.random.split(key, 32)
    f = jnp.float32
    n_pages = PAST_LEN // PAGE_SIZE
    n_pool = (DEC_BATCH * n_pages * POOL_NUM + POOL_DEN - 1) // POOL_DEN
    total_in = sum(in_proj_sizes())

    def dense(k, shape, fan_in):
        return jax.random.normal(k, shape, f) * fan_in ** -0.5

    def gain(k, n):
        return 1.0 + 0.05 * jax.random.normal(k, (DEPTH, n), f)

    perm = jax.random.permutation(ks[6], n_pool)[: DEC_BATCH * n_pages]
    return {
        "x_prompt": jax.random.normal(ks[0], (BATCH, SEQ, D_MODEL), f),
        "x_sample": jax.random.normal(ks[1], (DEC_BATCH, DEC_SEQ, D_MODEL), f),
        "cache_k": jax.random.normal(ks[2], (DEPTH, n_pool, PAGE_SIZE, N_KV_HEADS, HEAD_DIM), f),
        "cache_v": jax.random.normal(ks[3], (DEPTH, n_pool, PAGE_SIZE, N_KV_HEADS, HEAD_DIM), f),
        "cache_idx_k": jax.random.normal(ks[4], (DEPTH, n_pool, PAGE_SIZE, IDX_DIM), f),
        "state_conv": jax.random.normal(ks[5], (DEPTH, DEC_BATCH, CONV_WIDTH - 1, D_CONV), f),
        "page_table": perm.reshape(DEC_BATCH, n_pages).astype(jnp.int32),
        "norm_ffn1_pre": gain(ks[7], D_MODEL),
        "norm_ffn1_post": gain(ks[8], D_MODEL),
        "w_ffn1_gate_up": dense(ks[9], (DEPTH, D_MODEL, 2 * D_FF), D_MODEL),
        "w_ffn1_down": dense(ks[10], (DEPTH, D_FF, D_MODEL), D_FF),
        "norm_mix_pre": gain(ks[11], D_MODEL),
        "norm_mix_post": gain(ks[12], D_MODEL),
        "w_in": dense(ks[13], (DEPTH, D_MODEL, total_in), D_MODEL),
        "w_conv": dense(ks[14], (DEPTH, CONV_WIDTH, D_CONV), CONV_WIDTH),
        "w_conv_out": dense(ks[15], (DEPTH, D_CONV, D_MODEL), D_CONV),
        "w_attn_out": dense(ks[16], (DEPTH, N_HEADS * HEAD_DIM, D_MODEL), N_HEADS * HEAD_DIM),
        "w_out": dense(ks[17], (DEPTH, D_MODEL, D_MODEL), D_MODEL),
        "norm_ffn2_pre": gain(ks[18], D_MODEL),
        "norm_ffn2_post": gain(ks[19], D_MODEL),
        "w_ffn2_gate_up": dense(ks[20], (DEPTH, D_MODEL, 2 * D_FF), D_MODEL),
        "w_ffn2_down": dense(ks[21], (DEPTH, D_FF, D_MODEL), D_FF),
    }


def reference(x_prompt, x_sample, cache_k, cache_v, cache_idx_k, state_conv, page_table,
              norm_ffn1_pre, norm_ffn1_post, w_ffn1_gate_up, w_ffn1_down,
              norm_mix_pre, norm_mix_post, w_in, w_conv, w_conv_out, w_attn_out, w_out,
              norm_ffn2_pre, norm_ffn2_post, w_ffn2_gate_up, w_ffn2_down):
    Bp, T, _ = x_prompt.shape
    Bs, Ts, _ = x_sample.shape
    past = page_table.shape[1] * cache_k.shape[2]
    pos_p = jnp.arange(T, dtype=jnp.int32)
    pos_s = past + jnp.arange(Ts, dtype=jnp.int32)
    hp, hs = x_prompt, x_sample
    kp_l, vp_l, ikp_l, cp_l, ks_l, vs_l, iks_l, cs_l = [], [], [], [], [], [], [], []
    for l in range(DEPTH):
        hp = ffn_half(hp, norm_ffn1_pre[l], norm_ffn1_post[l], w_ffn1_gate_up[l], w_ffn1_down[l])
        hs = ffn_half(hs, norm_ffn1_pre[l], norm_ffn1_post[l], w_ffn1_gate_up[l], w_ffn1_down[l])

        up = rmsnorm(hp, norm_mix_pre[l])
        z, b, q, k, v, iq, ik, iw, gc, ga = mixer_inputs(up, w_in[l], pos_p)
        z_ext = jnp.pad(z, ((0, 0), (CONV_WIDTH - 1, 0), (0, 0)))
        conv_y = b * causal_conv(z_ext, w_conv[l], T)
        attn_o = prompt_attention(q, k, v, iq, ik, iw, pos_p)
        mix = merge(conv_y, attn_o, gc, ga, w_conv_out[l], w_attn_out[l], w_out[l])
        hp = hp + rmsnorm(mix, norm_mix_post[l])
        kp_l.append(k.reshape(Bp, T // PAGE_SIZE, PAGE_SIZE, N_KV_HEADS, HEAD_DIM))
        vp_l.append(v.reshape(Bp, T // PAGE_SIZE, PAGE_SIZE, N_KV_HEADS, HEAD_DIM))
        ikp_l.append(ik.reshape(Bp, T // PAGE_SIZE, PAGE_SIZE, IDX_DIM))
        cp_l.append(z_ext[:, T:])

        us = rmsnorm(hs, norm_mix_pre[l])
        z, b, q, k, v, iq, ik, iw, gc, ga = mixer_inputs(us, w_in[l], pos_s)
        z_ext = jnp.concatenate([state_conv[l].astype(z.dtype), z], axis=1)
        conv_y = b * causal_conv(z_ext, w_conv[l], Ts)
        attn_o = sample_attention(q, k, v, iq, ik, iw, pos_s, cache_k[l], cache_v[l], cache_idx_k[l], page_table)
        mix = merge(conv_y, attn_o, gc, ga, w_conv_out[l], w_attn_out[l], w_out[l])
        hs = hs + rmsnorm(mix, norm_mix_post[l])
        ks_l.append(k)
        vs_l.append(v)
        iks_l.append(ik)
        cs_l.append(z_ext[:, Ts:])

        hp = ffn_half(hp, norm_ffn2_pre[l], norm_ffn2_post[l], w_ffn2_gate_up[l], w_ffn2_down[l])
        hs = ffn_half(hs, norm_ffn2_pre[l], norm_ffn2_post[l], w_ffn2_gate_up[l], w_ffn2_down[l])

    return (hp, hs,
            jnp.stack(kp_l), jnp.stack(vp_l), jnp.stack(ikp_l), jnp.stack(cp_l),
            jnp.stack(ks_l), jnp.stack(vs_l), jnp.stack(iks_l), jnp.stack(cs_l))
```

```python
import functools

import numpy as np
import jax
import jax.numpy as jnp
from jax import lax
from jax.experimental import pallas as pl
from jax.experimental.pallas import tpu as pltpu

F32 = jnp.float32
BF16 = jnp.bfloat16
I32 = jnp.int32

N_HEADS = 16
N_KV_HEADS = 4
IDX_HEADS = 16
IDX_DIM = 64
TOPK_MAX = 256
CONV_WIDTH = 3
ROPE_THETA = 500000.0
ROT_DIV = 4
EPS = 1e-6

LANES = 128
VMEM_LIMIT = 56 * 1024 * 1024
NEG_BIAS = -1e30
INT_MIN = -2 ** 31
INT_MAX = 2 ** 31 - 1
NEG_INF_KEY = int(np.int32(np.uint32(0x807FFFFF)))


def _cparams(sem):
    return pltpu.CompilerParams(dimension_semantics=sem, vmem_limit_bytes=VMEM_LIMIT)


def _rms_scale(x):
    return lax.rsqrt(jnp.mean(x * x, axis=-1, keepdims=True) + EPS)


def _nt_dot(a, b):
    return lax.dot_general(a, b, (((1,), (1,)), ((), ())), preferred_element_type=F32)


def _dot(a, b):
    return jnp.dot(a, b, preferred_element_type=F32)


def _ffn_kernel(x_ref, pre_ref, post_ref, wg_ref, wu_ref, wd_ref, o_ref, xn_ref, acc_ref):
    j = pl.program_id(1)

    @pl.when(j == 0)
    def _():
        x = x_ref[...]
        xn_ref[...] = (x * _rms_scale(x) * pre_ref[...]).astype(BF16)
        acc_ref[...] = jnp.zeros_like(acc_ref)

    xn = xn_ref[...]
    g = _dot(xn, wg_ref[...])
    u = _dot(xn, wu_ref[...])
    h = (g * jax.nn.sigmoid(g) * u).astype(BF16)
    acc_ref[...] += _dot(h, wd_ref[...])

    @pl.when(j == pl.num_programs(1) - 1)
    def _():
        y = acc_ref[...]
        o_ref[...] = x_ref[...] + 0.5 * (y * _rms_scale(y) * post_ref[...])


def _ffn_half(x, pre, post, w_gu, w_d, *, tm, tf):
    m, d = x.shape
    f = w_d.shape[0]
    nf = f // tf
    return pl.pallas_call(
        _ffn_kernel,
        grid=(m // tm, nf),
        in_specs=[
            pl.BlockSpec((tm, d), lambda i, j: (i, 0)),
            pl.BlockSpec((1, d), lambda i, j: (0, 0)),
            pl.BlockSpec((1, d), lambda i, j: (0, 0)),
            pl.BlockSpec((d, tf), lambda i, j: (0, j)),
            pl.BlockSpec((d, tf), lambda i, j: (0, nf + j)),
            pl.BlockSpec((tf, d), lambda i, j: (j, 0)),
        ],
        out_specs=pl.BlockSpec((tm, d), lambda i, j: (i, 0)),
        out_shape=jax.ShapeDtypeStruct((m, d), F32),
        scratch_shapes=[pltpu.VMEM((tm, d), BF16), pltpu.VMEM((tm, d), F32)],
        compiler_params=_cparams(("arbitrary", "arbitrary")),
        name="ffn_half",
    )(x, pre, post, w_gu, w_gu, w_d)


def _merge_kernel(h_ref, cy_ref, ao_ref, gc_ref, ga_ref, post_ref, wc_ref, wa_ref, wo_ref,
                  o_ref, acc_ref):
    j = pl.program_id(1)

    @pl.when(j == 0)
    def _():
        acc_ref[...] = jnp.zeros_like(acc_ref)

    mc = _dot(cy_ref[...], wc_ref[...])
    ma = _dot(ao_ref[...], wa_ref[...])
    mix = gc_ref[...].astype(F32) * mc + ga_ref[...].astype(F32) * ma
    acc_ref[...] += _dot(mix.astype(BF16), wo_ref[...])

    @pl.when(j == pl.num_programs(1) - 1)
    def _():
        y = acc_ref[...]
        o_ref[...] = h_ref[...] + y * _rms_scale(y) * post_ref[...]


def _merge(h, conv_y, attn_o, gc, ga, post, w_co, w_ao, w_out, *, tm, tn):
    m, d = h.shape
    return pl.pallas_call(
        _merge_kernel,
        grid=(m // tm, d // tn),
        in_specs=[
            pl.BlockSpec((tm, d), lambda i, j: (i, 0)),
            pl.BlockSpec((tm, d), lambda i, j: (i, 0)),
            pl.BlockSpec((tm, d), lambda i, j: (i, 0)),
            pl.BlockSpec((tm, tn), lambda i, j: (i, j)),
            pl.BlockSpec((tm, tn), lambda i, j: (i, j)),
            pl.BlockSpec((1, d), lambda i, j: (0, 0)),
            pl.BlockSpec((d, tn), lambda i, j: (0, j)),
            pl.BlockSpec((d, tn), lambda i, j: (0, j)),
            pl.BlockSpec((tn, d), lambda i, j: (j, 0)),
        ],
        out_specs=pl.BlockSpec((tm, d), lambda i, j: (i, 0)),
        out_shape=jax.ShapeDtypeStruct((m, d), F32),
        scratch_shapes=[pltpu.VMEM((tm, d), F32)],
        compiler_params=_cparams(("arbitrary", "arbitrary")),
        name="merge",
    )(h, conv_y, attn_o, gc, ga, post, w_co, w_ao, w_out)


def _norm_to_scratch(h_ref, gain_ref, xn_ref):
    @pl.when(pl.program_id(1) == 0)
    def _():
        x = h_ref[...]
        xn_ref[...] = (x * _rms_scale(x) * gain_ref[...]).astype(BF16)


def _rope(x, a, b, c, half):
    return x * a + pltpu.roll(x, LANES - half, axis=1) * b + pltpu.roll(x, half, axis=1) * c


def _conv_seq_kernel(h_ref, gain_ref, wx_ref, wb_ref, wc_ref, wconv_ref, cy_ref, tail_ref,
                     xn_ref, carry_ref):
    i = pl.program_id(0)
    j = pl.program_id(1)
    _norm_to_scratch(h_ref, gain_ref, xn_ref)
    xn = xn_ref[...]
    z = _dot(xn, wc_ref[...]) * _dot(xn, wx_ref[...])
    b = _dot(xn, wb_ref[...])
    tm = z.shape[0]

    @pl.when(i == 0)
    def _():
        carry_ref[j] = jnp.zeros(carry_ref.shape[1:], F32)

    prev = carry_ref[j]
    row = lax.broadcasted_iota(I32, z.shape, 0)
    z1 = jnp.where(row == 0, prev[1:2, :], pltpu.roll(z, 1, axis=0))
    z2 = jnp.where(row == 0, prev[0:1, :],
                   jnp.where(row == 1, prev[1:2, :], pltpu.roll(z, 2, axis=0)))
    w = wconv_ref[...]
    cy_ref[...] = (b * (z2 * w[0:1, :] + z1 * w[1:2, :] + z * w[2:3, :])).astype(BF16)
    tail = z[tm - 8:, :]
    carry_ref[j] = pltpu.roll(tail, 2, axis=0)
    tail_ref[...] = tail


def _conv_seq(h, gain, w_x, w_b, w_c, w_conv, *, tm, tn):
    m, d = h.shape
    dc = w_x.shape[1]
    wspec = pl.BlockSpec((d, tn), lambda i, j: (0, j))
    return pl.pallas_call(
        _conv_seq_kernel,
        grid=(m // tm, dc // tn),
        in_specs=[
            pl.BlockSpec((tm, d), lambda i, j: (i, 0)),
            pl.BlockSpec((1, d), lambda i, j: (0, 0)),
            wspec, wspec, wspec,
            pl.BlockSpec((CONV_WIDTH, tn), lambda i, j: (0, j)),
        ],
        out_specs=[
            pl.BlockSpec((tm, tn), lambda i, j: (i, j)),
            pl.BlockSpec((8, tn), lambda i, j: (i, j)),
        ],
        out_shape=[jax.ShapeDtypeStruct((m, dc), BF16), jax.ShapeDtypeStruct((m // tm * 8, dc), F32)],
        scratch_shapes=[pltpu.VMEM((tm, d), BF16), pltpu.VMEM((dc // tn, 8, tn), F32)],
        compiler_params=_cparams(("arbitrary", "arbitrary")),
        name="conv_seq",
    )(h, gain, w_x, w_b, w_c, w_conv)


def _conv_step_kernel(h_ref, gain_ref, wx_ref, wb_ref, wc_ref, wconv_ref, s0_ref, s1_ref,
                      cy_ref, z_ref, xn_ref):
    _norm_to_scratch(h_ref, gain_ref, xn_ref)
    xn = xn_ref[...]
    z = _dot(xn, wc_ref[...]) * _dot(xn, wx_ref[...])
    b = _dot(xn, wb_ref[...])
    w = wconv_ref[...]
    cy_ref[...] = (b * (s0_ref[...] * w[0:1, :] + s1_ref[...] * w[1:2, :] + z * w[2:3, :])
                   ).astype(BF16)
    z_ref[...] = z


def _conv_step(h, gain, w_x, w_b, w_c, w_conv, s0, s1, *, tn):
    m, d = h.shape
    dc = w_x.shape[1]
    wspec = pl.BlockSpec((d, tn), lambda i, j: (0, j))
    cspec = pl.BlockSpec((m, tn), lambda i, j: (0, j))
    return pl.pallas_call(
        _conv_step_kernel,
        grid=(1, dc // tn),
        in_specs=[
            pl.BlockSpec((m, d), lambda i, j: (0, 0)),
            pl.BlockSpec((1, d), lambda i, j: (0, 0)),
            wspec, wspec, wspec,
            pl.BlockSpec((CONV_WIDTH, tn), lambda i, j: (0, j)),
            cspec, cspec,
        ],
        out_specs=[cspec, cspec],
        out_shape=[jax.ShapeDtypeStruct((m, dc), BF16), jax.ShapeDtypeStruct((m, dc), F32)],
        scratch_shapes=[pltpu.VMEM((m, d), BF16)],
        compiler_params=_cparams(("arbitrary", "arbitrary")),
        name="conv_step",
    )(h, gain, w_x, w_b, w_c, w_conv, s0, s1)


def _rope_proj_kernel(h_ref, gain_ref, w_ref, a_ref, b_ref, c_ref, o_ref, xn_ref, *, half, scale):
    _norm_to_scratch(h_ref, gain_ref, xn_ref)
    y = _dot(xn_ref[...], w_ref[...])
    a, b, c = a_ref[...], b_ref[...], c_ref[...]
    for g in range(y.shape[1] // LANES):
        sl = slice(g * LANES, (g + 1) * LANES)
        o_ref[:, sl] = (_rope(y[:, sl], a, b, c, half) * scale).astype(o_ref.dtype)


def _rope_proj(h, gain, w, tabs, *, half, scale, tm, tn):
    m, d = h.shape
    n = w.shape[1]
    tspec = pl.BlockSpec((tm, LANES), lambda i, j: (i, 0))
    return pl.pallas_call(
        functools.partial(_rope_proj_kernel, half=half, scale=scale),
        grid=(m // tm, n // tn),
        in_specs=[
            pl.BlockSpec((tm, d), lambda i, j: (i, 0)),
            pl.BlockSpec((1, d), lambda i, j: (0, 0)),
            pl.BlockSpec((d, tn), lambda i, j: (0, j)),
            tspec, tspec, tspec,
        ],
        out_specs=pl.BlockSpec((tm, tn), lambda i, j: (i, j)),
        out_shape=jax.ShapeDtypeStruct((m, n), BF16),
        scratch_shapes=[pltpu.VMEM((tm, d), BF16)],
        compiler_params=_cparams(("arbitrary", "arbitrary")),
        name="rope_proj",
    )(h, gain, w, *tabs)


def _kv_proj_kernel(h_ref, gain_ref, w_ref, ka_ref, kb_ref, kc_ref, ia_ref, ib_ref, ic_ref,
                    k_ref, v_ref, misc_ref, kbf_ref, vbf_ref, ika_ref, ikb_ref, *, dkv):
    x = h_ref[...]
    xn = (x * _rms_scale(x) * gain_ref[...]).astype(BF16)
    y = _dot(xn, w_ref[...])
    a, b, c = ka_ref[...], kb_ref[...], kc_ref[...]
    for g in range(dkv // LANES):
        sl = slice(g * LANES, (g + 1) * LANES)
        kg = _rope(y[:, sl], a, b, c, LANES // ROT_DIV // 2)
        k_ref[:, sl] = kg
        kbf_ref[:, sl] = kg.astype(BF16)
    v = y[:, dkv:2 * dkv]
    v_ref[...] = v
    vbf_ref[...] = v.astype(BF16)
    misc = _rope(y[:, 2 * dkv:], ia_ref[...], ib_ref[...], ic_ref[...], IDX_DIM // ROT_DIV // 2)
    misc_ref[...] = misc
    lane = lax.broadcasted_iota(I32, misc.shape, 1)
    ik_lo = jnp.where(lane < IDX_DIM, misc, 0.0)
    ika_ref[...] = ik_lo.astype(BF16)
    ikb_ref[...] = pltpu.roll(ik_lo, IDX_DIM, axis=1).astype(BF16)


def _kv_proj(h, gain, w_kvm, ktabs, itabs, *, dkv, tm):
    m, d = h.shape
    n = w_kvm.shape[1]
    tspec = pl.BlockSpec((tm, LANES), lambda i: (i, 0))
    kvspec = pl.BlockSpec((tm, dkv), lambda i: (i, 0))
    mspec = pl.BlockSpec((tm, LANES), lambda i: (i, 0))
    return pl.pallas_call(
        functools.partial(_kv_proj_kernel, dkv=dkv),
        grid=(m // tm,),
        in_specs=[
            pl.BlockSpec((tm, d), lambda i: (i, 0)),
            pl.BlockSpec((1, d), lambda i: (0, 0)),
            pl.BlockSpec((d, n), lambda i: (0, 0)),
            tspec, tspec, tspec, tspec, tspec, tspec,
        ],
        out_specs=[kvspec, kvspec, mspec, kvspec, kvspec, mspec, mspec],
        out_shape=[
            jax.ShapeDtypeStruct((m, dkv), F32), jax.ShapeDtypeStruct((m, dkv), F32),
            jax.ShapeDtypeStruct((m, LANES), F32),
            jax.ShapeDtypeStruct((m, dkv), BF16), jax.ShapeDtypeStruct((m, dkv), BF16),
            jax.ShapeDtypeStruct((m, LANES), BF16), jax.ShapeDtypeStruct((m, LANES), BF16),
        ],
        compiler_params=_cparams(("arbitrary",)),
        name="kv_proj",
    )(h, gain, w_kvm, *ktabs, *itabs)


def _gate_proj_kernel(h_ref, gain_ref, wc_ref, wa_ref, gc_ref, ga_ref, xn_ref):
    _norm_to_scratch(h_ref, gain_ref, xn_ref)
    xn = xn_ref[...]
    gc_ref[...] = jax.nn.sigmoid(_dot(xn, wc_ref[...])).astype(BF16)
    ga_ref[...] = jax.nn.sigmoid(_dot(xn, wa_ref[...])).astype(BF16)


def _gate_proj(h, gain, w_gc, w_ga, *, tm, tn):
    m, d = h.shape
    n = w_gc.shape[1]
    wspec = pl.BlockSpec((d, tn), lambda i, j: (0, j))
    ospec = pl.BlockSpec((tm, tn), lambda i, j: (i, j))
    return pl.pallas_call(
        _gate_proj_kernel,
        grid=(m // tm, n // tn),
        in_specs=[
            pl.BlockSpec((tm, d), lambda i, j: (i, 0)),
            pl.BlockSpec((1, d), lambda i, j: (0, 0)),
            wspec, wspec,
        ],
        out_specs=[ospec, ospec],
        out_shape=[jax.ShapeDtypeStruct((m, n), BF16), jax.ShapeDtypeStruct((m, n), BF16)],
        scratch_shapes=[pltpu.VMEM((tm, d), BF16)],
        compiler_params=_cparams(("arbitrary", "arbitrary")),
        name="gate_proj",
    )(h, gain, w_gc, w_ga)


def _ordered_key(score):
    bits = pltpu.bitcast(score, I32)
    return bits ^ ((bits >> 31) & INT_MAX)


def _kth_largest(count_ge, k, like):
    def body(b, lo):
        cand = lo + jnp.left_shift(jnp.int32(1), 31 - b)
        return jnp.where(count_ge(cand) >= k, cand, lo)
    return lax.fori_loop(0, 32, body, jnp.full_like(like, INT_MIN))


def _tie_limit(count_eq_before, need, nbits, like):
    def body(b, lo):
        cand = lo + jnp.left_shift(jnp.int32(1), nbits - 1 - b)
        return jnp.where(count_eq_before(cand) < need, cand, lo)
    return lax.fori_loop(0, nbits, body, jnp.zeros_like(like))


def _prompt_attn_kernel(iq_ref, wt_ref, q_ref, ika_ref, ikb_ref, k_ref, vt_ref, o_ref,
                        key_ref, lim_ref, m_ref, l_ref, acc_ref, *, tq, sc, topk, n_rep, hd):
    i = pl.program_id(0)
    t0 = i * tq
    n_chunks = (t0 + tq) // sc
    n_pairs = iq_ref.shape[1] // LANES
    n_kv = k_ref.shape[1] // hd
    pos_bits = int(k_ref.shape[0] - 1).bit_length()

    def score_chunk(c, carry):
        s0 = pl.multiple_of(c * sc, sc)
        ka = ika_ref[pl.ds(s0, sc), :]
        kb = ikb_ref[pl.ds(s0, sc), :]
        acc = jnp.zeros((sc, tq), F32)
        for p in range(n_pairs):
            iq_p = iq_ref[:, p * LANES:(p + 1) * LANES]
            acc += jnp.maximum(_nt_dot(ka, iq_p), 0.0) * wt_ref[2 * p:2 * p + 1, :]
            acc += jnp.maximum(_nt_dot(kb, iq_p), 0.0) * wt_ref[2 * p + 1:2 * p + 2, :]
        spos = s0 + lax.broadcasted_iota(I32, (sc, tq), 0)
        tpos = t0 + lax.broadcasted_iota(I32, (sc, tq), 1)
        acc = jnp.where(spos <= tpos, acc, -jnp.inf)
        key_ref[pl.ds(s0, sc), :] = _ordered_key(acc)
        return carry

    lax.fori_loop(0, n_chunks, score_chunk, 0)

    def column_count(pred):
        def body(c, cnt):
            s0 = pl.multiple_of(c * sc, sc)
            blk = key_ref[pl.ds(s0, sc), :]
            spos = s0 + lax.broadcasted_iota(I32, (sc, tq), 0)
            hit = jnp.where(pred(blk, spos), 1, 0).astype(I32)
            return cnt + hit.reshape(sc // 8, 8, tq).sum(axis=0)
        cnt = lax.fori_loop(0, n_chunks, body, jnp.zeros((8, tq), I32))
        return cnt.astype(F32).sum(axis=0, keepdims=True).astype(I32)

    like = jnp.zeros((1, tq), I32)
    thr = _kth_largest(lambda v: column_count(lambda blk, spos: blk >= v), topk, like)
    n_gt = column_count(lambda blk, spos: blk > thr)
    n_ge = column_count(lambda blk, spos: blk >= thr)
    need = topk - n_gt
    has_tie = jnp.max(jnp.where((n_ge > topk) & (thr > NEG_INF_KEY), 1.0, 0.0)) > 0.0
    lim_ref[...] = jnp.full(lim_ref.shape, INT_MAX, I32)

    @pl.when(has_tie)
    def _():
        lim = _tie_limit(
            lambda p: column_count(lambda blk, spos: (blk == thr) & (spos < p)),
            need, pos_bits + 1, like)
        lim_ref[...] = jnp.broadcast_to(lim, lim_ref.shape)

    lim = lim_ref[0:1, :]

    def bias_chunk(c, carry):
        s0 = pl.multiple_of(c * sc, sc)
        blk = key_ref[pl.ds(s0, sc), :]
        spos = s0 + lax.broadcasted_iota(I32, (sc, tq), 0)
        sel = ((blk > thr) | ((blk == thr) & (spos <= lim))) & (blk > NEG_INF_KEY)
        key_ref[pl.ds(s0, sc), :] = pltpu.bitcast(jnp.where(sel, 0.0, NEG_BIAS).astype(F32), I32)
        return carry

    lax.fori_loop(0, n_chunks, bias_chunk, 0)

    for g in range(n_kv):
        m_ref[...] = jnp.full(m_ref.shape, NEG_BIAS, F32)
        l_ref[...] = jnp.zeros(l_ref.shape, F32)
        acc_ref[...] = jnp.zeros(acc_ref.shape, F32)

        def attn_chunk(c, carry):
            s0 = pl.multiple_of(c * sc, sc)
            kc = k_ref[pl.ds(s0, sc), g * hd:(g + 1) * hd]
            vt = vt_ref[g, c]
            bias = pltpu.bitcast(key_ref[pl.ds(s0, sc), :], F32)
            for r in range(n_rep):
                head = g * n_rep + r
                cols = slice(r * tq, (r + 1) * tq)
                logit = _nt_dot(kc, q_ref[:, head * hd:(head + 1) * hd]) + bias
                m_old = m_ref[:, cols]
                m_new = jnp.maximum(m_old, jnp.max(logit, axis=0, keepdims=True))
                p = jnp.exp(logit - m_new)
                alpha = jnp.exp(m_old - m_new)
                l_ref[:, cols] = alpha * l_ref[:, cols] + jnp.sum(p, axis=0, keepdims=True)
                acc_ref[:, cols] = alpha * acc_ref[:, cols] + _dot(vt, p.astype(BF16))
                m_ref[:, cols] = m_new
            return carry

        lax.fori_loop(0, n_chunks, attn_chunk, 0)
        for r in range(n_rep):
            head = g * n_rep + r
            cols = slice(r * tq, (r + 1) * tq)
            out_t = acc_ref[:, cols] / l_ref[:, cols]
            o_ref[:, head * hd:(head + 1) * hd] = out_t.T.astype(o_ref.dtype)


def _prompt_attention(iq, wt, q, ika, ikb, k_bf, vt, *, tq, sc, topk, hd):
    t, dq = q.shape
    n_kv = k_bf.shape[1] // hd
    n_rep = dq // hd // n_kv
    assert tq >= topk and tq % sc == 0 and t % tq == 0
    resident = functools.partial(pl.BlockSpec, pipeline_mode=pl.Buffered(1))
    return pl.pallas_call(
        functools.partial(_prompt_attn_kernel, tq=tq, sc=sc, topk=topk, n_rep=n_rep, hd=hd),
        grid=(t // tq,),
        in_specs=[
            pl.BlockSpec((tq, iq.shape[1]), lambda i: (i, 0)),
            pl.BlockSpec((wt.shape[0], tq), lambda i: (0, i)),
            pl.BlockSpec((tq, dq), lambda i: (i, 0)),
            resident(ika.shape, lambda i: (0, 0)),
            resident(ikb.shape, lambda i: (0, 0)),
            resident(k_bf.shape, lambda i: (0, 0)),
            resident(vt.shape, lambda i: (0, 0, 0, 0)),
        ],
        out_specs=pl.BlockSpec((tq, dq), lambda i: (i, 0)),
        out_shape=jax.ShapeDtypeStruct((t, dq), BF16),
        scratch_shapes=[
            pltpu.VMEM((t, tq), I32),
            pltpu.VMEM((8, tq), I32),
            pltpu.VMEM((1, n_rep * tq), F32),
            pltpu.VMEM((1, n_rep * tq), F32),
            pltpu.VMEM((hd, n_rep * tq), F32),
        ],
        compiler_params=_cparams(("arbitrary",)),
        name="prompt_attention",
    )(iq, wt, q, ika, ikb, k_bf, vt)


def _sample_score_kernel(pt_ref, iq_ref, w_ref, ikn_ref, cache_ref, o_ref, buf_ref, sem_ref,
                         *, n_pages, page):
    b = pl.program_id(0)
    nb = pl.num_programs(0)

    def page_copy(bb, slot, p):
        return pltpu.make_async_copy(cache_ref.at[pt_ref[bb, p]], buf_ref.at[slot, p],
                                     sem_ref.at[slot])

    def start_all(bb, slot):
        def body(p, carry):
            page_copy(bb, slot, p).start()
            return carry
        lax.fori_loop(0, n_pages, body, 0)

    slot = b % 2

    @pl.when(b == 0)
    def _():
        start_all(b, slot)

    @pl.when(b + 1 < nb)
    def _():
        start_all(b + 1, 1 - slot)

    def wait_body(p, carry):
        page_copy(b, slot, p).wait()
        return carry
    lax.fori_loop(0, n_pages, wait_body, 0)

    iq = iq_ref[0]
    w = w_ref[0]

    def head_sum(keys):
        s = jnp.maximum(_nt_dot(iq, keys), 0.0) * w
        return jnp.sum(s, axis=0, keepdims=True)

    group = 8
    idim = buf_ref.shape[3]

    def group_body(gi, carry):
        p0 = pl.multiple_of(gi * group, group)
        keys = buf_ref[slot, pl.ds(p0, group)].reshape(group * page, idim).astype(BF16)
        s = head_sum(keys)
        for r in range(group):
            o_ref[0, pl.ds(p0 + r, 1), :] = s[:, r * page:(r + 1) * page]
        return carry
    lax.fori_loop(0, n_pages // group, group_body, 0)
    own = head_sum(jnp.broadcast_to(ikn_ref[0], (page, ikn_ref.shape[2])))
    lane = lax.broadcasted_iota(I32, (1, page), 1)
    o_ref[0, pl.ds(n_pages, 1), :] = jnp.where(lane == 0, own, -jnp.inf)
    o_ref[0, pl.ds(n_pages + 1, 7), :] = jnp.full((7, page), -jnp.inf, F32)


def _sample_scores(page_table, iq, w, ik_new, cache_ik):
    bs, n_pages = page_table.shape
    page, idim = cache_ik.shape[1:]
    rows = n_pages + 8
    grid_spec = pltpu.PrefetchScalarGridSpec(
        num_scalar_prefetch=1,
        grid=(bs,),
        in_specs=[
            pl.BlockSpec((1,) + iq.shape[1:], lambda b, pt: (b, 0, 0)),
            pl.BlockSpec((1,) + w.shape[1:], lambda b, pt: (b, 0, 0)),
            pl.BlockSpec((1,) + ik_new.shape[1:], lambda b, pt: (b, 0, 0)),
            pl.BlockSpec(memory_space=pl.ANY),
        ],
        out_specs=pl.BlockSpec((1, rows, page), lambda b, pt: (b, 0, 0)),
        scratch_shapes=[pltpu.VMEM((2, n_pages, page, idim), F32),
                        pltpu.SemaphoreType.DMA((2,))],
    )
    return pl.pallas_call(
        functools.partial(_sample_score_kernel, n_pages=n_pages, page=page),
        grid_spec=grid_spec,
        out_shape=jax.ShapeDtypeStruct((bs, rows, page), F32),
        compiler_params=_cparams(("arbitrary",)),
        name="sample_scores",
    )(page_table, iq, w, ik_new, cache_ik)


def _sample_select_kernel(s_ref, idx_ref, code_ref, *, topk):
    key = _ordered_key(s_ref[0])
    rows, page = key.shape
    pos = (lax.broadcasted_iota(I32, key.shape, 0) * page
           + lax.broadcasted_iota(I32, key.shape, 1))

    def count(pred):
        return jnp.sum(jnp.where(pred, 1.0, 0.0), keepdims=True).astype(I32)

    like = jnp.zeros((1, 1), I32)
    thr = _kth_largest(lambda v: count(key >= v), topk, like)
    need = topk - count(key > thr)
    lim = _tie_limit(lambda p: count((key == thr) & (pos < p)), need,
                     int(rows * page - 1).bit_length() + 1, like)
    sel = ((key > thr) | ((key == thr) & (pos <= lim))) & (key > NEG_INF_KEY)

    upper = (lax.broadcasted_iota(I32, (page, page), 0)
             < lax.broadcasted_iota(I32, (page, page), 1))
    within = _dot(jnp.where(sel, 1.0, 0.0).astype(BF16), jnp.where(upper, 1.0, 0.0).astype(BF16))
    code_ref[...] = jnp.where(sel, within, -1.0)

    slot = lax.broadcasted_iota(I32, (topk, page), 0).astype(F32)
    lane = lax.broadcasted_iota(I32, (topk, page), 1)

    def row_body(r, carry):
        found, base = carry
        code = code_ref[pl.ds(r, 1), :]
        hit = (code >= 0.0) & (code + base == slot)
        found = jnp.maximum(found, jnp.where(hit, r * page + lane, -1))
        base = base + jnp.sum(jnp.where(code >= 0.0, 1.0, 0.0), axis=1, keepdims=True)
        return found, base

    found, _ = lax.fori_loop(0, rows, row_body,
                             (jnp.full((topk, page), -1, I32), jnp.zeros((1, 1), F32)))
    idx_ref[0] = jnp.max(found.astype(F32), axis=1, keepdims=True).astype(I32)


def _sample_select(scores, *, topk):
    bs, rows, page = scores.shape
    return pl.pallas_call(
        functools.partial(_sample_select_kernel, topk=topk),
        grid=(bs,),
        in_specs=[pl.BlockSpec((1, rows, page), lambda b: (b, 0, 0))],
        out_specs=pl.BlockSpec((1, topk, 1), lambda b: (b, 0, 0)),
        out_shape=jax.ShapeDtypeStruct((bs, topk, 1), I32),
        scratch_shapes=[pltpu.VMEM((rows, page), F32)],
        compiler_params=_cparams(("arbitrary",)),
        name="sample_select",
    )(scores)


def _sample_attn_kernel(pt_ref, idx_ref, idxv_ref, q_ref, kn_ref, vn_ref, ck_ref, cv_ref, o_ref,
                        kbuf_ref, vbuf_ref, sem_ref, *, page, past, topk, n_rep):
    b = pl.program_id(0)
    nb = pl.num_programs(0)
    n_kv = kn_ref.shape[1]

    def row_copies(bb, slot, j):
        pos = jnp.clip(idx_ref[bb, j], 0, past - 1)
        phys = pt_ref[bb, lax.div(pos, page)]
        off = lax.rem(pos, page)
        return (pltpu.make_async_copy(ck_ref.at[phys, off], kbuf_ref.at[slot, j], sem_ref.at[0, slot]),
                pltpu.make_async_copy(cv_ref.at[phys, off], vbuf_ref.at[slot, j], sem_ref.at[1, slot]))

    def start_all(bb, slot):
        def body(j, carry):
            ck, cv = row_copies(bb, slot, j)
            ck.start()
            cv.start()
            return carry
        lax.fori_loop(0, topk, body, 0)

    slot = b % 2

    @pl.when(b == 0)
    def _():
        start_all(b, slot)

    @pl.when(b + 1 < nb)
    def _():
        start_all(b + 1, 1 - slot)

    def wait_body(j, carry):
        ck, cv = row_copies(b, slot, j)
        ck.wait()
        cv.wait()
        return carry
    lax.fori_loop(0, topk, wait_body, 0)

    pos = idxv_ref[0]
    bias = jnp.where((pos >= 0) & (pos < past), 0.0, NEG_BIAS)
    own_sel = jnp.max(jnp.where(pos == past, 1.0, 0.0), axis=1, keepdims=True)
    own_bias = jnp.where(own_sel > 0.0, 0.0, NEG_BIAS)
    for g in range(n_kv):
        rows = slice(g * n_rep, (g + 1) * n_rep)
        qg = q_ref[0, rows, :]
        kg = kbuf_ref[slot, :, g, :].astype(BF16)
        vg = vbuf_ref[slot, :, g, :].astype(BF16)
        kn = kn_ref[0, g:g + 1, :].astype(BF16).astype(F32)
        vn = vn_ref[0, g:g + 1, :].astype(BF16).astype(F32)
        logit = _nt_dot(qg, kg) + bias
        own = jnp.sum(qg.astype(F32) * kn, axis=1, keepdims=True) + own_bias
        m = jnp.maximum(jnp.max(logit, axis=1, keepdims=True), own)
        p = jnp.exp(logit - m)
        p_own = jnp.exp(own - m)
        denom = jnp.sum(p, axis=1, keepdims=True) + p_own
        num = _dot(p.astype(BF16), vg) + p_own.astype(BF16).astype(F32) * vn
        o_ref[0, rows, :] = num / denom


def _sample_attention(page_table, idx, q, k_new, v_new, cache_k, cache_v, *, topk):
    bs, n_pages = page_table.shape
    page, n_kv, hd = cache_k.shape[1:]
    n_heads = q.shape[1]
    past = n_pages * page
    grid_spec = pltpu.PrefetchScalarGridSpec(
        num_scalar_prefetch=2,
        grid=(bs,),
        in_specs=[
            pl.BlockSpec((1, 1, topk), lambda b, pt, ix: (b, 0, 0)),
            pl.BlockSpec((1, n_heads, hd), lambda b, pt, ix: (b, 0, 0)),
            pl.BlockSpec((1, n_kv, hd), lambda b, pt, ix: (b, 0, 0)),
            pl.BlockSpec((1, n_kv, hd), lambda b, pt, ix: (b, 0, 0)),
            pl.BlockSpec(memory_space=pl.ANY),
            pl.BlockSpec(memory_space=pl.ANY),
        ],
        out_specs=pl.BlockSpec((1, n_heads, hd), lambda b, pt, ix: (b, 0, 0)),
        scratch_shapes=[pltpu.VMEM((2, topk, n_kv, hd), F32),
                        pltpu.VMEM((2, topk, n_kv, hd), F32),
                        pltpu.SemaphoreType.DMA((2, 2))],
    )
    return pl.pallas_call(
        functools.partial(_sample_attn_kernel, page=page, past=past, topk=topk,
                          n_rep=n_heads // n_kv),
        grid_spec=grid_spec,
        out_shape=jax.ShapeDtypeStruct((bs, n_heads, hd), F32),
        compiler_params=_cparams(("arbitrary",)),
        name="sample_attention",
    )(page_table, idx, idx.reshape(bs, 1, topk), q, k_new, v_new, cache_k, cache_v)


def _rope_tables(pos, dim, pad_lanes=0):
    rot = dim // ROT_DIV
    half = rot // 2
    inv = ROPE_THETA ** (-jnp.arange(half, dtype=F32) / half)
    ang = pos.astype(F32)[:, None] * inv[None, :]
    cos, sin = jnp.cos(ang), jnp.sin(ang)
    n = pos.shape[0]
    zero_h, rest0, rest1 = jnp.zeros((n, half), F32), jnp.zeros((n, dim - rot), F32), jnp.ones((n, dim - rot), F32)
    a = jnp.concatenate([cos, cos, rest1], axis=1)
    b = jnp.concatenate([-sin, zero_h, rest0], axis=1)
    c = jnp.concatenate([zero_h, sin, rest0], axis=1)
    if pad_lanes:
        a = jnp.concatenate([a, jnp.ones((n, pad_lanes), F32)], axis=1)
        b = jnp.concatenate([b, jnp.zeros((n, pad_lanes), F32)], axis=1)
        c = jnp.concatenate([c, jnp.zeros((n, pad_lanes), F32)], axis=1)
    reps = LANES // a.shape[1]
    return tuple(jnp.tile(t, (1, reps)) for t in (a, b, c))


def _split_w_in(w_in, d_conv, dq, dkv):
    sizes = (d_conv, d_conv, d_conv, dq, dkv, dkv, IDX_HEADS * IDX_DIM, IDX_DIM, IDX_HEADS)
    d_model = w_in.shape[0]
    sizes = sizes + (d_model, d_model)
    offs = np.concatenate([[0], np.cumsum(sizes)])
    parts = [w_in[:, int(offs[i]):int(offs[i + 1])].astype(BF16) for i in range(len(sizes))]
    w_x, w_b, w_c, w_q, w_k, w_v, w_iq, w_ik, w_iw, w_gc, w_ga = parts
    pad = jnp.zeros((d_model, LANES - IDX_DIM - IDX_HEADS), BF16)
    w_kvm = jnp.concatenate([w_k, w_v, w_ik, w_iw, pad], axis=1)
    return w_x, w_b, w_c, w_q, w_kvm, w_iq, w_gc, w_ga


def _mixer_common(h, gain, wts, ktabs, itabs, *, hd, dkv, tm):
    w_x, w_b, w_c, w_q, w_kvm, w_iq, w_gc, w_ga = wts
    tn = 512
    q = _rope_proj(h, gain, w_q, ktabs, half=hd // ROT_DIV // 2, scale=hd ** -0.5, tm=tm, tn=tn)
    iq = _rope_proj(h, gain, w_iq, itabs[0], half=IDX_DIM // ROT_DIV // 2, scale=IDX_DIM ** -0.5,
                    tm=tm, tn=tn)
    k, v, misc, k_bf, v_bf, ika, ikb = _kv_proj(h, gain, w_kvm, ktabs, itabs[1], dkv=dkv, tm=tm)
    gc, ga = _gate_proj(h, gain, w_gc, w_ga, tm=tm, tn=tn)
    return q, iq, k, v, misc, k_bf, v_bf, ika, ikb, gc, ga


def kernel(x_prompt, x_sample, cache_k, cache_v, cache_idx_k, state_conv, page_table,
           norm_ffn1_pre, norm_ffn1_post, w_ffn1_gate_up, w_ffn1_down,
           norm_mix_pre, norm_mix_post, w_in, w_conv, w_conv_out, w_attn_out, w_out,
           norm_ffn2_pre, norm_ffn2_post, w_ffn2_gate_up, w_ffn2_down):
    bp, t, d = x_prompt.shape
    bs, ts, _ = x_sample.shape
    depth = w_in.shape[0]
    page, n_kv, hd = cache_k.shape[2:]
    n_pages = page_table.shape[1]
    past = n_pages * page
    d_conv = w_conv.shape[2]
    dq = w_attn_out.shape[1]
    dkv = n_kv * hd
    assert bp == 1 and ts == 1 and dq == N_HEADS * hd and n_kv == N_KV_HEADS

    tm = min(512, t)
    tf = 512
    tq = sc = min(256, t)
    topk_p = min(TOPK_MAX, t // 4)
    topk_s = min(TOPK_MAX, (past + ts) // 4)

    pos_p = jnp.arange(t, dtype=jnp.int32)
    pos_s = jnp.full((bs,), past, jnp.int32)
    ktabs_p, ktabs_s = _rope_tables(pos_p, hd), _rope_tables(pos_s, hd)
    itabs_p = (_rope_tables(pos_p, IDX_DIM), _rope_tables(pos_p, IDX_DIM, LANES - IDX_DIM))
    itabs_s = (_rope_tables(pos_s, IDX_DIM), _rope_tables(pos_s, IDX_DIM, LANES - IDX_DIM))

    hp = x_prompt.reshape(t, d)
    hs = x_sample.reshape(bs, d)
    outs = [[] for _ in range(8)]
    row = lambda a: a.reshape(1, -1)
    for l in range(depth):
        w1gu, w1d = w_ffn1_gate_up[l].astype(BF16), w_ffn1_down[l].astype(BF16)
        w2gu, w2d = w_ffn2_gate_up[l].astype(BF16), w_ffn2_down[l].astype(BF16)
        wts = _split_w_in(w_in[l], d_conv, dq, dkv)
        w_co, w_ao, w_o = (w_conv_out[l].astype(BF16), w_attn_out[l].astype(BF16),
                           w_out[l].astype(BF16))
        gain_mix = row(norm_mix_pre[l])

        hp = _ffn_half(hp, row(norm_ffn1_pre[l]), row(norm_ffn1_post[l]), w1gu, w1d, tm=tm, tf=tf)
        conv_y, tail = _conv_seq(hp, gain_mix, wts[0], wts[1], wts[2], w_conv[l], tm=tm, tn=512)
        q, iq, k, v, misc, k_bf, v_bf, ika, ikb, gc, ga = _mixer_common(
            hp, gain_mix, wts, ktabs_p, itabs_p, hd=hd, dkv=dkv, tm=tm)
        wt = (misc[:, IDX_DIM:IDX_DIM + IDX_HEADS] * IDX_HEADS ** -0.5).T
        vt = v_bf.reshape(t // sc, sc, n_kv, hd).transpose(2, 0, 3, 1)
        attn_o = _prompt_attention(iq, wt, q, ika, ikb, k_bf, vt, tq=tq, sc=sc, topk=topk_p, hd=hd)
        hp = _merge(hp, conv_y, attn_o, gc, ga, row(norm_mix_post[l]), w_co, w_ao, w_o,
                    tm=tm, tn=512)
        outs[0].append(k.reshape(bp, t // page, page, n_kv, hd))
        outs[1].append(v.reshape(bp, t // page, page, n_kv, hd))
        outs[2].append(misc[:, :IDX_DIM].reshape(bp, t // page, page, IDX_DIM))
        outs[3].append(tail[tail.shape[0] - (CONV_WIDTH - 1):].reshape(bp, CONV_WIDTH - 1, d_conv))
        hp = _ffn_half(hp, row(norm_ffn2_pre[l]), row(norm_ffn2_post[l]), w2gu, w2d, tm=tm, tf=tf)

        hs = _ffn_half(hs, row(norm_ffn1_pre[l]), row(norm_ffn1_post[l]), w1gu, w1d, tm=bs, tf=tf)
        st = state_conv[l]
        conv_y, z = _conv_step(hs, gain_mix, wts[0], wts[1], wts[2], w_conv[l],
                               st[:, 0, :], st[:, 1, :], tn=512)
        q, iq, k, v, misc, k_bf, v_bf, ika, ikb, gc, ga = _mixer_common(
            hs, gain_mix, wts, ktabs_s, itabs_s, hd=hd, dkv=dkv, tm=bs)
        w_idx = (misc[:, IDX_DIM:IDX_DIM + IDX_HEADS] * IDX_HEADS ** -0.5).reshape(bs, IDX_HEADS, 1)
        scores = _sample_scores(page_table, iq.reshape(bs, IDX_HEADS, IDX_DIM), w_idx,
                                ika[:, :IDX_DIM].reshape(bs, 1, IDX_DIM), cache_idx_k[l])
        idx = _sample_select(scores, topk=topk_s).reshape(bs, topk_s)
        attn_o = _sample_attention(page_table, idx, q.reshape(bs, N_HEADS, hd),
                                   k.reshape(bs, n_kv, hd), v.reshape(bs, n_kv, hd),
                                   cache_k[l], cache_v[l], topk=topk_s)
        hs = _merge(hs, conv_y, attn_o.reshape(bs, dq).astype(BF16), gc, ga, row(norm_mix_post[l]),
                    w_co, w_ao, w_o, tm=bs, tn=512)
        outs[4].append(k.reshape(bs, ts, n_kv, hd))
        outs[5].append(v.reshape(bs, ts, n_kv, hd))
        outs[6].append(misc[:, :IDX_DIM].reshape(bs, ts, IDX_DIM))
        outs[7].append(jnp.stack([st[:, 1, :], z], axis=1))
        hs = _ffn_half(hs, row(norm_ffn2_pre[l]), row(norm_ffn2_post[l]), w2gu, w2d, tm=bs, tf=tf)

    return (hp.reshape(bp, t, d), hs.reshape(bs, ts, d)) + tuple(jnp.stack(o) for o in outs)
```

```python
import functools

import numpy as np
import jax
import jax.numpy as jnp
from jax import lax
from jax.experimental import pallas as pl
from jax.experimental.pallas import tpu as pltpu

F32 = jnp.float32
BF16 = jnp.bfloat16
I32 = jnp.int32

N_HEADS = 16
N_KV_HEADS = 4
IDX_HEADS = 16
IDX_DIM = 64
TOPK_MAX = 256
CONV_WIDTH = 3
ROPE_THETA = 500000.0
ROT_DIV = 4
EPS = 1e-6

LANES = 128
VMEM_LIMIT = 56 * 1024 * 1024
LOG2E = 1.4426950408889634
NEG_BIAS = -1e30
INT_MIN = -2 ** 31
INT_MAX = 2 ** 31 - 1
NEG_INF_KEY = int(np.int32(np.uint32(0x807FFFFF)))


def _cparams(sem):
    return pltpu.CompilerParams(dimension_semantics=sem, vmem_limit_bytes=VMEM_LIMIT)


def _rms_scale(x):
    return lax.rsqrt(jnp.mean(x * x, axis=-1, keepdims=True) + EPS)


def _nt_dot(a, b):
    return lax.dot_general(a, b, (((1,), (1,)), ((), ())), preferred_element_type=F32)


def _dot(a, b):
    return jnp.dot(a, b, preferred_element_type=F32)


def _ffn_kernel(x_ref, pre_ref, post_ref, wg_ref, wu_ref, wd_ref, o_ref, xn_ref, acc_ref):
    j = pl.program_id(1)

    @pl.when(j == 0)
    def _():
        x = x_ref[...]
        xn_ref[...] = (x * _rms_scale(x) * pre_ref[...]).astype(BF16)
        acc_ref[...] = jnp.zeros_like(acc_ref)

    xn = xn_ref[...]
    g = _dot(xn, wg_ref[...])
    u = _dot(xn, wu_ref[...])
    h = (g * jax.nn.sigmoid(g) * u).astype(BF16)
    acc_ref[...] += _dot(h, wd_ref[...])

    @pl.when(j == pl.num_programs(1) - 1)
    def _():
        y = acc_ref[...]
        o_ref[...] = x_ref[...] + 0.5 * (y * _rms_scale(y) * post_ref[...])


def _ffn_half(x, pre, post, w_gu, w_d, *, tm, tf):
    m, d = x.shape
    f = w_d.shape[0]
    nf = f // tf
    return pl.pallas_call(
        _ffn_kernel,
        grid=(m // tm, nf),
        in_specs=[
            pl.BlockSpec((tm, d), lambda i, j: (i, 0)),
            pl.BlockSpec((1, d), lambda i, j: (0, 0)),
            pl.BlockSpec((1, d), lambda i, j: (0, 0)),
            pl.BlockSpec((d, tf), lambda i, j: (0, j)),
            pl.BlockSpec((d, tf), lambda i, j: (0, nf + j)),
            pl.BlockSpec((tf, d), lambda i, j: (j, 0)),
        ],
        out_specs=pl.BlockSpec((tm, d), lambda i, j: (i, 0)),
        out_shape=jax.ShapeDtypeStruct((m, d), F32),
        scratch_shapes=[pltpu.VMEM((tm, d), BF16), pltpu.VMEM((tm, d), F32)],
        compiler_params=_cparams(("arbitrary", "arbitrary")),
        name="ffn_half",
    )(x, pre, post, w_gu, w_gu, w_d)


def _merge_kernel(h_ref, cy_ref, ao_ref, gc_ref, ga_ref, post_ref, wc_ref, wa_ref, wo_ref,
                  o_ref, acc_ref):
    j = pl.program_id(1)

    @pl.when(j == 0)
    def _():
        acc_ref[...] = jnp.zeros_like(acc_ref)

    mc = _dot(cy_ref[...], wc_ref[...])
    ma = _dot(ao_ref[...], wa_ref[...])
    mix = gc_ref[...].astype(F32) * mc + ga_ref[...].astype(F32) * ma
    acc_ref[...] += _dot(mix.astype(BF16), wo_ref[...])

    @pl.when(j == pl.num_programs(1) - 1)
    def _():
        y = acc_ref[...]
        o_ref[...] = h_ref[...] + y * _rms_scale(y) * post_ref[...]


def _merge(h, conv_y, attn_o, gc, ga, post, w_co, w_ao, w_out, *, tm, tn):
    m, d = h.shape
    return pl.pallas_call(
        _merge_kernel,
        grid=(m // tm, d // tn),
        in_specs=[
            pl.BlockSpec((tm, d), lambda i, j: (i, 0)),
            pl.BlockSpec((tm, d), lambda i, j: (i, 0)),
            pl.BlockSpec((tm, d), lambda i, j: (i, 0)),
            pl.BlockSpec((tm, tn), lambda i, j: (i, j)),
            pl.BlockSpec((tm, tn), lambda i, j: (i, j)),
            pl.BlockSpec((1, d), lambda i, j: (0, 0)),
            pl.BlockSpec((d, tn), lambda i, j: (0, j)),
            pl.BlockSpec((d, tn), lambda i, j: (0, j)),
            pl.BlockSpec((tn, d), lambda i, j: (j, 0)),
        ],
        out_specs=pl.BlockSpec((tm, d), lambda i, j: (i, 0)),
        out_shape=jax.ShapeDtypeStruct((m, d), F32),
        scratch_shapes=[pltpu.VMEM((tm, d), F32)],
        compiler_params=_cparams(("arbitrary", "arbitrary")),
        name="merge",
    )(h, conv_y, attn_o, gc, ga, post, w_co, w_ao, w_out)


def _norm_to_scratch(h_ref, gain_ref, xn_ref):
    @pl.when(pl.program_id(1) == 0)
    def _():
        x = h_ref[...]
        xn_ref[...] = (x * _rms_scale(x) * gain_ref[...]).astype(BF16)


def _rope(x, a, b, c, half):
    return x * a + pltpu.roll(x, LANES - half, axis=1) * b + pltpu.roll(x, half, axis=1) * c


def _conv_seq_kernel(h_ref, gain_ref, wx_ref, wb_ref, wc_ref, wconv_ref, cy_ref, tail_ref,
                     xn_ref, carry_ref):
    i = pl.program_id(0)
    j = pl.program_id(1)
    _norm_to_scratch(h_ref, gain_ref, xn_ref)
    xn = xn_ref[...]
    z = _dot(xn, wc_ref[...]) * _dot(xn, wx_ref[...])
    b = _dot(xn, wb_ref[...])
    tm = z.shape[0]

    @pl.when(i == 0)
    def _():
        carry_ref[j] = jnp.zeros(carry_ref.shape[1:], F32)

    prev = carry_ref[j]
    row = lax.broadcasted_iota(I32, z.shape, 0)
    z1 = jnp.where(row == 0, prev[1:2, :], pltpu.roll(z, 1, axis=0))
    z2 = jnp.where(row == 0, prev[0:1, :],
                   jnp.where(row == 1, prev[1:2, :], pltpu.roll(z, 2, axis=0)))
    w = wconv_ref[...]
    cy_ref[...] = (b * (z2 * w[0:1, :] + z1 * w[1:2, :] + z * w[2:3, :])).astype(BF16)
    tail = z[tm - 8:, :]
    carry_ref[j] = pltpu.roll(tail, 2, axis=0)
    tail_ref[...] = tail


def _conv_seq(h, gain, w_x, w_b, w_c, w_conv, *, tm, tn):
    m, d = h.shape
    dc = w_x.shape[1]
    wspec = pl.BlockSpec((d, tn), lambda i, j: (0, j))
    return pl.pallas_call(
        _conv_seq_kernel,
        grid=(m // tm, dc // tn),
        in_specs=[
            pl.BlockSpec((tm, d), lambda i, j: (i, 0)),
            pl.BlockSpec((1, d), lambda i, j: (0, 0)),
            wspec, wspec, wspec,
            pl.BlockSpec((CONV_WIDTH, tn), lambda i, j: (0, j)),
        ],
        out_specs=[
            pl.BlockSpec((tm, tn), lambda i, j: (i, j)),
            pl.BlockSpec((8, tn), lambda i, j: (i, j)),
        ],
        out_shape=[jax.ShapeDtypeStruct((m, dc), BF16), jax.ShapeDtypeStruct((m // tm * 8, dc), F32)],
        scratch_shapes=[pltpu.VMEM((tm, d), BF16), pltpu.VMEM((dc // tn, 8, tn), F32)],
        compiler_params=_cparams(("arbitrary", "arbitrary")),
        name="conv_seq",
    )(h, gain, w_x, w_b, w_c, w_conv)


def _conv_step_kernel(h_ref, gain_ref, wx_ref, wb_ref, wc_ref, wconv_ref, s0_ref, s1_ref,
                      cy_ref, z_ref, xn_ref):
    _norm_to_scratch(h_ref, gain_ref, xn_ref)
    xn = xn_ref[...]
    z = _dot(xn, wc_ref[...]) * _dot(xn, wx_ref[...])
    b = _dot(xn, wb_ref[...])
    w = wconv_ref[...]
    cy_ref[...] = (b * (s0_ref[...] * w[0:1, :] + s1_ref[...] * w[1:2, :] + z * w[2:3, :])
                   ).astype(BF16)
    z_ref[...] = z


def _conv_step(h, gain, w_x, w_b, w_c, w_conv, s0, s1, *, tn):
    m, d = h.shape
    dc = w_x.shape[1]
    wspec = pl.BlockSpec((d, tn), lambda i, j: (0, j))
    cspec = pl.BlockSpec((m, tn), lambda i, j: (0, j))
    return pl.pallas_call(
        _conv_step_kernel,
        grid=(1, dc // tn),
        in_specs=[
            pl.BlockSpec((m, d), lambda i, j: (0, 0)),
            pl.BlockSpec((1, d), lambda i, j: (0, 0)),
            wspec, wspec, wspec,
            pl.BlockSpec((CONV_WIDTH, tn), lambda i, j: (0, j)),
            cspec, cspec,
        ],
        out_specs=[cspec, cspec],
        out_shape=[jax.ShapeDtypeStruct((m, dc), BF16), jax.ShapeDtypeStruct((m, dc), F32)],
        scratch_shapes=[pltpu.VMEM((m, d), BF16)],
        compiler_params=_cparams(("arbitrary", "arbitrary")),
        name="conv_step",
    )(h, gain, w_x, w_b, w_c, w_conv, s0, s1)


def _rope_proj_kernel(h_ref, gain_ref, w_ref, a_ref, b_ref, c_ref, o_ref, xn_ref, *, half, scale):
    _norm_to_scratch(h_ref, gain_ref, xn_ref)
    y = _dot(xn_ref[...], w_ref[...])
    a, b, c = a_ref[...], b_ref[...], c_ref[...]
    for g in range(y.shape[1] // LANES):
        sl = slice(g * LANES, (g + 1) * LANES)
        o_ref[:, sl] = (_rope(y[:, sl], a, b, c, half) * scale).astype(o_ref.dtype)


def _rope_proj(h, gain, w, tabs, *, half, scale, tm, tn):
    m, d = h.shape
    n = w.shape[1]
    tspec = pl.BlockSpec((tm, LANES), lambda i, j: (i, 0))
    return pl.pallas_call(
        functools.partial(_rope_proj_kernel, half=half, scale=scale),
        grid=(m // tm, n // tn),
        in_specs=[
            pl.BlockSpec((tm, d), lambda i, j: (i, 0)),
            pl.BlockSpec((1, d), lambda i, j: (0, 0)),
            pl.BlockSpec((d, tn), lambda i, j: (0, j)),
            tspec, tspec, tspec,
        ],
        out_specs=pl.BlockSpec((tm, tn), lambda i, j: (i, j)),
        out_shape=jax.ShapeDtypeStruct((m, n), BF16),
        scratch_shapes=[pltpu.VMEM((tm, d), BF16)],
        compiler_params=_cparams(("arbitrary", "arbitrary")),
        name="rope_proj",
    )(h, gain, w, *tabs)


def _kv_proj_kernel(h_ref, gain_ref, w_ref, ka_ref, kb_ref, kc_ref, ia_ref, ib_ref, ic_ref,
                    k_ref, v_ref, misc_ref, kbf_ref, vbf_ref, ika_ref, ikb_ref, *, dkv):
    x = h_ref[...]
    xn = (x * _rms_scale(x) * gain_ref[...]).astype(BF16)
    y = _dot(xn, w_ref[...])
    a, b, c = ka_ref[...], kb_ref[...], kc_ref[...]
    for g in range(dkv // LANES):
        sl = slice(g * LANES, (g + 1) * LANES)
        kg = _rope(y[:, sl], a, b, c, LANES // ROT_DIV // 2)
        k_ref[:, sl] = kg
        kbf_ref[:, sl] = kg.astype(BF16)
    v = y[:, dkv:2 * dkv]
    v_ref[...] = v
    vbf_ref[...] = v.astype(BF16)
    misc = _rope(y[:, 2 * dkv:], ia_ref[...], ib_ref[...], ic_ref[...], IDX_DIM // ROT_DIV // 2)
    misc_ref[...] = misc
    lane = lax.broadcasted_iota(I32, misc.shape, 1)
    ik_lo = jnp.where(lane < IDX_DIM, misc, 0.0)
    ika_ref[...] = ik_lo.astype(BF16)
    ikb_ref[...] = pltpu.roll(ik_lo, IDX_DIM, axis=1).astype(BF16)


def _kv_proj(h, gain, w_kvm, ktabs, itabs, *, dkv, tm):
    m, d = h.shape
    n = w_kvm.shape[1]
    tspec = pl.BlockSpec((tm, LANES), lambda i: (i, 0))
    kvspec = pl.BlockSpec((tm, dkv), lambda i: (i, 0))
    mspec = pl.BlockSpec((tm, LANES), lambda i: (i, 0))
    return pl.pallas_call(
        functools.partial(_kv_proj_kernel, dkv=dkv),
        grid=(m // tm,),
        in_specs=[
            pl.BlockSpec((tm, d), lambda i: (i, 0)),
            pl.BlockSpec((1, d), lambda i: (0, 0)),
            pl.BlockSpec((d, n), lambda i: (0, 0)),
            tspec, tspec, tspec, tspec, tspec, tspec,
        ],
        out_specs=[kvspec, kvspec, mspec, kvspec, kvspec, mspec, mspec],
        out_shape=[
            jax.ShapeDtypeStruct((m, dkv), F32), jax.ShapeDtypeStruct((m, dkv), F32),
            jax.ShapeDtypeStruct((m, LANES), F32),
            jax.ShapeDtypeStruct((m, dkv), BF16), jax.ShapeDtypeStruct((m, dkv), BF16),
            jax.ShapeDtypeStruct((m, LANES), BF16), jax.ShapeDtypeStruct((m, LANES), BF16),
        ],
        compiler_params=_cparams(("arbitrary",)),
        name="kv_proj",
    )(h, gain, w_kvm, *ktabs, *itabs)


def _gate_proj_kernel(h_ref, gain_ref, wc_ref, wa_ref, gc_ref, ga_ref, xn_ref):
    _norm_to_scratch(h_ref, gain_ref, xn_ref)
    xn = xn_ref[...]
    gc_ref[...] = jax.nn.sigmoid(_dot(xn, wc_ref[...])).astype(BF16)
    ga_ref[...] = jax.nn.sigmoid(_dot(xn, wa_ref[...])).astype(BF16)


def _gate_proj(h, gain, w_gc, w_ga, *, tm, tn):
    m, d = h.shape
    n = w_gc.shape[1]
    wspec = pl.BlockSpec((d, tn), lambda i, j: (0, j))
    ospec = pl.BlockSpec((tm, tn), lambda i, j: (i, j))
    return pl.pallas_call(
        _gate_proj_kernel,
        grid=(m // tm, n // tn),
        in_specs=[
            pl.BlockSpec((tm, d), lambda i, j: (i, 0)),
            pl.BlockSpec((1, d), lambda i, j: (0, 0)),
            wspec, wspec,
        ],
        out_specs=[ospec, ospec],
        out_shape=[jax.ShapeDtypeStruct((m, n), BF16), jax.ShapeDtypeStruct((m, n), BF16)],
        scratch_shapes=[pltpu.VMEM((tm, d), BF16)],
        compiler_params=_cparams(("arbitrary", "arbitrary")),
        name="gate_proj",
    )(h, gain, w_gc, w_ga)


def _ordered_key(score):
    bits = pltpu.bitcast(score, I32)
    return bits ^ ((bits >> 31) & INT_MAX)


def _kth_largest(count_ge, k, like):
    def body(b, lo):
        cand = lo + jnp.left_shift(jnp.int32(1), 31 - b)
        return jnp.where(count_ge(cand) >= k, cand, lo)
    return lax.fori_loop(0, 32, body, jnp.full_like(like, INT_MIN))


def _tie_limit(count_eq_before, need, nbits, like):
    def body(b, lo):
        cand = lo + jnp.left_shift(jnp.int32(1), nbits - 1 - b)
        return jnp.where(count_eq_before(cand) < need, cand, lo)
    return lax.fori_loop(0, nbits, body, jnp.zeros_like(like))


def _prompt_attn_kernel(iq_ref, wt_ref, q_ref, ika_ref, ikb_ref, k_ref, vt_ref, o_ref,
                        key_ref, half_ref, lim_ref, q4_ref, m_ref, acc_ref, lga_ref, lgb_ref,
                        *, tq, sc, topk, n_rep, hd, hb):
    i = pl.program_id(0)
    t0 = i * tq
    n_chunks = (t0 + tq) // sc
    n_pairs = iq_ref.shape[1] // LANES
    n_kv = k_ref.shape[1] // hd
    pos_bits = int(k_ref.shape[0] - 1).bit_length()
    i16_min = -2 ** 15

    def rows(c):
        return pl.ds(pl.multiple_of(c * sc, sc), sc)

    def score_chunk(c, carry):
        ka = ika_ref[rows(c), :]
        kb = ikb_ref[rows(c), :]
        acc = jnp.zeros((sc, tq), F32)
        for p in range(n_pairs):
            iq_p = iq_ref[:, p * LANES:(p + 1) * LANES]
            acc += jnp.maximum(_nt_dot(ka, iq_p), 0.0) * wt_ref[2 * p:2 * p + 1, :]
            acc += jnp.maximum(_nt_dot(kb, iq_p), 0.0) * wt_ref[2 * p + 1:2 * p + 2, :]
        spos = c * sc + lax.broadcasted_iota(I32, (sc, tq), 0)
        tpos = t0 + lax.broadcasted_iota(I32, (sc, tq), 1)
        key = _ordered_key(jnp.where(spos <= tpos, acc, -jnp.inf))
        key_ref[rows(c), :] = key
        half_ref[rows(c), :] = (key >> 16).astype(jnp.int16)
        return carry

    lax.fori_loop(0, n_chunks, score_chunk, 0)

    def half_count(cand, strict=False):
        cand16 = cand.astype(jnp.int16)
        def body(c, cnt):
            blk = half_ref[rows(c), :]
            hit = jnp.where(blk > cand16 if strict else blk >= cand16,
                            jnp.bfloat16(1), jnp.bfloat16(0))
            part = hit[0:16]
            for r in range(1, sc // 16):
                part = part + hit[r * 16:(r + 1) * 16]
            return cnt + part.astype(F32)
        cnt = lax.fori_loop(0, n_chunks, body, jnp.zeros((16, tq), F32))
        return cnt.sum(axis=0, keepdims=True).astype(I32)

    def half_search(k):
        def body(b, carry):
            lo, cnt_lo = carry
            cand = lo + jnp.left_shift(jnp.int32(1), 15 - b)
            cnt = half_count(cand)
            ok = cnt >= k
            return jnp.where(ok, cand, lo), jnp.where(ok, cnt, cnt_lo)
        init = (jnp.full((1, tq), i16_min, I32), jnp.full((1, tq), n_chunks * sc, I32))
        return lax.fori_loop(0, 16, body, init)

    thr_hi, _ = half_search(topk)
    n_gt_hi = half_count(thr_hi, strict=True)

    def low_chunk(c, carry):
        key = key_ref[rows(c), :]
        low = (key & 0xFFFF) - 2 ** 15
        half_ref[rows(c), :] = jnp.where((key >> 16) == thr_hi, low, i16_min).astype(jnp.int16)
        return carry

    lax.fori_loop(0, n_chunks, low_chunk, 0)
    thr_lo, n_ge_lo = half_search(topk - n_gt_hi)
    thr = thr_hi * 65536 + (thr_lo + 2 ** 15)
    n_ge = n_gt_hi + n_ge_lo
    has_tie = jnp.max(jnp.where((n_ge > topk) & (thr > NEG_INF_KEY), 1.0, 0.0)) > 0.0
    lim_ref[...] = jnp.full(lim_ref.shape, INT_MAX, I32)

    @pl.when(has_tie)
    def _():
        def column_count(pred):
            def body(c, cnt):
                blk = key_ref[rows(c), :]
                spos = c * sc + lax.broadcasted_iota(I32, (sc, tq), 0)
                hit = jnp.where(pred(blk, spos), 1, 0).astype(I32)
                return cnt + hit.reshape(sc // 8, 8, tq).sum(axis=0)
            cnt = lax.fori_loop(0, n_chunks, body, jnp.zeros((8, tq), I32))
            return cnt.astype(F32).sum(axis=0, keepdims=True).astype(I32)

        need = topk - column_count(lambda blk, spos: blk > thr)
        lim = _tie_limit(
            lambda p: column_count(lambda blk, spos: (blk == thr) & (spos < p)),
            need, pos_bits + 1, jnp.zeros((1, tq), I32))
        lim_ref[...] = jnp.broadcast_to(lim, lim_ref.shape)

    lim = lim_ref[0:1, :]

    def bias_chunk(c, carry):
        blk = key_ref[rows(c), :]
        spos = c * sc + lax.broadcasted_iota(I32, (sc, tq), 0)
        sel = ((blk > thr) | ((blk == thr) & (spos <= lim))) & (blk > NEG_INF_KEY)
        key_ref[rows(c), :] = pltpu.bitcast(jnp.where(sel, 0.0, NEG_BIAS).astype(F32), I32)
        return carry

    lax.fori_loop(0, n_chunks, bias_chunk, 0)

    n_heads = n_kv * n_rep
    n_grp = n_heads // hb
    for j in range(n_grp):
        for r in range(hb):
            head = j * hb + r
            q4_ref[j, r * tq:(r + 1) * tq, :] = q_ref[:, head * hd:(head + 1) * hd]
    m_ref[...] = jnp.full(m_ref.shape, NEG_BIAS, F32)
    acc_ref[...] = jnp.zeros(acc_ref.shape, F32)

    def qk_store(c, j, dst_ref):
        g = j * hb // n_rep
        dst_ref[...] = _nt_dot(k_ref[rows(c), g * hd:(g + 1) * hd], q4_ref[j])

    bufs = (lga_ref, lgb_ref)
    qk_store(0, 0, bufs[0])

    def attn_chunk(c, carry):
        bias = pltpu.bitcast(key_ref[rows(c), :], F32)
        bias_w = jnp.concatenate([bias] * hb, axis=1)
        c_next = jnp.minimum(c + 1, n_chunks - 1)
        for j in range(n_grp):
            cur, nxt = bufs[j % 2], bufs[(j + 1) % 2]
            if j + 1 < n_grp:
                qk_store(c, j + 1, nxt)
            else:
                qk_store(c_next, 0, nxt)
            logit = cur[...] + bias_w
            m_old = m_ref[j]
            m_new = jnp.maximum(m_old, jnp.max(logit, axis=0, keepdims=True))
            p = jnp.exp2(logit - m_new).astype(BF16)
            acc_ref[j] = (jnp.exp2(m_old - m_new) * acc_ref[j]
                          + _dot(vt_ref[j * hb // n_rep, c], p))
            m_ref[j] = m_new
        return carry

    lax.fori_loop(0, n_chunks, attn_chunk, 0)
    for j in range(n_grp):
        for r in range(hb):
            head = j * hb + r
            cols = slice(r * tq, (r + 1) * tq)
            out_t = acc_ref[j, 0:hd, cols] / acc_ref[j, hd:hd + 1, cols]
            o_ref[:, head * hd:(head + 1) * hd] = out_t.T.astype(o_ref.dtype)


def _prompt_attention(iq, wt, q, ika, ikb, k_bf, vt, *, tq, sc, topk, hd):
    t, dq = q.shape
    n_kv = k_bf.shape[1] // hd
    n_rep = dq // hd // n_kv
    assert tq >= topk and tq % sc == 0 and t % tq == 0
    hb = 4
    assert (n_kv * n_rep // hb) % 2 == 0
    n_grp = n_kv * n_rep // hb
    resident = functools.partial(pl.BlockSpec, pipeline_mode=pl.Buffered(1))
    return pl.pallas_call(
        functools.partial(_prompt_attn_kernel, tq=tq, sc=sc, topk=topk, n_rep=n_rep, hd=hd, hb=hb),
        grid=(t // tq,),
        in_specs=[
            pl.BlockSpec((tq, iq.shape[1]), lambda i: (i, 0)),
            pl.BlockSpec((wt.shape[0], tq), lambda i: (0, i)),
            pl.BlockSpec((tq, dq), lambda i: (i, 0)),
            resident(ika.shape, lambda i: (0, 0)),
            resident(ikb.shape, lambda i: (0, 0)),
            resident(k_bf.shape, lambda i: (0, 0)),
            resident(vt.shape, lambda i: (0, 0, 0, 0)),
        ],
        out_specs=pl.BlockSpec((tq, dq), lambda i: (i, 0)),
        out_shape=jax.ShapeDtypeStruct((t, dq), BF16),
        scratch_shapes=[
            pltpu.VMEM((t, tq), I32),
            pltpu.VMEM((t, tq), jnp.int16),
            pltpu.VMEM((8, tq), I32),
            pltpu.VMEM((n_grp, hb * tq, hd), BF16),
            pltpu.VMEM((n_grp, 1, hb * tq), F32),
            pltpu.VMEM((n_grp, vt.shape[2], hb * tq), F32),
            pltpu.VMEM((sc, hb * tq), F32),
            pltpu.VMEM((sc, hb * tq), F32),
        ],
        compiler_params=_cparams(("arbitrary",)),
        name="prompt_attention",
    )(iq, wt, q, ika, ikb, k_bf, vt)


def _sample_score_kernel(pt_ref, iq_ref, w_ref, ikn_ref, cache_ref, o_ref, buf_ref, sem_ref,
                         *, n_pages, page):
    b = pl.program_id(0)
    nb = pl.num_programs(0)

    def page_copy(bb, slot, p):
        return pltpu.make_async_copy(cache_ref.at[pt_ref[bb, p]], buf_ref.at[slot, p],
                                     sem_ref.at[slot])

    def start_all(bb, slot):
        def body(p, carry):
            page_copy(bb, slot, p).start()
            return carry
        lax.fori_loop(0, n_pages, body, 0)

    slot = b % 2

    @pl.when(b == 0)
    def _():
        start_all(b, slot)

    @pl.when(b + 1 < nb)
    def _():
        start_all(b + 1, 1 - slot)

    def wait_body(p, carry):
        page_copy(b, slot, p).wait()
        return carry
    lax.fori_loop(0, n_pages, wait_body, 0)

    iq = iq_ref[0]
    w = w_ref[0]

    def head_sum(keys_t):
        s = jnp.maximum(_dot(iq, keys_t), 0.0) * w
        return jnp.sum(s, axis=0, keepdims=True)

    group = 8

    def group_body(gi, carry):
        p0 = pl.multiple_of(gi * group, group)
        blk = buf_ref[slot, pl.ds(p0, group)]
        keys_t = jnp.concatenate([blk[r] for r in range(group)], axis=1).astype(BF16)
        s = head_sum(keys_t)
        for r in range(group):
            o_ref[0, pl.ds(p0 + r, 1), :] = s[:, r * page:(r + 1) * page]
        return carry
    lax.fori_loop(0, n_pages // group, group_body, 0)
    own = head_sum(jnp.broadcast_to(ikn_ref[0], (ikn_ref.shape[1], page)))
    lane = lax.broadcasted_iota(I32, (1, page), 1)
    o_ref[0, pl.ds(n_pages, 1), :] = jnp.where(lane == 0, own, -jnp.inf)
    o_ref[0, pl.ds(n_pages + 1, 7), :] = jnp.full((7, page), -jnp.inf, F32)


def _sample_scores(page_table, iq, w, ik_new, cache_ik_t):
    bs, n_pages = page_table.shape
    idim, page = cache_ik_t.shape[1:]
    rows = n_pages + 8
    grid_spec = pltpu.PrefetchScalarGridSpec(
        num_scalar_prefetch=1,
        grid=(bs,),
        in_specs=[
            pl.BlockSpec((1,) + iq.shape[1:], lambda b, pt: (b, 0, 0)),
            pl.BlockSpec((1,) + w.shape[1:], lambda b, pt: (b, 0, 0)),
            pl.BlockSpec((1,) + ik_new.shape[1:], lambda b, pt: (b, 0, 0)),
            pl.BlockSpec(memory_space=pl.ANY),
        ],
        out_specs=pl.BlockSpec((1, rows, page), lambda b, pt: (b, 0, 0)),
        scratch_shapes=[pltpu.VMEM((2, n_pages, idim, page), F32),
                        pltpu.SemaphoreType.DMA((2,))],
    )
    return pl.pallas_call(
        functools.partial(_sample_score_kernel, n_pages=n_pages, page=page),
        grid_spec=grid_spec,
        out_shape=jax.ShapeDtypeStruct((bs, rows, page), F32),
        compiler_params=_cparams(("arbitrary",)),
        name="sample_scores",
    )(page_table, iq, w, ik_new, cache_ik_t)


def _slab_pos(shape):
    nd = len(shape)
    return (lax.broadcasted_iota(I32, shape, nd - 2) * shape[-1]
            + lax.broadcasted_iota(I32, shape, nd - 1))


def _sample_thresh_kernel(s_ref, thr_ref, lim_ref, *, topk):
    key = _ordered_key(s_ref[...])
    bs, rows, page = key.shape
    pos = _slab_pos(key.shape)

    def count(pred):
        c = jnp.sum(jnp.where(pred, 1.0, 0.0), axis=1, keepdims=True)
        return jnp.sum(c, axis=2, keepdims=True).astype(I32)

    like = jnp.zeros((bs, 1, 1), I32)
    thr = _kth_largest(lambda v: count(key >= v), topk, like)
    need = topk - count(key > thr)
    lim = _tie_limit(lambda p: count((key == thr) & (pos < p)), need,
                     int(rows * page - 1).bit_length() + 1, like)
    thr_ref[...] = thr
    lim_ref[...] = lim


def _sample_compact_kernel(s_ref, thr_ref, lim_ref, idx_ref, code_ref, *, topk):
    key = _ordered_key(s_ref[0])
    rows, page = key.shape
    thr, lim = thr_ref[0], lim_ref[0]
    sel = ((key > thr) | ((key == thr) & (_slab_pos(key.shape) <= lim))) & (key > NEG_INF_KEY)
    self32 = jnp.where(sel, 1.0, 0.0)

    upper = (lax.broadcasted_iota(I32, (page, page), 0)
             < lax.broadcasted_iota(I32, (page, page), 1))
    within = _dot(self32.astype(BF16), jnp.where(upper, 1.0, 0.0).astype(BF16))
    kpad = 256
    row_tot = jnp.broadcast_to(jnp.sum(self32, axis=1, keepdims=True), (rows, page))
    row_tot = jnp.concatenate([row_tot, jnp.zeros((kpad - rows, page), F32)], axis=0)
    earlier = (lax.broadcasted_iota(I32, (rows, kpad), 1)
               < lax.broadcasted_iota(I32, (rows, kpad), 0))
    before = _dot(jnp.where(earlier, 1.0, 0.0).astype(BF16), row_tot.astype(BF16))
    code_ref[...] = jnp.where(sel, within + before, -1.0)

    slot = lax.broadcasted_iota(I32, (topk, page), 0).astype(F32)
    lane = lax.broadcasted_iota(I32, (topk, page), 1)
    unroll = 8

    def row_body(r8, found):
        for u in range(unroll):
            r = r8 * unroll + u
            hit = code_ref[pl.ds(r, 1), :] == slot
            found = jnp.maximum(found, jnp.where(hit, r * page + lane, -1))
        return found

    found = lax.fori_loop(0, rows // unroll, row_body, jnp.full((topk, page), -1, I32))
    idx_ref[0] = jnp.max(found.astype(F32), axis=1, keepdims=True).astype(I32)


def _sample_select(scores, *, topk):
    bs, rows, page = scores.shape
    assert rows % 8 == 0 and rows <= 256
    one = pl.BlockSpec((1, 1, 1), lambda b: (b, 0, 0))
    thr, lim = pl.pallas_call(
        functools.partial(_sample_thresh_kernel, topk=topk),
        out_shape=[jax.ShapeDtypeStruct((bs, 1, 1), I32)] * 2,
        compiler_params=pltpu.CompilerParams(vmem_limit_bytes=VMEM_LIMIT),
        name="sample_thresh",
    )(scores)
    return pl.pallas_call(
        functools.partial(_sample_compact_kernel, topk=topk),
        grid=(bs,),
        in_specs=[pl.BlockSpec((1, rows, page), lambda b: (b, 0, 0)), one, one],
        out_specs=pl.BlockSpec((1, topk, 1), lambda b: (b, 0, 0)),
        out_shape=jax.ShapeDtypeStruct((bs, topk, 1), I32),
        scratch_shapes=[pltpu.VMEM((rows, page), F32)],
        compiler_params=_cparams(("arbitrary",)),
        name="sample_compact",
    )(scores, thr, lim)


def _sample_attn_kernel(pt_ref, idx_ref, idxv_ref, q_ref, kn_ref, vn_ref, ck_ref, cv_ref, o_ref,
                        kbuf_ref, vbuf_ref, sem_ref, *, page, past, topk, n_rep):
    b = pl.program_id(0)
    nb = pl.num_programs(0)
    n_kv = kn_ref.shape[1]

    def row_copies(bb, slot, j):
        pos = jnp.clip(idx_ref[bb, j], 0, past - 1)
        phys = pt_ref[bb, lax.div(pos, page)]
        off = lax.rem(pos, page)
        return (pltpu.make_async_copy(ck_ref.at[phys, off], kbuf_ref.at[slot, j], sem_ref.at[0, slot]),
                pltpu.make_async_copy(cv_ref.at[phys, off], vbuf_ref.at[slot, j], sem_ref.at[1, slot]))

    unroll = 8

    def start_all(bb, slot):
        def body(j8, carry):
            for u in range(unroll):
                ck, cv = row_copies(bb, slot, j8 * unroll + u)
                ck.start()
                cv.start()
            return carry
        lax.fori_loop(0, topk // unroll, body, 0)

    slot = b % 2

    @pl.when(b == 0)
    def _():
        start_all(b, slot)

    @pl.when(b + 1 < nb)
    def _():
        start_all(b + 1, 1 - slot)

    for h in range(topk // page):
        slab = pl.ds(h * page, page)
        pltpu.make_async_copy(ck_ref.at[0], kbuf_ref.at[slot, slab], sem_ref.at[0, slot]).wait()
        pltpu.make_async_copy(cv_ref.at[0], vbuf_ref.at[slot, slab], sem_ref.at[1, slot]).wait()

    pos = idxv_ref[0]
    bias = jnp.where((pos >= 0) & (pos < past), 0.0, NEG_BIAS)
    own_sel = jnp.max(jnp.where(pos == past, 1.0, 0.0), axis=1, keepdims=True)
    own_bias = jnp.where(own_sel > 0.0, 0.0, NEG_BIAS)
    for g in range(n_kv):
        rows = slice(g * n_rep, (g + 1) * n_rep)
        qg = q_ref[0, rows, :]
        kg = kbuf_ref[slot, :, g, :].astype(BF16)
        vg = vbuf_ref[slot, :, g, :].astype(BF16)
        kn = kn_ref[0, g:g + 1, :].astype(BF16).astype(F32)
        vn = vn_ref[0, g:g + 1, :].astype(BF16).astype(F32)
        logit = _nt_dot(qg, kg) + bias
        own = jnp.sum(qg.astype(F32) * kn, axis=1, keepdims=True) + own_bias
        m = jnp.maximum(jnp.max(logit, axis=1, keepdims=True), own)
        p = jnp.exp2(logit - m)
        p_own = jnp.exp2(own - m)
        denom = jnp.sum(p, axis=1, keepdims=True) + p_own
        num = _dot(p.astype(BF16), vg) + p_own.astype(BF16).astype(F32) * vn
        o_ref[0, rows, :] = num / denom


def _sample_attention(page_table, idx, q, k_new, v_new, cache_k, cache_v, *, topk):
    bs, n_pages = page_table.shape
    page, n_kv, hd = cache_k.shape[1:]
    n_heads = q.shape[1]
    past = n_pages * page
    assert topk % page == 0 and topk % 8 == 0
    grid_spec = pltpu.PrefetchScalarGridSpec(
        num_scalar_prefetch=2,
        grid=(bs,),
        in_specs=[
            pl.BlockSpec((1, 1, topk), lambda b, pt, ix: (b, 0, 0)),
            pl.BlockSpec((1, n_heads, hd), lambda b, pt, ix: (b, 0, 0)),
            pl.BlockSpec((1, n_kv, hd), lambda b, pt, ix: (b, 0, 0)),
            pl.BlockSpec((1, n_kv, hd), lambda b, pt, ix: (b, 0, 0)),
            pl.BlockSpec(memory_space=pl.ANY),
            pl.BlockSpec(memory_space=pl.ANY),
        ],
        out_specs=pl.BlockSpec((1, n_heads, hd), lambda b, pt, ix: (b, 0, 0)),
        scratch_shapes=[pltpu.VMEM((2, topk, n_kv, hd), F32),
                        pltpu.VMEM((2, topk, n_kv, hd), F32),
                        pltpu.SemaphoreType.DMA((2, 2))],
    )
    return pl.pallas_call(
        functools.partial(_sample_attn_kernel, page=page, past=past, topk=topk,
                          n_rep=n_heads // n_kv),
        grid_spec=grid_spec,
        out_shape=jax.ShapeDtypeStruct((bs, n_heads, hd), F32),
        compiler_params=_cparams(("arbitrary",)),
        name="sample_attention",
    )(page_table, idx, idx.reshape(bs, 1, topk), q, k_new, v_new, cache_k, cache_v)


def _rope_tables(pos, dim, pad_lanes=0):
    rot = dim // ROT_DIV
    half = rot // 2
    inv = ROPE_THETA ** (-jnp.arange(half, dtype=F32) / half)
    ang = pos.astype(F32)[:, None] * inv[None, :]
    cos, sin = jnp.cos(ang), jnp.sin(ang)
    n = pos.shape[0]
    zero_h, rest0, rest1 = jnp.zeros((n, half), F32), jnp.zeros((n, dim - rot), F32), jnp.ones((n, dim - rot), F32)
    a = jnp.concatenate([cos, cos, rest1], axis=1)
    b = jnp.concatenate([-sin, zero_h, rest0], axis=1)
    c = jnp.concatenate([zero_h, sin, rest0], axis=1)
    if pad_lanes:
        a = jnp.concatenate([a, jnp.ones((n, pad_lanes), F32)], axis=1)
        b = jnp.concatenate([b, jnp.zeros((n, pad_lanes), F32)], axis=1)
        c = jnp.concatenate([c, jnp.zeros((n, pad_lanes), F32)], axis=1)
    reps = LANES // a.shape[1]
    return tuple(jnp.tile(t, (1, reps)) for t in (a, b, c))


def _split_w_in(w_in, d_conv, dq, dkv):
    sizes = (d_conv, d_conv, d_conv, dq, dkv, dkv, IDX_HEADS * IDX_DIM, IDX_DIM, IDX_HEADS)
    d_model = w_in.shape[0]
    sizes = sizes + (d_model, d_model)
    offs = np.concatenate([[0], np.cumsum(sizes)])
    parts = [w_in[:, int(offs[i]):int(offs[i + 1])].astype(BF16) for i in range(len(sizes))]
    w_x, w_b, w_c, w_q, w_k, w_v, w_iq, w_ik, w_iw, w_gc, w_ga = parts
    pad = jnp.zeros((d_model, LANES - IDX_DIM - IDX_HEADS), BF16)
    w_kvm = jnp.concatenate([w_k, w_v, w_ik, w_iw, pad], axis=1)
    return w_x, w_b, w_c, w_q, w_kvm, w_iq, w_gc, w_ga


def _mixer_common(h, gain, wts, ktabs, itabs, *, hd, dkv, tm):
    w_x, w_b, w_c, w_q, w_kvm, w_iq, w_gc, w_ga = wts
    tn = 512
    q = _rope_proj(h, gain, w_q, ktabs, half=hd // ROT_DIV // 2, scale=hd ** -0.5 * LOG2E, tm=tm, tn=tn)
    iq = _rope_proj(h, gain, w_iq, itabs[0], half=IDX_DIM // ROT_DIV // 2, scale=IDX_DIM ** -0.5,
                    tm=tm, tn=tn)
    k, v, misc, k_bf, v_bf, ika, ikb = _kv_proj(h, gain, w_kvm, ktabs, itabs[1], dkv=dkv, tm=tm)
    gc, ga = _gate_proj(h, gain, w_gc, w_ga, tm=tm, tn=tn)
    return q, iq, k, v, misc, k_bf, v_bf, ika, ikb, gc, ga


def kernel(x_prompt, x_sample, cache_k, cache_v, cache_idx_k, state_conv, page_table,
           norm_ffn1_pre, norm_ffn1_post, w_ffn1_gate_up, w_ffn1_down,
           norm_mix_pre, norm_mix_post, w_in, w_conv, w_conv_out, w_attn_out, w_out,
           norm_ffn2_pre, norm_ffn2_post, w_ffn2_gate_up, w_ffn2_down):
    bp, t, d = x_prompt.shape
    bs, ts, _ = x_sample.shape
    depth = w_in.shape[0]
    page, n_kv, hd = cache_k.shape[2:]
    n_pages = page_table.shape[1]
    past = n_pages * page
    d_conv = w_conv.shape[2]
    dq = w_attn_out.shape[1]
    dkv = n_kv * hd
    assert bp == 1 and ts == 1 and dq == N_HEADS * hd and n_kv == N_KV_HEADS

    tm = min(512, t)
    tf = 512
    tq = sc = min(256, t)
    topk_p = min(TOPK_MAX, t // 4)
    topk_s = min(TOPK_MAX, (past + ts) // 4)

    pos_p = jnp.arange(t, dtype=jnp.int32)
    pos_s = jnp.full((bs,), past, jnp.int32)
    ktabs_p, ktabs_s = _rope_tables(pos_p, hd), _rope_tables(pos_s, hd)
    itabs_p = (_rope_tables(pos_p, IDX_DIM), _rope_tables(pos_p, IDX_DIM, LANES - IDX_DIM))
    itabs_s = (_rope_tables(pos_s, IDX_DIM), _rope_tables(pos_s, IDX_DIM, LANES - IDX_DIM))

    hp = x_prompt.reshape(t, d)
    hs = x_sample.reshape(bs, d)
    outs = [[] for _ in range(8)]
    row = lambda a: a.reshape(1, -1)
    for l in range(depth):
        w1gu, w1d = w_ffn1_gate_up[l].astype(BF16), w_ffn1_down[l].astype(BF16)
        w2gu, w2d = w_ffn2_gate_up[l].astype(BF16), w_ffn2_down[l].astype(BF16)
        wts = _split_w_in(w_in[l], d_conv, dq, dkv)
        w_co, w_ao, w_o = (w_conv_out[l].astype(BF16), w_attn_out[l].astype(BF16),
                           w_out[l].astype(BF16))
        gain_mix = row(norm_mix_pre[l])

        hp = _ffn_half(hp, row(norm_ffn1_pre[l]), row(norm_ffn1_post[l]), w1gu, w1d, tm=tm, tf=tf)
        conv_y, tail = _conv_seq(hp, gain_mix, wts[0], wts[1], wts[2], w_conv[l], tm=tm, tn=512)
        q, iq, k, v, misc, k_bf, v_bf, ika, ikb, gc, ga = _mixer_common(
            hp, gain_mix, wts, ktabs_p, itabs_p, hd=hd, dkv=dkv, tm=tm)
        wt = (misc[:, IDX_DIM:IDX_DIM + IDX_HEADS] * IDX_HEADS ** -0.5).T
        vt = v_bf.reshape(t // sc, sc, n_kv, hd).transpose(2, 0, 3, 1)
        vt = jnp.concatenate([vt, jnp.ones((n_kv, t // sc, 16, sc), BF16)], axis=2)
        attn_o = _prompt_attention(iq, wt, q, ika, ikb, k_bf, vt, tq=tq, sc=sc, topk=topk_p, hd=hd)
        hp = _merge(hp, conv_y, attn_o, gc, ga, row(norm_mix_post[l]), w_co, w_ao, w_o,
                    tm=tm, tn=512)
        outs[0].append(k.reshape(bp, t // page, page, n_kv, hd))
        outs[1].append(v.reshape(bp, t // page, page, n_kv, hd))
        outs[2].append(misc[:, :IDX_DIM].reshape(bp, t // page, page, IDX_DIM))
        outs[3].append(tail[tail.shape[0] - (CONV_WIDTH - 1):].reshape(bp, CONV_WIDTH - 1, d_conv))
        hp = _ffn_half(hp, row(norm_ffn2_pre[l]), row(norm_ffn2_post[l]), w2gu, w2d, tm=tm, tf=tf)

        hs = _ffn_half(hs, row(norm_ffn1_pre[l]), row(norm_ffn1_post[l]), w1gu, w1d, tm=bs, tf=tf)
        st = state_conv[l]
        conv_y, z = _conv_step(hs, gain_mix, wts[0], wts[1], wts[2], w_conv[l],
                               st[:, 0, :], st[:, 1, :], tn=512)
        q, iq, k, v, misc, k_bf, v_bf, ika, ikb, gc, ga = _mixer_common(
            hs, gain_mix, wts, ktabs_s, itabs_s, hd=hd, dkv=dkv, tm=bs)
        w_idx = (misc[:, IDX_DIM:IDX_DIM + IDX_HEADS] * IDX_HEADS ** -0.5).reshape(bs, IDX_HEADS, 1)
        scores = _sample_scores(page_table, iq.reshape(bs, IDX_HEADS, IDX_DIM), w_idx,
                                ika[:, :IDX_DIM].reshape(bs, IDX_DIM, 1),
                                cache_idx_k[l].transpose(0, 2, 1))
        idx = _sample_select(scores, topk=topk_s).reshape(bs, topk_s)
        attn_o = _sample_attention(page_table, idx, q.reshape(bs, N_HEADS, hd),
                                   k.reshape(bs, n_kv, hd), v.reshape(bs, n_kv, hd),
                                   cache_k[l], cache_v[l], topk=topk_s)
        hs = _merge(hs, conv_y, attn_o.reshape(bs, dq).astype(BF16), gc, ga, row(norm_mix_post[l]),
                    w_co, w_ao, w_o, tm=bs, tn=512)
        outs[4].append(k.reshape(bs, ts, n_kv, hd))
        outs[5].append(v.reshape(bs, ts, n_kv, hd))
        outs[6].append(misc[:, :IDX_DIM].reshape(bs, ts, IDX_DIM))
        outs[7].append(jnp.stack([st[:, 1, :], z], axis=1))
        hs = _ffn_half(hs, row(norm_ffn2_pre[l]), row(norm_ffn2_post[l]), w2gu, w2d, tm=bs, tf=tf)

    return (hp.reshape(bp, t, d), hs.reshape(bs, ts, d)) + tuple(jnp.stack(o) for o in outs)
```

```python
import functools

import numpy as np
import jax
import jax.numpy as jnp
from jax import lax
from jax.experimental import pallas as pl
from jax.experimental.pallas import tpu as pltpu

F32 = jnp.float32
BF16 = jnp.bfloat16
I32 = jnp.int32

N_HEADS = 16
N_KV_HEADS = 4
IDX_HEADS = 16
IDX_DIM = 64
TOPK_MAX = 256
CONV_WIDTH = 3
ROPE_THETA = 500000.0
ROT_DIV = 4
EPS = 1e-6

LANES = 128
VMEM_LIMIT = 56 * 1024 * 1024
LOG2E = 1.4426950408889634
NEG_BIAS = -1e30
INT_MIN = -2 ** 31
INT_MAX = 2 ** 31 - 1
NEG_INF_KEY = int(np.int32(np.uint32(0x807FFFFF)))


def _cparams(sem):
    return pltpu.CompilerParams(dimension_semantics=sem, vmem_limit_bytes=VMEM_LIMIT)


def _rms_scale(x):
    return lax.rsqrt(jnp.mean(x * x, axis=-1, keepdims=True) + EPS)


def _nt_dot(a, b):
    return lax.dot_general(a, b, (((1,), (1,)), ((), ())), preferred_element_type=F32)


def _dot(a, b):
    return jnp.dot(a, b, preferred_element_type=F32)


def _sigmoid(x):
    return 0.5 * jnp.tanh(0.5 * x) + 0.5


def _ffn_kernel(x_ref, pre_ref, post_ref, wg_ref, wu_ref, wd_ref, o_ref, xn_ref, acc_ref):
    j = pl.program_id(1)

    @pl.when(j == 0)
    def _():
        x = x_ref[...]
        xn_ref[...] = (x * _rms_scale(x) * pre_ref[...]).astype(BF16)
        acc_ref[...] = jnp.zeros_like(acc_ref)

    xn = xn_ref[...]
    g = _dot(xn, wg_ref[...])
    u = _dot(xn, wu_ref[...])
    h = (g * _sigmoid(g) * u).astype(BF16)
    acc_ref[...] += _dot(h, wd_ref[...])

    @pl.when(j == pl.num_programs(1) - 1)
    def _():
        y = acc_ref[...]
        o_ref[...] = x_ref[...] + 0.5 * (y * _rms_scale(y) * post_ref[...])


def _ffn_half(x, pre, post, w_gu, w_d, *, tm, tf):
    m, d = x.shape
    f = w_d.shape[0]
    nf = f // tf
    return pl.pallas_call(
        _ffn_kernel,
        grid=(m // tm, nf),
        in_specs=[
            pl.BlockSpec((tm, d), lambda i, j: (i, 0)),
            pl.BlockSpec((1, d), lambda i, j: (0, 0)),
            pl.BlockSpec((1, d), lambda i, j: (0, 0)),
            pl.BlockSpec((d, tf), lambda i, j: (0, j)),
            pl.BlockSpec((d, tf), lambda i, j: (0, nf + j)),
            pl.BlockSpec((tf, d), lambda i, j: (j, 0)),
        ],
        out_specs=pl.BlockSpec((tm, d), lambda i, j: (i, 0)),
        out_shape=jax.ShapeDtypeStruct((m, d), F32),
        scratch_shapes=[pltpu.VMEM((tm, d), BF16), pltpu.VMEM((tm, d), F32)],
        compiler_params=_cparams(("arbitrary", "arbitrary")),
        name="ffn_half",
    )(x, pre, post, w_gu, w_gu, w_d)


def _merge_kernel(h_ref, cy_ref, ao_ref, gc_ref, ga_ref, post_ref, wc_ref, wa_ref, wo_ref,
                  o_ref, acc_ref):
    j = pl.program_id(1)

    @pl.when(j == 0)
    def _():
        acc_ref[...] = jnp.zeros_like(acc_ref)

    mc = _dot(cy_ref[...], wc_ref[...])
    ma = _dot(ao_ref[...], wa_ref[...])
    mix = gc_ref[...].astype(F32) * mc + ga_ref[...].astype(F32) * ma
    acc_ref[...] += _dot(mix.astype(BF16), wo_ref[...])

    @pl.when(j == pl.num_programs(1) - 1)
    def _():
        y = acc_ref[...]
        o_ref[...] = h_ref[...] + y * _rms_scale(y) * post_ref[...]


def _merge(h, conv_y, attn_o, gc, ga, post, w_co, w_ao, w_out, *, tm, tn):
    m, d = h.shape
    return pl.pallas_call(
        _merge_kernel,
        grid=(m // tm, d // tn),
        in_specs=[
            pl.BlockSpec((tm, d), lambda i, j: (i, 0)),
            pl.BlockSpec((tm, d), lambda i, j: (i, 0)),
            pl.BlockSpec((tm, d), lambda i, j: (i, 0)),
            pl.BlockSpec((tm, tn), lambda i, j: (i, j)),
            pl.BlockSpec((tm, tn), lambda i, j: (i, j)),
            pl.BlockSpec((1, d), lambda i, j: (0, 0)),
            pl.BlockSpec((d, tn), lambda i, j: (0, j)),
            pl.BlockSpec((d, tn), lambda i, j: (0, j)),
            pl.BlockSpec((tn, d), lambda i, j: (j, 0)),
        ],
        out_specs=pl.BlockSpec((tm, d), lambda i, j: (i, 0)),
        out_shape=jax.ShapeDtypeStruct((m, d), F32),
        scratch_shapes=[pltpu.VMEM((tm, d), F32)],
        compiler_params=_cparams(("arbitrary", "arbitrary")),
        name="merge",
    )(h, conv_y, attn_o, gc, ga, post, w_co, w_ao, w_out)


def _norm_to_scratch(h_ref, gain_ref, xn_ref):
    @pl.when(pl.program_id(1) == 0)
    def _():
        x = h_ref[...]
        xn_ref[...] = (x * _rms_scale(x) * gain_ref[...]).astype(BF16)


def _rope(x, a, b, c, half):
    return x * a + pltpu.roll(x, LANES - half, axis=1) * b + pltpu.roll(x, half, axis=1) * c


def _conv_seq_kernel(h_ref, gain_ref, wx_ref, wb_ref, wc_ref, wconv_ref, cy_ref, tail_ref,
                     xn_ref, carry_ref):
    i = pl.program_id(0)
    j = pl.program_id(1)
    _norm_to_scratch(h_ref, gain_ref, xn_ref)
    xn = xn_ref[...]
    z = _dot(xn, wc_ref[...]) * _dot(xn, wx_ref[...])
    b = _dot(xn, wb_ref[...])
    tm = z.shape[0]

    @pl.when(i == 0)
    def _():
        carry_ref[j] = jnp.zeros(carry_ref.shape[1:], F32)

    prev = carry_ref[j]
    row = lax.broadcasted_iota(I32, z.shape, 0)
    z1 = jnp.where(row == 0, prev[1:2, :], pltpu.roll(z, 1, axis=0))
    z2 = jnp.where(row == 0, prev[0:1, :],
                   jnp.where(row == 1, prev[1:2, :], pltpu.roll(z, 2, axis=0)))
    w = wconv_ref[...]
    cy_ref[...] = (b * (z2 * w[0:1, :] + z1 * w[1:2, :] + z * w[2:3, :])).astype(BF16)
    tail = z[tm - 8:, :]
    carry_ref[j] = pltpu.roll(tail, 2, axis=0)
    tail_ref[...] = tail


def _col_blocks(d, tn, first_col):
    assert first_col % tn == 0
    return pl.BlockSpec((d, tn), lambda i, j: (0, first_col // tn + j))


def _conv_seq(h, gain, w_all, w_conv, *, tm, tn):
    m, d = h.shape
    dc = w_conv.shape[1]
    return pl.pallas_call(
        _conv_seq_kernel,
        grid=(m // tm, dc // tn),
        in_specs=[
            pl.BlockSpec((tm, d), lambda i, j: (i, 0)),
            pl.BlockSpec((1, d), lambda i, j: (0, 0)),
            _col_blocks(d, tn, 0), _col_blocks(d, tn, dc), _col_blocks(d, tn, 2 * dc),
            pl.BlockSpec((CONV_WIDTH, tn), lambda i, j: (0, j)),
        ],
        out_specs=[
            pl.BlockSpec((tm, tn), lambda i, j: (i, j)),
            pl.BlockSpec((8, tn), lambda i, j: (i, j)),
        ],
        out_shape=[jax.ShapeDtypeStruct((m, dc), BF16), jax.ShapeDtypeStruct((m // tm * 8, dc), F32)],
        scratch_shapes=[pltpu.VMEM((tm, d), BF16), pltpu.VMEM((dc // tn, 8, tn), F32)],
        compiler_params=_cparams(("arbitrary", "arbitrary")),
        name="conv_seq",
    )(h, gain, w_all, w_all, w_all, w_conv)


def _conv_step_kernel(h_ref, gain_ref, wx_ref, wb_ref, wc_ref, wconv_ref, s0_ref, s1_ref,
                      cy_ref, z_ref, xn_ref):
    _norm_to_scratch(h_ref, gain_ref, xn_ref)
    xn = xn_ref[...]
    z = _dot(xn, wc_ref[...]) * _dot(xn, wx_ref[...])
    b = _dot(xn, wb_ref[...])
    w = wconv_ref[...]
    cy_ref[...] = (b * (s0_ref[...] * w[0:1, :] + s1_ref[...] * w[1:2, :] + z * w[2:3, :])
                   ).astype(BF16)
    z_ref[...] = z


def _conv_step(h, gain, w_all, w_conv, s0, s1, *, tn):
    m, d = h.shape
    dc = w_conv.shape[1]
    cspec = pl.BlockSpec((m, tn), lambda i, j: (0, j))
    return pl.pallas_call(
        _conv_step_kernel,
        grid=(1, dc // tn),
        in_specs=[
            pl.BlockSpec((m, d), lambda i, j: (0, 0)),
            pl.BlockSpec((1, d), lambda i, j: (0, 0)),
            _col_blocks(d, tn, 0), _col_blocks(d, tn, dc), _col_blocks(d, tn, 2 * dc),
            pl.BlockSpec((CONV_WIDTH, tn), lambda i, j: (0, j)),
            cspec, cspec,
        ],
        out_specs=[cspec, cspec],
        out_shape=[jax.ShapeDtypeStruct((m, dc), BF16), jax.ShapeDtypeStruct((m, dc), F32)],
        scratch_shapes=[pltpu.VMEM((m, d), BF16)],
        compiler_params=_cparams(("arbitrary", "arbitrary")),
        name="conv_step",
    )(h, gain, w_all, w_all, w_all, w_conv, s0, s1)


def _rope_proj_kernel(h_ref, gain_ref, w_ref, a_ref, b_ref, c_ref, o_ref, xn_ref, *, half, scale):
    _norm_to_scratch(h_ref, gain_ref, xn_ref)
    y = _dot(xn_ref[...], w_ref[...])
    a, b, c = a_ref[...], b_ref[...], c_ref[...]
    for g in range(y.shape[1] // LANES):
        sl = slice(g * LANES, (g + 1) * LANES)
        o_ref[:, sl] = (_rope(y[:, sl], a, b, c, half) * scale).astype(o_ref.dtype)


def _rope_proj(h, gain, w_all, tabs, *, first_col, n, half, scale, tm, tn):
    m, d = h.shape
    tspec = pl.BlockSpec((tm, LANES), lambda i, j: (i, 0))
    return pl.pallas_call(
        functools.partial(_rope_proj_kernel, half=half, scale=scale),
        grid=(m // tm, n // tn),
        in_specs=[
            pl.BlockSpec((tm, d), lambda i, j: (i, 0)),
            pl.BlockSpec((1, d), lambda i, j: (0, 0)),
            _col_blocks(d, tn, first_col),
            tspec, tspec, tspec,
        ],
        out_specs=pl.BlockSpec((tm, tn), lambda i, j: (i, j)),
        out_shape=jax.ShapeDtypeStruct((m, n), BF16),
        scratch_shapes=[pltpu.VMEM((tm, d), BF16)],
        compiler_params=_cparams(("arbitrary", "arbitrary")),
        name="rope_proj",
    )(h, gain, w_all, *tabs)


def _kv_proj_kernel(h_ref, gain_ref, wk_ref, wv_ref, wm_ref, ka_ref, kb_ref, kc_ref,
                    ia_ref, ib_ref, ic_ref,
                    k_ref, v_ref, misc_ref, kbf_ref, vbf_ref, ika_ref, ikb_ref, *, dkv):
    x = h_ref[...]
    xn = (x * _rms_scale(x) * gain_ref[...]).astype(BF16)
    yk = _dot(xn, wk_ref[...])
    a, b, c = ka_ref[...], kb_ref[...], kc_ref[...]
    for g in range(dkv // LANES):
        sl = slice(g * LANES, (g + 1) * LANES)
        kg = _rope(yk[:, sl], a, b, c, LANES // ROT_DIV // 2)
        k_ref[:, sl] = kg
        kbf_ref[:, sl] = kg.astype(BF16)
    v = _dot(xn, wv_ref[...])
    v_ref[...] = v
    vbf_ref[...] = v.astype(BF16)
    misc = _rope(_dot(xn, wm_ref[...]), ia_ref[...], ib_ref[...], ic_ref[...],
                 IDX_DIM // ROT_DIV // 2)
    misc_ref[...] = misc
    lane = lax.broadcasted_iota(I32, misc.shape, 1)
    ik_lo = jnp.where(lane < IDX_DIM, misc, 0.0)
    ika_ref[...] = ik_lo.astype(BF16)
    ikb_ref[...] = pltpu.roll(ik_lo, IDX_DIM, axis=1).astype(BF16)


def _kv_proj(h, gain, w_all, ktabs, itabs, *, k_col, v_col, misc_col, dkv, tm):
    m, d = h.shape
    assert k_col % dkv == 0 and v_col % dkv == 0 and misc_col % LANES == 0
    tspec = pl.BlockSpec((tm, LANES), lambda i: (i, 0))
    kvspec = pl.BlockSpec((tm, dkv), lambda i: (i, 0))
    mspec = pl.BlockSpec((tm, LANES), lambda i: (i, 0))
    return pl.pallas_call(
        functools.partial(_kv_proj_kernel, dkv=dkv),
        grid=(m // tm,),
        in_specs=[
            pl.BlockSpec((tm, d), lambda i: (i, 0)),
            pl.BlockSpec((1, d), lambda i: (0, 0)),
            pl.BlockSpec((d, dkv), lambda i: (0, k_col // dkv)),
            pl.BlockSpec((d, dkv), lambda i: (0, v_col // dkv)),
            pl.BlockSpec((d, LANES), lambda i: (0, misc_col // LANES)),
            tspec, tspec, tspec, tspec, tspec, tspec,
        ],
        out_specs=[kvspec, kvspec, mspec, kvspec, kvspec, mspec, mspec],
        out_shape=[
            jax.ShapeDtypeStruct((m, dkv), F32), jax.ShapeDtypeStruct((m, dkv), F32),
            jax.ShapeDtypeStruct((m, LANES), F32),
            jax.ShapeDtypeStruct((m, dkv), BF16), jax.ShapeDtypeStruct((m, dkv), BF16),
            jax.ShapeDtypeStruct((m, LANES), BF16), jax.ShapeDtypeStruct((m, LANES), BF16),
        ],
        compiler_params=_cparams(("arbitrary",)),
        name="kv_proj",
    )(h, gain, w_all, w_all, w_all, *ktabs, *itabs)


def _gate_proj_kernel(h_ref, gain_ref, wc_ref, wa_ref, gc_ref, ga_ref, xn_ref):
    _norm_to_scratch(h_ref, gain_ref, xn_ref)
    xn = xn_ref[...]
    gc_ref[...] = _sigmoid(_dot(xn, wc_ref[...])).astype(BF16)
    ga_ref[...] = _sigmoid(_dot(xn, wa_ref[...])).astype(BF16)


def _gate_proj(h, gain, w_gates, *, tm, tn):
    m, d = h.shape
    n = w_gates.shape[1] // 2
    nb = n // tn
    ospec = pl.BlockSpec((tm, tn), lambda i, j: (i, j))
    return pl.pallas_call(
        _gate_proj_kernel,
        grid=(m // tm, nb),
        in_specs=[
            pl.BlockSpec((tm, d), lambda i, j: (i, 0)),
            pl.BlockSpec((1, d), lambda i, j: (0, 0)),
            pl.BlockSpec((d, tn), lambda i, j: (0, j)),
            pl.BlockSpec((d, tn), lambda i, j: (0, nb + j)),
        ],
        out_specs=[ospec, ospec],
        out_shape=[jax.ShapeDtypeStruct((m, n), BF16), jax.ShapeDtypeStruct((m, n), BF16)],
        scratch_shapes=[pltpu.VMEM((tm, d), BF16)],
        compiler_params=_cparams(("arbitrary", "arbitrary")),
        name="gate_proj",
    )(h, gain, w_gates, w_gates)


def _ordered_key(score):
    bits = pltpu.bitcast(score, I32)
    return bits ^ ((bits >> 31) & INT_MAX)


def _kth_largest(count_ge, k, like):
    def body(b, lo):
        cand = lo + jnp.left_shift(jnp.int32(1), 31 - b)
        return jnp.where(count_ge(cand) >= k, cand, lo)
    return lax.fori_loop(0, 32, body, jnp.full_like(like, INT_MIN))


def _tie_limit(count_eq_before, need, nbits, like):
    def body(b, lo):
        cand = lo + jnp.left_shift(jnp.int32(1), nbits - 1 - b)
        return jnp.where(count_eq_before(cand) < need, cand, lo)
    return lax.fori_loop(0, nbits, body, jnp.zeros_like(like))


def _prompt_attn_kernel(iq_ref, wt_ref, q_ref, ika_ref, ikb_ref, k_ref, vt_ref, o_ref,
                        key_ref, half_ref, lim_ref, q4_ref, m_ref, acc_ref, lga_ref, lgb_ref,
                        *, tq, sc, topk, n_rep, hd, hb):
    i = pl.program_id(0)
    t0 = i * tq
    n_chunks = (t0 + tq) // sc
    n_pairs = iq_ref.shape[1] // LANES
    n_kv = k_ref.shape[1] // hd
    pos_bits = int(k_ref.shape[0] - 1).bit_length()
    i16_min = -2 ** 15

    def rows(c):
        return pl.ds(pl.multiple_of(c * sc, sc), sc)

    def score_chunk(c, carry):
        ka = ika_ref[rows(c), :]
        kb = ikb_ref[rows(c), :]
        acc = jnp.zeros((sc, tq), F32)
        for p in range(n_pairs):
            iq_p = iq_ref[:, p * LANES:(p + 1) * LANES]
            acc += jnp.maximum(_nt_dot(ka, iq_p), 0.0) * wt_ref[2 * p:2 * p + 1, :]
            acc += jnp.maximum(_nt_dot(kb, iq_p), 0.0) * wt_ref[2 * p + 1:2 * p + 2, :]
        spos = c * sc + lax.broadcasted_iota(I32, (sc, tq), 0)
        tpos = t0 + lax.broadcasted_iota(I32, (sc, tq), 1)
        key = _ordered_key(jnp.where(spos <= tpos, acc, -jnp.inf))
        key_ref[rows(c), :] = key
        half_ref[rows(c), :] = (key >> 16).astype(jnp.int16)
        return carry

    lax.fori_loop(0, n_chunks, score_chunk, 0)

    def half_count(cand, strict=False):
        cand16 = cand.astype(jnp.int16)

        def hits(c):
            blk = half_ref[rows(c), :]
            hit = jnp.where(blk > cand16 if strict else blk >= cand16,
                            jnp.bfloat16(1), jnp.bfloat16(0))
            part = hit[0:16]
            for r in range(1, sc // 16):
                part = part + hit[r * 16:(r + 1) * 16]
            return part

        def body2(c2, cnt):
            return cnt + (hits(2 * c2) + hits(2 * c2 + 1)).astype(F32)

        def body1(c, cnt):
            return cnt + hits(c).astype(F32)

        n2 = n_chunks // 2
        cnt = lax.fori_loop(0, n2, body2, jnp.zeros((16, tq), F32))
        cnt = lax.fori_loop(2 * n2, n_chunks, body1, cnt)
        return cnt.sum(axis=0, keepdims=True).astype(I32)

    def half_search(k):
        def body(b, carry):
            lo, cnt_lo = carry
            cand = lo + jnp.left_shift(jnp.int32(1), 15 - b)
            cnt = half_count(cand)
            ok = cnt >= k
            return jnp.where(ok, cand, lo), jnp.where(ok, cnt, cnt_lo)
        init = (jnp.full((1, tq), i16_min, I32), jnp.full((1, tq), n_chunks * sc, I32))
        return lax.fori_loop(0, 16, body, init)

    thr_hi, _ = half_search(topk)
    n_gt_hi = half_count(thr_hi, strict=True)

    def low_chunk(c, carry):
        key = key_ref[rows(c), :]
        low = (key & 0xFFFF) - 2 ** 15
        half_ref[rows(c), :] = jnp.where((key >> 16) == thr_hi, low, i16_min).astype(jnp.int16)
        return carry

    lax.fori_loop(0, n_chunks, low_chunk, 0)
    thr_lo, n_ge_lo = half_search(topk - n_gt_hi)
    thr = thr_hi * 65536 + (thr_lo + 2 ** 15)
    n_ge = n_gt_hi + n_ge_lo
    has_tie = jnp.max(jnp.where((n_ge > topk) & (thr > NEG_INF_KEY), 1.0, 0.0)) > 0.0
    lim_ref[...] = jnp.full(lim_ref.shape, INT_MAX, I32)

    @pl.when(has_tie)
    def _():
        def column_count(pred):
            def body(c, cnt):
                blk = key_ref[rows(c), :]
                spos = c * sc + lax.broadcasted_iota(I32, (sc, tq), 0)
                hit = jnp.where(pred(blk, spos), 1, 0).astype(I32)
                return cnt + hit.reshape(sc // 8, 8, tq).sum(axis=0)
            cnt = lax.fori_loop(0, n_chunks, body, jnp.zeros((8, tq), I32))
            return cnt.astype(F32).sum(axis=0, keepdims=True).astype(I32)

        need = topk - column_count(lambda blk, spos: blk > thr)
        lim = _tie_limit(
            lambda p: column_count(lambda blk, spos: (blk == thr) & (spos < p)),
            need, pos_bits + 1, jnp.zeros((1, tq), I32))
        lim_ref[...] = jnp.broadcast_to(lim, lim_ref.shape)

    lim = lim_ref[0:1, :]

    def bias_chunk(c, carry):
        blk = key_ref[rows(c), :]
        spos = c * sc + lax.broadcasted_iota(I32, (sc, tq), 0)
        sel = ((blk > thr) | ((blk == thr) & (spos <= lim))) & (blk > NEG_INF_KEY)
        key_ref[rows(c), :] = pltpu.bitcast(jnp.where(sel, 0.0, NEG_BIAS).astype(F32), I32)
        return carry

    lax.fori_loop(0, n_chunks, bias_chunk, 0)

    n_heads = n_kv * n_rep
    n_grp = n_heads // hb
    for j in range(n_grp):
        for r in range(hb):
            head = j * hb + r
            q4_ref[j, r * tq:(r + 1) * tq, :] = q_ref[:, head * hd:(head + 1) * hd]
    m_ref[...] = jnp.full(m_ref.shape, NEG_BIAS, F32)
    acc_ref[...] = jnp.zeros(acc_ref.shape, F32)

    def qk_store(c, j, dst_ref):
        g = j * hb // n_rep
        dst_ref[...] = _nt_dot(k_ref[rows(c), g * hd:(g + 1) * hd], q4_ref[j])

    bufs = (lga_ref, lgb_ref)
    qk_store(0, 0, bufs[0])

    def attn_chunk(c, carry):
        bias = pltpu.bitcast(key_ref[rows(c), :], F32)
        bias_w = jnp.concatenate([bias] * hb, axis=1)
        c_next = jnp.minimum(c + 1, n_chunks - 1)
        for j in range(n_grp):
            cur, nxt = bufs[j % 2], bufs[(j + 1) % 2]
            if j + 1 < n_grp:
                qk_store(c, j + 1, nxt)
            else:
                qk_store(c_next, 0, nxt)
            logit = cur[...] + bias_w
            m_old = m_ref[j]
            m_new = jnp.maximum(m_old, jnp.max(logit, axis=0, keepdims=True))
            p = jnp.exp2(logit - m_new).astype(BF16)
            acc_ref[j] = (jnp.exp2(m_old - m_new) * acc_ref[j]
                          + _dot(vt_ref[j * hb // n_rep, c], p))
            m_ref[j] = m_new
        return carry

    lax.fori_loop(0, n_chunks, attn_chunk, 0)
    for j in range(n_grp):
        for r in range(hb):
            head = j * hb + r
            cols = slice(r * tq, (r + 1) * tq)
            out_t = acc_ref[j, 0:hd, cols] / acc_ref[j, hd:hd + 1, cols]
            o_ref[:, head * hd:(head + 1) * hd] = out_t.T.astype(o_ref.dtype)


def _prompt_attention(iq, wt, q, ika, ikb, k_bf, vt, *, tq, sc, topk, hd):
    t, dq = q.shape
    n_kv = k_bf.shape[1] // hd
    n_rep = dq // hd // n_kv
    assert tq >= topk and tq % sc == 0 and t % tq == 0
    hb = 4
    assert (n_kv * n_rep // hb) % 2 == 0
    n_grp = n_kv * n_rep // hb
    resident = functools.partial(pl.BlockSpec, pipeline_mode=pl.Buffered(1))
    return pl.pallas_call(
        functools.partial(_prompt_attn_kernel, tq=tq, sc=sc, topk=topk, n_rep=n_rep, hd=hd, hb=hb),
        grid=(t // tq,),
        in_specs=[
            pl.BlockSpec((tq, iq.shape[1]), lambda i: (i, 0)),
            pl.BlockSpec((wt.shape[0], tq), lambda i: (0, i)),
            pl.BlockSpec((tq, dq), lambda i: (i, 0)),
            resident(ika.shape, lambda i: (0, 0)),
            resident(ikb.shape, lambda i: (0, 0)),
            resident(k_bf.shape, lambda i: (0, 0)),
            resident(vt.shape, lambda i: (0, 0, 0, 0)),
        ],
        out_specs=pl.BlockSpec((tq, dq), lambda i: (i, 0)),
        out_shape=jax.ShapeDtypeStruct((t, dq), BF16),
        scratch_shapes=[
            pltpu.VMEM((t, tq), I32),
            pltpu.VMEM((t, tq), jnp.int16),
            pltpu.VMEM((8, tq), I32),
            pltpu.VMEM((n_grp, hb * tq, hd), BF16),
            pltpu.VMEM((n_grp, 1, hb * tq), F32),
            pltpu.VMEM((n_grp, vt.shape[2], hb * tq), F32),
            pltpu.VMEM((sc, hb * tq), F32),
            pltpu.VMEM((sc, hb * tq), F32),
        ],
        compiler_params=_cparams(("arbitrary",)),
        name="prompt_attention",
    )(iq, wt, q, ika, ikb, k_bf, vt)


def _sample_score_kernel(pt_ref, iq_ref, w_ref, ikn_ref, cache_ref, o_ref, buf_ref, sem_ref,
                         *, n_pages, page):
    b = pl.program_id(0)
    nb = pl.num_programs(0)

    def page_copy(bb, slot, p):
        return pltpu.make_async_copy(cache_ref.at[pt_ref[bb, p]], buf_ref.at[slot, p],
                                     sem_ref.at[slot])

    def start_all(bb, slot):
        def body(p, carry):
            page_copy(bb, slot, p).start()
            return carry
        lax.fori_loop(0, n_pages, body, 0)

    slot = b % 2

    @pl.when(b == 0)
    def _():
        start_all(b, slot)

    @pl.when(b + 1 < nb)
    def _():
        start_all(b + 1, 1 - slot)

    def wait_body(p, carry):
        page_copy(b, slot, p).wait()
        return carry
    lax.fori_loop(0, n_pages, wait_body, 0)

    iq = iq_ref[0]
    w = w_ref[0]

    def head_sum(keys_t):
        s = jnp.maximum(_dot(iq, keys_t), 0.0) * w
        return jnp.sum(s, axis=0, keepdims=True)

    group = 8

    def group_body(gi, carry):
        p0 = pl.multiple_of(gi * group, group)
        blk = buf_ref[slot, pl.ds(p0, group)]
        keys_t = jnp.concatenate([blk[r] for r in range(group)], axis=1).astype(BF16)
        s = head_sum(keys_t)
        for r in range(group):
            o_ref[0, pl.ds(p0 + r, 1), :] = s[:, r * page:(r + 1) * page]
        return carry
    lax.fori_loop(0, n_pages // group, group_body, 0)
    own = head_sum(jnp.broadcast_to(ikn_ref[0], (ikn_ref.shape[1], page)))
    lane = lax.broadcasted_iota(I32, (1, page), 1)
    o_ref[0, pl.ds(n_pages, 1), :] = jnp.where(lane == 0, own, -jnp.inf)
    o_ref[0, pl.ds(n_pages + 1, 7), :] = jnp.full((7, page), -jnp.inf, F32)


def _sample_scores(page_table, iq, w, ik_new, cache_ik_t):
    bs, n_pages = page_table.shape
    idim, page = cache_ik_t.shape[1:]
    rows = n_pages + 8
    grid_spec = pltpu.PrefetchScalarGridSpec(
        num_scalar_prefetch=1,
        grid=(bs,),
        in_specs=[
            pl.BlockSpec((1,) + iq.shape[1:], lambda b, pt: (b, 0, 0)),
            pl.BlockSpec((1,) + w.shape[1:], lambda b, pt: (b, 0, 0)),
            pl.BlockSpec((1,) + ik_new.shape[1:], lambda b, pt: (b, 0, 0)),
            pl.BlockSpec(memory_space=pl.ANY),
        ],
        out_specs=pl.BlockSpec((1, rows, page), lambda b, pt: (b, 0, 0)),
        scratch_shapes=[pltpu.VMEM((2, n_pages, idim, page), F32),
                        pltpu.SemaphoreType.DMA((2,))],
    )
    return pl.pallas_call(
        functools.partial(_sample_score_kernel, n_pages=n_pages, page=page),
        grid_spec=grid_spec,
        out_shape=jax.ShapeDtypeStruct((bs, rows, page), F32),
        compiler_params=_cparams(("arbitrary",)),
        name="sample_scores",
    )(page_table, iq, w, ik_new, cache_ik_t)


def _slab_pos(shape):
    nd = len(shape)
    return (lax.broadcasted_iota(I32, shape, nd - 2) * shape[-1]
            + lax.broadcasted_iota(I32, shape, nd - 1))


def _sample_thresh_kernel(s_ref, thr_ref, lim_ref, *, topk):
    key = _ordered_key(s_ref[...])
    bs, rows, page = key.shape
    pos = _slab_pos(key.shape)

    def count(pred):
        c = jnp.sum(jnp.where(pred, 1.0, 0.0), axis=1, keepdims=True)
        return jnp.sum(c, axis=2, keepdims=True).astype(I32)

    like = jnp.zeros((bs, 1, 1), I32)
    thr = _kth_largest(lambda v: count(key >= v), topk, like)
    need = topk - count(key > thr)
    lim = _tie_limit(lambda p: count((key == thr) & (pos < p)), need,
                     int(rows * page - 1).bit_length() + 1, like)
    thr_ref[...] = thr
    lim_ref[...] = lim


def _sample_compact_kernel(s_ref, thr_ref, lim_ref, idx_ref, code_ref, *, topk):
    key = _ordered_key(s_ref[0])
    rows, page = key.shape
    thr, lim = thr_ref[0], lim_ref[0]
    sel = ((key > thr) | ((key == thr) & (_slab_pos(key.shape) <= lim))) & (key > NEG_INF_KEY)
    self32 = jnp.where(sel, 1.0, 0.0)

    upper = (lax.broadcasted_iota(I32, (page, page), 0)
             < lax.broadcasted_iota(I32, (page, page), 1))
    within = _dot(self32.astype(BF16), jnp.where(upper, 1.0, 0.0).astype(BF16))
    kpad = 256
    row_tot = jnp.broadcast_to(jnp.sum(self32, axis=1, keepdims=True), (rows, page))
    row_tot = jnp.concatenate([row_tot, jnp.zeros((kpad - rows, page), F32)], axis=0)
    earlier = (lax.broadcasted_iota(I32, (rows, kpad), 1)
               < lax.broadcasted_iota(I32, (rows, kpad), 0))
    before = _dot(jnp.where(earlier, 1.0, 0.0).astype(BF16), row_tot.astype(BF16))
    code_ref[...] = jnp.where(sel, within + before, -1.0)

    slot = lax.broadcasted_iota(I32, (topk, page), 0).astype(F32)
    lane = lax.broadcasted_iota(I32, (topk, page), 1)
    unroll = 8

    def row_body(r8, found):
        for u in range(unroll):
            r = r8 * unroll + u
            hit = code_ref[pl.ds(r, 1), :] == slot
            found = jnp.maximum(found, jnp.where(hit, r * page + lane, -1))
        return found

    found = lax.fori_loop(0, rows // unroll, row_body, jnp.full((topk, page), -1, I32))
    idx_ref[0] = jnp.max(found.astype(F32), axis=1, keepdims=True).astype(I32)


def _sample_select(scores, *, topk):
    bs, rows, page = scores.shape
    assert rows % 8 == 0 and rows <= 256
    one = pl.BlockSpec((1, 1, 1), lambda b: (b, 0, 0))
    thr, lim = pl.pallas_call(
        functools.partial(_sample_thresh_kernel, topk=topk),
        out_shape=[jax.ShapeDtypeStruct((bs, 1, 1), I32)] * 2,
        compiler_params=pltpu.CompilerParams(vmem_limit_bytes=VMEM_LIMIT),
        name="sample_thresh",
    )(scores)
    return pl.pallas_call(
        functools.partial(_sample_compact_kernel, topk=topk),
        grid=(bs,),
        in_specs=[pl.BlockSpec((1, rows, page), lambda b: (b, 0, 0)), one, one],
        out_specs=pl.BlockSpec((1, topk, 1), lambda b: (b, 0, 0)),
        out_shape=jax.ShapeDtypeStruct((bs, topk, 1), I32),
        scratch_shapes=[pltpu.VMEM((rows, page), F32)],
        compiler_params=_cparams(("arbitrary",)),
        name="sample_compact",
    )(scores, thr, lim)


def _sample_attn_kernel(pt_ref, idx_ref, idxv_ref, q_ref, kn_ref, vn_ref, ck_ref, cv_ref, o_ref,
                        kbuf_ref, vbuf_ref, sem_ref, *, page, past, topk, n_rep):
    b = pl.program_id(0)
    nb = pl.num_programs(0)
    n_kv = kn_ref.shape[1]

    def row_copies(bb, slot, j):
        pos = jnp.clip(idx_ref[bb, j], 0, past - 1)
        phys = pt_ref[bb, lax.div(pos, page)]
        off = lax.rem(pos, page)
        return (pltpu.make_async_copy(ck_ref.at[phys, off], kbuf_ref.at[slot, j], sem_ref.at[0, slot]),
                pltpu.make_async_copy(cv_ref.at[phys, off], vbuf_ref.at[slot, j], sem_ref.at[1, slot]))

    unroll = 8

    def start_all(bb, slot):
        def body(j8, carry):
            for u in range(unroll):
                ck, cv = row_copies(bb, slot, j8 * unroll + u)
                ck.start()
                cv.start()
            return carry
        lax.fori_loop(0, topk // unroll, body, 0)

    slot = b % 2

    @pl.when(b == 0)
    def _():
        start_all(b, slot)

    @pl.when(b + 1 < nb)
    def _():
        start_all(b + 1, 1 - slot)

    for h in range(topk // page):
        slab = pl.ds(h * page, page)
        pltpu.make_async_copy(ck_ref.at[0], kbuf_ref.at[slot, slab], sem_ref.at[0, slot]).wait()
        pltpu.make_async_copy(cv_ref.at[0], vbuf_ref.at[slot, slab], sem_ref.at[1, slot]).wait()

    pos = idxv_ref[0]
    bias = jnp.where((pos >= 0) & (pos < past), 0.0, NEG_BIAS)
    own_sel = jnp.max(jnp.where(pos == past, 1.0, 0.0), axis=1, keepdims=True)
    own_bias = jnp.where(own_sel > 0.0, 0.0, NEG_BIAS)
    for g in range(n_kv):
        rows = slice(g * n_rep, (g + 1) * n_rep)
        qg = q_ref[0, rows, :]
        kg = kbuf_ref[slot, :, g, :].astype(BF16)
        vg = vbuf_ref[slot, :, g, :].astype(BF16)
        kn = kn_ref[0, g:g + 1, :].astype(BF16).astype(F32)
        vn = vn_ref[0, g:g + 1, :].astype(BF16).astype(F32)
        logit = _nt_dot(qg, kg) + bias
        own = jnp.sum(qg.astype(F32) * kn, axis=1, keepdims=True) + own_bias
        m = jnp.maximum(jnp.max(logit, axis=1, keepdims=True), own)
        p = jnp.exp2(logit - m)
        p_own = jnp.exp2(own - m)
        denom = jnp.sum(p, axis=1, keepdims=True) + p_own
        num = _dot(p.astype(BF16), vg) + p_own.astype(BF16).astype(F32) * vn
        o_ref[0, rows, :] = num / denom


def _sample_attention(page_table, idx, q, k_new, v_new, cache_k, cache_v, *, topk):
    bs, n_pages = page_table.shape
    page, n_kv, hd = cache_k.shape[1:]
    n_heads = q.shape[1]
    past = n_pages * page
    assert topk % page == 0 and topk % 8 == 0
    grid_spec = pltpu.PrefetchScalarGridSpec(
        num_scalar_prefetch=2,
        grid=(bs,),
        in_specs=[
            pl.BlockSpec((1, 1, topk), lambda b, pt, ix: (b, 0, 0)),
            pl.BlockSpec((1, n_heads, hd), lambda b, pt, ix: (b, 0, 0)),
            pl.BlockSpec((1, n_kv, hd), lambda b, pt, ix: (b, 0, 0)),
            pl.BlockSpec((1, n_kv, hd), lambda b, pt, ix: (b, 0, 0)),
            pl.BlockSpec(memory_space=pl.ANY),
            pl.BlockSpec(memory_space=pl.ANY),
        ],
        out_specs=pl.BlockSpec((1, n_heads, hd), lambda b, pt, ix: (b, 0, 0)),
        scratch_shapes=[pltpu.VMEM((2, topk, n_kv, hd), F32),
                        pltpu.VMEM((2, topk, n_kv, hd), F32),
                        pltpu.SemaphoreType.DMA((2, 2))],
    )
    return pl.pallas_call(
        functools.partial(_sample_attn_kernel, page=page, past=past, topk=topk,
                          n_rep=n_heads // n_kv),
        grid_spec=grid_spec,
        out_shape=jax.ShapeDtypeStruct((bs, n_heads, hd), F32),
        compiler_params=_cparams(("arbitrary",)),
        name="sample_attention",
    )(page_table, idx, idx.reshape(bs, 1, topk), q, k_new, v_new, cache_k, cache_v)


def _rope_tables(pos, dim, pad_lanes=0):
    rot = dim // ROT_DIV
    half = rot // 2
    inv = ROPE_THETA ** (-jnp.arange(half, dtype=F32) / half)
    ang = pos.astype(F32)[:, None] * inv[None, :]
    cos, sin = jnp.cos(ang), jnp.sin(ang)
    width = dim + pad_lanes
    a = jnp.pad(jnp.concatenate([cos, cos], axis=1), ((0, 0), (0, width - rot)), constant_values=1.0)
    b = jnp.pad(-sin, ((0, 0), (0, width - half)))
    c = jnp.pad(sin, ((0, 0), (half, width - rot)))
    reps = LANES // width
    return tuple(jnp.tile(t, (1, reps)) if reps > 1 else t for t in (a, b, c))


def _in_proj_columns(d_model, d_conv, dq, dkv):
    names = ("x", "b", "c", "q", "k", "v", "iq", "ik", "iw", "gc", "ga")
    sizes = (d_conv, d_conv, d_conv, dq, dkv, dkv, IDX_HEADS * IDX_DIM, IDX_DIM, IDX_HEADS,
             d_model, d_model)
    starts = np.concatenate([[0], np.cumsum(sizes)[:-1]])
    return {n: int(s) for n, s in zip(names, starts)}


def _mixer_common(h, gain, w_all, w_gates, cols, ktabs, itabs, *, hd, dq, dkv, tm, tn):
    q = _rope_proj(h, gain, w_all, ktabs, first_col=cols["q"], n=dq, half=hd // ROT_DIV // 2,
                   scale=hd ** -0.5 * LOG2E, tm=tm, tn=tn)
    iq = _rope_proj(h, gain, w_all, itabs[0], first_col=cols["iq"], n=IDX_HEADS * IDX_DIM,
                    half=IDX_DIM // ROT_DIV // 2, scale=IDX_DIM ** -0.5, tm=tm, tn=tn)
    assert cols["iw"] == cols["ik"] + IDX_DIM
    k, v, misc, k_bf, v_bf, ika, ikb = _kv_proj(
        h, gain, w_all, ktabs, itabs[1], k_col=cols["k"], v_col=cols["v"], misc_col=cols["ik"],
        dkv=dkv, tm=tm)
    gc, ga = _gate_proj(h, gain, w_gates, tm=tm, tn=tn)
    return q, iq, k, v, misc, k_bf, v_bf, ika, ikb, gc, ga


def kernel(x_prompt, x_sample, cache_k, cache_v, cache_idx_k, state_conv, page_table,
           norm_ffn1_pre, norm_ffn1_post, w_ffn1_gate_up, w_ffn1_down,
           norm_mix_pre, norm_mix_post, w_in, w_conv, w_conv_out, w_attn_out, w_out,
           norm_ffn2_pre, norm_ffn2_post, w_ffn2_gate_up, w_ffn2_down):
    bp, t, d = x_prompt.shape
    bs, ts, _ = x_sample.shape
    depth = w_in.shape[0]
    page, n_kv, hd = cache_k.shape[2:]
    n_pages = page_table.shape[1]
    past = n_pages * page
    d_conv = w_conv.shape[2]
    dq = w_attn_out.shape[1]
    dkv = n_kv * hd
    assert bp == 1 and ts == 1 and dq == N_HEADS * hd and n_kv == N_KV_HEADS

    tm = min(512, t)
    tf = 512
    tn = 1024
    tq = sc = min(256, t)
    cols = _in_proj_columns(d, d_conv, dq, dkv)
    topk_p = min(TOPK_MAX, t // 4)
    topk_s = min(TOPK_MAX, (past + ts) // 4)

    pos_p = jnp.arange(t, dtype=jnp.int32)
    pos_s = jnp.full((bs,), past, jnp.int32)
    ktabs_p, ktabs_s = _rope_tables(pos_p, hd), _rope_tables(pos_s, hd)
    itabs_p = (_rope_tables(pos_p, IDX_DIM), _rope_tables(pos_p, IDX_DIM, LANES - IDX_DIM))
    itabs_s = (_rope_tables(pos_s, IDX_DIM), _rope_tables(pos_s, IDX_DIM, LANES - IDX_DIM))

    hp = x_prompt.reshape(t, d)
    hs = x_sample.reshape(bs, d)
    outs = [[] for _ in range(8)]
    row = lambda a: a.reshape(1, -1)
    for l in range(depth):
        w1gu, w1d = w_ffn1_gate_up[l].astype(BF16), w_ffn1_down[l].astype(BF16)
        w2gu, w2d = w_ffn2_gate_up[l].astype(BF16), w_ffn2_down[l].astype(BF16)
        w_all, w_gates = w_in[l].astype(BF16), w_in[l][:, cols["gc"]:].astype(BF16)
        w_co, w_ao, w_o = (w_conv_out[l].astype(BF16), w_attn_out[l].astype(BF16),
                           w_out[l].astype(BF16))
        gain_mix = row(norm_mix_pre[l])

        hp = _ffn_half(hp, row(norm_ffn1_pre[l]), row(norm_ffn1_post[l]), w1gu, w1d, tm=tm, tf=tf)
        conv_y, tail = _conv_seq(hp, gain_mix, w_all, w_conv[l], tm=tm, tn=tn)
        q, iq, k, v, misc, k_bf, v_bf, ika, ikb, gc, ga = _mixer_common(
            hp, gain_mix, w_all, w_gates, cols, ktabs_p, itabs_p, hd=hd, dq=dq, dkv=dkv, tm=tm, tn=tn)
        wt = (misc[:, IDX_DIM:IDX_DIM + IDX_HEADS] * IDX_HEADS ** -0.5).T
        vt = v_bf.reshape(t // sc, sc, n_kv, hd).transpose(2, 0, 3, 1)
        vt = jnp.concatenate([vt, jnp.ones((n_kv, t // sc, 16, sc), BF16)], axis=2)
        attn_o = _prompt_attention(iq, wt, q, ika, ikb, k_bf, vt, tq=tq, sc=sc, topk=topk_p, hd=hd)
        hp = _merge(hp, conv_y, attn_o, gc, ga, row(norm_mix_post[l]), w_co, w_ao, w_o,
                    tm=tm, tn=tn // 2)
        outs[0].append(k.reshape(bp, t // page, page, n_kv, hd))
        outs[1].append(v.reshape(bp, t // page, page, n_kv, hd))
        outs[2].append(misc[:, :IDX_DIM].reshape(bp, t // page, page, IDX_DIM))
        outs[3].append(tail[tail.shape[0] - (CONV_WIDTH - 1):].reshape(bp, CONV_WIDTH - 1, d_conv))
        hp = _ffn_half(hp, row(norm_ffn2_pre[l]), row(norm_ffn2_post[l]), w2gu, w2d, tm=tm, tf=tf)

        hs = _ffn_half(hs, row(norm_ffn1_pre[l]), row(norm_ffn1_post[l]), w1gu, w1d, tm=bs, tf=tf)
        st = state_conv[l]
        conv_y, z = _conv_step(hs, gain_mix, w_all, w_conv[l], st[:, 0, :], st[:, 1, :], tn=tn)
        q, iq, k, v, misc, k_bf, v_bf, ika, ikb, gc, ga = _mixer_common(
            hs, gain_mix, w_all, w_gates, cols, ktabs_s, itabs_s, hd=hd, dq=dq, dkv=dkv, tm=bs, tn=tn)
        w_idx = (misc[:, IDX_DIM:IDX_DIM + IDX_HEADS] * IDX_HEADS ** -0.5).reshape(bs, IDX_HEADS, 1)
        scores = _sample_scores(page_table, iq.reshape(bs, IDX_HEADS, IDX_DIM), w_idx,
                                ika[:, :IDX_DIM].reshape(bs, IDX_DIM, 1),
                                cache_idx_k[l].transpose(0, 2, 1))
        idx = _sample_select(scores, topk=topk_s).reshape(bs, topk_s)
        attn_o = _sample_attention(page_table, idx, q.reshape(bs, N_HEADS, hd),
                                   k.reshape(bs, n_kv, hd), v.reshape(bs, n_kv, hd),
                                   cache_k[l], cache_v[l], topk=topk_s)
        hs = _merge(hs, conv_y, attn_o.reshape(bs, dq).astype(BF16), gc, ga, row(norm_mix_post[l]),
                    w_co, w_ao, w_o, tm=bs, tn=tn // 2)
        outs[4].append(k.reshape(bs, ts, n_kv, hd))
        outs[5].append(v.reshape(bs, ts, n_kv, hd))
        outs[6].append(misc[:, :IDX_DIM].reshape(bs, ts, IDX_DIM))
        outs[7].append(jnp.stack([st[:, 1, :], z], axis=1))
        hs = _ffn_half(hs, row(norm_ffn2_pre[l]), row(norm_ffn2_post[l]), w2gu, w2d, tm=bs, tf=tf)

    return (hp.reshape(bp, t, d), hs.reshape(bs, ts, d)) + tuple(jnp.stack(o) for o in outs)
```

```python
import functools

import numpy as np
import jax
import jax.numpy as jnp
from jax import lax
from jax.experimental import pallas as pl
from jax.experimental.pallas import tpu as pltpu

F32 = jnp.float32
BF16 = jnp.bfloat16
I32 = jnp.int32

N_HEADS = 16
N_KV_HEADS = 4
IDX_HEADS = 16
IDX_DIM = 64
TOPK_MAX = 256
CONV_WIDTH = 3
ROPE_THETA = 500000.0
ROT_DIV = 4
EPS = 1e-6

LANES = 128
VMEM_LIMIT = 56 * 1024 * 1024
LOG2E = 1.4426950408889634
NEG_BIAS = -1e30
INT_MIN = -2 ** 31
INT_MAX = 2 ** 31 - 1
NEG_INF_KEY = int(np.int32(np.uint32(0x807FFFFF)))


def _cparams(sem):
    return pltpu.CompilerParams(dimension_semantics=sem, vmem_limit_bytes=VMEM_LIMIT)


def _rms_scale(x):
    return lax.rsqrt(jnp.mean(x * x, axis=-1, keepdims=True) + EPS)


def _nt_dot(a, b):
    return lax.dot_general(a, b, (((1,), (1,)), ((), ())), preferred_element_type=F32)


def _dot(a, b):
    return jnp.dot(a, b, preferred_element_type=F32)


def _sigmoid(x):
    return 0.5 * jnp.tanh(0.5 * x) + 0.5


def _ffn_kernel(x_ref, pre_ref, post_ref, wg_ref, wu_ref, wd_ref, o_ref, xn_ref, acc_ref):
    j = pl.program_id(1)

    @pl.when(j == 0)
    def _():
        x = x_ref[...]
        xn_ref[...] = (x * _rms_scale(x) * pre_ref[...]).astype(BF16)
        acc_ref[...] = jnp.zeros_like(acc_ref)

    xn = xn_ref[...]
    g = _dot(xn, wg_ref[...])
    u = _dot(xn, wu_ref[...])
    h = (g * _sigmoid(g) * u).astype(BF16)
    acc_ref[...] += _dot(h, wd_ref[...])

    @pl.when(j == pl.num_programs(1) - 1)
    def _():
        y = acc_ref[...]
        o_ref[...] = x_ref[...] + 0.5 * (y * _rms_scale(y) * post_ref[...])


def _ffn_half(x, pre, post, w_gu, w_d, *, tm, tf):
    m, d = x.shape
    f = w_d.shape[0]
    nf = f // tf
    return pl.pallas_call(
        _ffn_kernel,
        grid=(m // tm, nf),
        in_specs=[
            pl.BlockSpec((tm, d), lambda i, j: (i, 0)),
            pl.BlockSpec((1, d), lambda i, j: (0, 0)),
            pl.BlockSpec((1, d), lambda i, j: (0, 0)),
            pl.BlockSpec((d, tf), lambda i, j: (0, j)),
            pl.BlockSpec((d, tf), lambda i, j: (0, nf + j)),
            pl.BlockSpec((tf, d), lambda i, j: (j, 0)),
        ],
        out_specs=pl.BlockSpec((tm, d), lambda i, j: (i, 0)),
        out_shape=jax.ShapeDtypeStruct((m, d), F32),
        scratch_shapes=[pltpu.VMEM((tm, d), BF16), pltpu.VMEM((tm, d), F32)],
        compiler_params=_cparams(("arbitrary", "arbitrary")),
        name="ffn_half",
    )(x, pre, post, w_gu, w_gu, w_d)


def _merge_kernel(h_ref, cy_ref, ao_ref, gc_ref, ga_ref, post_ref, wc_ref, wa_ref, wo_ref,
                  o_ref, acc_ref):
    j = pl.program_id(1)

    @pl.when(j == 0)
    def _():
        acc_ref[...] = jnp.zeros_like(acc_ref)

    mc = _dot(cy_ref[...], wc_ref[...])
    ma = _dot(ao_ref[...], wa_ref[...])
    mix = gc_ref[...].astype(F32) * mc + ga_ref[...].astype(F32) * ma
    acc_ref[...] += _dot(mix.astype(BF16), wo_ref[...])

    @pl.when(j == pl.num_programs(1) - 1)
    def _():
        y = acc_ref[...]
        o_ref[...] = h_ref[...] + y * _rms_scale(y) * post_ref[...]


def _merge(h, conv_y, attn_o, gc, ga, post, w_co, w_ao, w_out, *, tm, tn):
    m, d = h.shape
    return pl.pallas_call(
        _merge_kernel,
        grid=(m // tm, d // tn),
        in_specs=[
            pl.BlockSpec((tm, d), lambda i, j: (i, 0)),
            pl.BlockSpec((tm, d), lambda i, j: (i, 0)),
            pl.BlockSpec((tm, d), lambda i, j: (i, 0)),
            pl.BlockSpec((tm, tn), lambda i, j: (i, j)),
            pl.BlockSpec((tm, tn), lambda i, j: (i, j)),
            pl.BlockSpec((1, d), lambda i, j: (0, 0)),
            pl.BlockSpec((d, tn), lambda i, j: (0, j)),
            pl.BlockSpec((d, tn), lambda i, j: (0, j)),
            pl.BlockSpec((tn, d), lambda i, j: (j, 0)),
        ],
        out_specs=pl.BlockSpec((tm, d), lambda i, j: (i, 0)),
        out_shape=jax.ShapeDtypeStruct((m, d), F32),
        scratch_shapes=[pltpu.VMEM((tm, d), F32)],
        compiler_params=_cparams(("arbitrary", "arbitrary")),
        name="merge",
    )(h, conv_y, attn_o, gc, ga, post, w_co, w_ao, w_out)


def _rms_cast_kernel(h_ref, gain_ref, u_ref):
    x = h_ref[...]
    u_ref[...] = (x * _rms_scale(x) * gain_ref[...]).astype(BF16)


def _rms_cast(h, gain, *, tm):
    m, d = h.shape
    return pl.pallas_call(
        _rms_cast_kernel,
        grid=(m // tm,),
        in_specs=[pl.BlockSpec((tm, d), lambda i: (i, 0)), pl.BlockSpec((1, d), lambda i: (0, 0))],
        out_specs=pl.BlockSpec((tm, d), lambda i: (i, 0)),
        out_shape=jax.ShapeDtypeStruct((m, d), BF16),
        compiler_params=_cparams(("arbitrary",)),
        name="rms_cast",
    )(h, gain)


def _rope(x, a, b, c, half):
    return x * a + pltpu.roll(x, LANES - half, axis=1) * b + pltpu.roll(x, half, axis=1) * c


def _conv_seq_kernel(u_ref, wx_ref, wb_ref, wc_ref, wconv_ref, cy_ref, tail_ref, carry_ref):
    i = pl.program_id(0)
    j = pl.program_id(1)
    xn = u_ref[...]
    z = _dot(xn, wc_ref[...]) * _dot(xn, wx_ref[...])
    b = _dot(xn, wb_ref[...])
    tm = z.shape[0]

    @pl.when(i == 0)
    def _():
        carry_ref[j] = jnp.zeros(carry_ref.shape[1:], F32)

    prev = carry_ref[j]
    row = lax.broadcasted_iota(I32, z.shape, 0)
    z1 = jnp.where(row == 0, prev[1:2, :], pltpu.roll(z, 1, axis=0))
    z2 = jnp.where(row == 0, prev[0:1, :],
                   jnp.where(row == 1, prev[1:2, :], pltpu.roll(z, 2, axis=0)))
    w = wconv_ref[...]
    cy_ref[...] = (b * (z2 * w[0:1, :] + z1 * w[1:2, :] + z * w[2:3, :])).astype(BF16)
    tail = z[tm - 8:, :]
    carry_ref[j] = pltpu.roll(tail, 2, axis=0)
    tail_ref[...] = tail


def _col_blocks(d, tn, first_col):
    assert first_col % tn == 0
    return pl.BlockSpec((d, tn), lambda i, j: (0, first_col // tn + j))


def _conv_seq(u, w_all, w_conv, *, tm, tn):
    m, d = u.shape
    dc = w_conv.shape[1]
    return pl.pallas_call(
        _conv_seq_kernel,
        grid=(m // tm, dc // tn),
        in_specs=[
            pl.BlockSpec((tm, d), lambda i, j: (i, 0)),
            _col_blocks(d, tn, 0), _col_blocks(d, tn, dc), _col_blocks(d, tn, 2 * dc),
            pl.BlockSpec((CONV_WIDTH, tn), lambda i, j: (0, j)),
        ],
        out_specs=[
            pl.BlockSpec((tm, tn), lambda i, j: (i, j)),
            pl.BlockSpec((8, tn), lambda i, j: (i, j)),
        ],
        out_shape=[jax.ShapeDtypeStruct((m, dc), BF16), jax.ShapeDtypeStruct((m // tm * 8, dc), F32)],
        scratch_shapes=[pltpu.VMEM((dc // tn, 8, tn), F32)],
        compiler_params=_cparams(("arbitrary", "arbitrary")),
        name="conv_seq",
    )(u, w_all, w_all, w_all, w_conv)


def _conv_step_kernel(u_ref, wx_ref, wb_ref, wc_ref, wconv_ref, s0_ref, s1_ref, cy_ref, z_ref):
    xn = u_ref[...]
    z = _dot(xn, wc_ref[...]) * _dot(xn, wx_ref[...])
    b = _dot(xn, wb_ref[...])
    w = wconv_ref[...]
    cy_ref[...] = (b * (s0_ref[...] * w[0:1, :] + s1_ref[...] * w[1:2, :] + z * w[2:3, :])
                   ).astype(BF16)
    z_ref[...] = z


def _conv_step(u, w_all, w_conv, s0, s1, *, tn):
    m, d = u.shape
    dc = w_conv.shape[1]
    cspec = pl.BlockSpec((m, tn), lambda i, j: (0, j))
    return pl.pallas_call(
        _conv_step_kernel,
        grid=(1, dc // tn),
        in_specs=[
            pl.BlockSpec((m, d), lambda i, j: (0, 0)),
            _col_blocks(d, tn, 0), _col_blocks(d, tn, dc), _col_blocks(d, tn, 2 * dc),
            pl.BlockSpec((CONV_WIDTH, tn), lambda i, j: (0, j)),
            cspec, cspec,
        ],
        out_specs=[cspec, cspec],
        out_shape=[jax.ShapeDtypeStruct((m, dc), BF16), jax.ShapeDtypeStruct((m, dc), F32)],
        compiler_params=_cparams(("arbitrary", "arbitrary")),
        name="conv_step",
    )(u, w_all, w_all, w_all, w_conv, s0, s1)


def _rope_proj_kernel(u_ref, w_ref, a_ref, b_ref, c_ref, o_ref, *, half, scale):
    y = _dot(u_ref[...], w_ref[...])
    a, b, c = a_ref[...], b_ref[...], c_ref[...]
    for g in range(y.shape[1] // LANES):
        sl = slice(g * LANES, (g + 1) * LANES)
        o_ref[:, sl] = (_rope(y[:, sl], a, b, c, half) * scale).astype(o_ref.dtype)


def _rope_proj(u, w_all, tabs, *, first_col, n, half, scale, tm, tn):
    m, d = u.shape
    tspec = pl.BlockSpec((tm, LANES), lambda i, j: (i, 0))
    return pl.pallas_call(
        functools.partial(_rope_proj_kernel, half=half, scale=scale),
        grid=(m // tm, n // tn),
        in_specs=[
            pl.BlockSpec((tm, d), lambda i, j: (i, 0)),
            _col_blocks(d, tn, first_col),
            tspec, tspec, tspec,
        ],
        out_specs=pl.BlockSpec((tm, tn), lambda i, j: (i, j)),
        out_shape=jax.ShapeDtypeStruct((m, n), BF16),
        compiler_params=_cparams(("arbitrary", "arbitrary")),
        name="rope_proj",
    )(u, w_all, *tabs)


def _kv_proj_kernel(u_ref, wk_ref, wv_ref, wm_ref, ka_ref, kb_ref, kc_ref,
                    ia_ref, ib_ref, ic_ref,
                    k_ref, v_ref, misc_ref, kbf_ref, vbf_ref, ika_ref, ikb_ref, *, dkv):
    xn = u_ref[...]
    yk = _dot(xn, wk_ref[...])
    a, b, c = ka_ref[...], kb_ref[...], kc_ref[...]
    for g in range(dkv // LANES):
        sl = slice(g * LANES, (g + 1) * LANES)
        kg = _rope(yk[:, sl], a, b, c, LANES // ROT_DIV // 2)
        k_ref[:, sl] = kg
        kbf_ref[:, sl] = kg.astype(BF16)
    v = _dot(xn, wv_ref[...])
    v_ref[...] = v
    vbf_ref[...] = v.astype(BF16)
    misc = _rope(_dot(xn, wm_ref[...]), ia_ref[...], ib_ref[...], ic_ref[...],
                 IDX_DIM // ROT_DIV // 2)
    misc_ref[...] = misc
    lane = lax.broadcasted_iota(I32, misc.shape, 1)
    ik_lo = jnp.where(lane < IDX_DIM, misc, 0.0)
    ika_ref[...] = ik_lo.astype(BF16)
    ikb_ref[...] = pltpu.roll(ik_lo, IDX_DIM, axis=1).astype(BF16)


def _kv_proj(u, w_all, ktabs, itabs, *, k_col, v_col, misc_col, dkv, tm):
    m, d = u.shape
    assert k_col % dkv == 0 and v_col % dkv == 0 and misc_col % LANES == 0
    tspec = pl.BlockSpec((tm, LANES), lambda i: (i, 0))
    kvspec = pl.BlockSpec((tm, dkv), lambda i: (i, 0))
    mspec = pl.BlockSpec((tm, LANES), lambda i: (i, 0))
    return pl.pallas_call(
        functools.partial(_kv_proj_kernel, dkv=dkv),
        grid=(m // tm,),
        in_specs=[
            pl.BlockSpec((tm, d), lambda i: (i, 0)),
            pl.BlockSpec((d, dkv), lambda i: (0, k_col // dkv)),
            pl.BlockSpec((d, dkv), lambda i: (0, v_col // dkv)),
            pl.BlockSpec((d, LANES), lambda i: (0, misc_col // LANES)),
            tspec, tspec, tspec, tspec, tspec, tspec,
        ],
        out_specs=[kvspec, kvspec, mspec, kvspec, kvspec, mspec, mspec],
        out_shape=[
            jax.ShapeDtypeStruct((m, dkv), F32), jax.ShapeDtypeStruct((m, dkv), F32),
            jax.ShapeDtypeStruct((m, LANES), F32),
            jax.ShapeDtypeStruct((m, dkv), BF16), jax.ShapeDtypeStruct((m, dkv), BF16),
            jax.ShapeDtypeStruct((m, LANES), BF16), jax.ShapeDtypeStruct((m, LANES), BF16),
        ],
        compiler_params=_cparams(("arbitrary",)),
        name="kv_proj",
    )(u, w_all, w_all, w_all, *ktabs, *itabs)


def _gate_proj_kernel(u_ref, wc_ref, wa_ref, gc_ref, ga_ref):
    xn = u_ref[...]
    gc_ref[...] = _sigmoid(_dot(xn, wc_ref[...])).astype(BF16)
    ga_ref[...] = _sigmoid(_dot(xn, wa_ref[...])).astype(BF16)


def _gate_proj(u, w_gates, *, tm, tn):
    m, d = u.shape
    n = w_gates.shape[1] // 2
    nb = n // tn
    ospec = pl.BlockSpec((tm, tn), lambda i, j: (i, j))
    return pl.pallas_call(
        _gate_proj_kernel,
        grid=(m // tm, nb),
        in_specs=[
            pl.BlockSpec((tm, d), lambda i, j: (i, 0)),
            pl.BlockSpec((d, tn), lambda i, j: (0, j)),
            pl.BlockSpec((d, tn), lambda i, j: (0, nb + j)),
        ],
        out_specs=[ospec, ospec],
        out_shape=[jax.ShapeDtypeStruct((m, n), BF16), jax.ShapeDtypeStruct((m, n), BF16)],
        compiler_params=_cparams(("arbitrary", "arbitrary")),
        name="gate_proj",
    )(u, w_gates, w_gates)


def _ordered_key(score):
    bits = pltpu.bitcast(score, I32)
    return bits ^ ((bits >> 31) & INT_MAX)


def _kth_largest(count_ge, k, like):
    def body(b, lo):
        cand = lo + jnp.left_shift(jnp.int32(1), 31 - b)
        return jnp.where(count_ge(cand) >= k, cand, lo)
    return lax.fori_loop(0, 32, body, jnp.full_like(like, INT_MIN))


def _tie_limit(count_eq_before, need, nbits, like):
    def body(b, lo):
        cand = lo + jnp.left_shift(jnp.int32(1), nbits - 1 - b)
        return jnp.where(count_eq_before(cand) < need, cand, lo)
    return lax.fori_loop(0, nbits, body, jnp.zeros_like(like))


def _prompt_attn_kernel(iq_ref, wt_ref, q_ref, ika_ref, ikb_ref, k_ref, vt_ref, o_ref,
                        key_ref, half_ref, lim_ref, q4_ref, m_ref, acc_ref, lga_ref, lgb_ref,
                        *, tq, sc, topk, n_rep, hd, hb):
    i = pl.program_id(0)
    t0 = i * tq
    n_chunks = (t0 + tq) // sc
    n_pairs = iq_ref.shape[1] // LANES
    n_kv = k_ref.shape[1] // hd
    pos_bits = int(k_ref.shape[0] - 1).bit_length()
    i16_min = -2 ** 15

    def rows(c):
        return pl.ds(pl.multiple_of(c * sc, sc), sc)

    def score_chunk(c, carry):
        ka = ika_ref[rows(c), :]
        kb = ikb_ref[rows(c), :]
        acc = jnp.zeros((sc, tq), F32)
        for p in range(n_pairs):
            iq_p = iq_ref[:, p * LANES:(p + 1) * LANES]
            acc += jnp.maximum(_nt_dot(ka, iq_p), 0.0) * wt_ref[2 * p:2 * p + 1, :]
            acc += jnp.maximum(_nt_dot(kb, iq_p), 0.0) * wt_ref[2 * p + 1:2 * p + 2, :]
        spos = c * sc + lax.broadcasted_iota(I32, (sc, tq), 0)
        tpos = t0 + lax.broadcasted_iota(I32, (sc, tq), 1)
        key = _ordered_key(jnp.where(spos <= tpos, acc, -jnp.inf))
        key_ref[rows(c), :] = key
        half_ref[rows(c), :] = (key >> 16).astype(jnp.int16)
        return carry

    lax.fori_loop(0, n_chunks, score_chunk, 0)

    def half_count(cand, strict=False):
        cand16 = cand.astype(jnp.int16)

        def hits(c):
            blk = half_ref[rows(c), :]
            hit = jnp.where(blk > cand16 if strict else blk >= cand16,
                            jnp.bfloat16(1), jnp.bfloat16(0))
            part = hit[0:16]
            for r in range(1, sc // 16):
                part = part + hit[r * 16:(r + 1) * 16]
            return part

        def body2(c2, cnt):
            return cnt + (hits(2 * c2) + hits(2 * c2 + 1)).astype(F32)

        def body1(c, cnt):
            return cnt + hits(c).astype(F32)

        n2 = n_chunks // 2
        cnt = lax.fori_loop(0, n2, body2, jnp.zeros((16, tq), F32))
        cnt = lax.fori_loop(2 * n2, n_chunks, body1, cnt)
        return cnt.sum(axis=0, keepdims=True).astype(I32)

    def half_search(k):
        def body(b, carry):
            lo, cnt_lo = carry
            cand = lo + jnp.left_shift(jnp.int32(1), 15 - b)
            cnt = half_count(cand)
            ok = cnt >= k
            return jnp.where(ok, cand, lo), jnp.where(ok, cnt, cnt_lo)
        init = (jnp.full((1, tq), i16_min, I32), jnp.full((1, tq), n_chunks * sc, I32))
        return lax.fori_loop(0, 16, body, init)

    thr_hi, _ = half_search(topk)
    n_gt_hi = half_count(thr_hi, strict=True)

    def low_chunk(c, carry):
        key = key_ref[rows(c), :]
        low = (key & 0xFFFF) - 2 ** 15
        half_ref[rows(c), :] = jnp.where((key >> 16) == thr_hi, low, i16_min).astype(jnp.int16)
        return carry

    lax.fori_loop(0, n_chunks, low_chunk, 0)
    thr_lo, n_ge_lo = half_search(topk - n_gt_hi)
    thr = thr_hi * 65536 + (thr_lo + 2 ** 15)
    n_ge = n_gt_hi + n_ge_lo
    has_tie = jnp.max(jnp.where((n_ge > topk) & (thr > NEG_INF_KEY), 1.0, 0.0)) > 0.0
    lim_ref[...] = jnp.full(lim_ref.shape, INT_MAX, I32)

    @pl.when(has_tie)
    def _():
        def column_count(pred):
            def body(c, cnt):
                blk = key_ref[rows(c), :]
                spos = c * sc + lax.broadcasted_iota(I32, (sc, tq), 0)
                hit = jnp.where(pred(blk, spos), 1, 0).astype(I32)
                return cnt + hit.reshape(sc // 8, 8, tq).sum(axis=0)
            cnt = lax.fori_loop(0, n_chunks, body, jnp.zeros((8, tq), I32))
            return cnt.astype(F32).sum(axis=0, keepdims=True).astype(I32)

        need = topk - column_count(lambda blk, spos: blk > thr)
        lim = _tie_limit(
            lambda p: column_count(lambda blk, spos: (blk == thr) & (spos < p)),
            need, pos_bits + 1, jnp.zeros((1, tq), I32))
        lim_ref[...] = jnp.broadcast_to(lim, lim_ref.shape)

    lim = lim_ref[0:1, :]

    def bias_chunk(c, carry):
        blk = key_ref[rows(c), :]
        spos = c * sc + lax.broadcasted_iota(I32, (sc, tq), 0)
        sel = ((blk > thr) | ((blk == thr) & (spos <= lim))) & (blk > NEG_INF_KEY)
        key_ref[rows(c), :] = pltpu.bitcast(jnp.where(sel, 0.0, NEG_BIAS).astype(F32), I32)
        return carry

    lax.fori_loop(0, n_chunks, bias_chunk, 0)

    n_heads = n_kv * n_rep
    n_grp = n_heads // hb
    for j in range(n_grp):
        for r in range(hb):
            head = j * hb + r
            q4_ref[j, r * tq:(r + 1) * tq, :] = q_ref[:, head * hd:(head + 1) * hd]
    m_ref[...] = jnp.full(m_ref.shape, NEG_BIAS, F32)
    acc_ref[...] = jnp.zeros(acc_ref.shape, F32)

    def bias_of(c):
        bias = pltpu.bitcast(key_ref[rows(c), :], F32)
        return jnp.concatenate([bias] * hb, axis=1)

    def qk_store(c, j, bias_w, dst_ref):
        g = j * hb // n_rep
        dst_ref[...] = _nt_dot(k_ref[rows(c), g * hd:(g + 1) * hd], q4_ref[j]) + bias_w

    bufs = (lga_ref, lgb_ref)
    qk_store(0, 0, bias_of(0), bufs[0])

    def attn_chunk(c, carry):
        bias_w = bias_of(c)
        c_next = jnp.minimum(c + 1, n_chunks - 1)
        for j in range(n_grp):
            cur, nxt = bufs[j % 2], bufs[(j + 1) % 2]
            if j + 1 < n_grp:
                qk_store(c, j + 1, bias_w, nxt)
            else:
                qk_store(c_next, 0, bias_of(c_next), nxt)
            logit = cur[...]
            m_old = m_ref[j]
            m_new = jnp.maximum(m_old, jnp.max(logit, axis=0, keepdims=True))
            p = jnp.exp2(logit - m_new).astype(BF16)
            acc_ref[j] = (jnp.exp2(m_old - m_new) * acc_ref[j]
                          + _dot(vt_ref[j * hb // n_rep, c], p))
            m_ref[j] = m_new
        return carry

    lax.fori_loop(0, n_chunks, attn_chunk, 0)
    for j in range(n_grp):
        for r in range(hb):
            head = j * hb + r
            cols = slice(r * tq, (r + 1) * tq)
            out_t = acc_ref[j, 0:hd, cols] / acc_ref[j, hd:hd + 1, cols]
            o_ref[:, head * hd:(head + 1) * hd] = out_t.T.astype(o_ref.dtype)


def _prompt_attention(iq, wt, q, ika, ikb, k_bf, vt, *, tq, sc, topk, hd):
    t, dq = q.shape
    n_kv = k_bf.shape[1] // hd
    n_rep = dq // hd // n_kv
    assert tq >= topk and tq % sc == 0 and t % tq == 0
    hb = 4
    assert (n_kv * n_rep // hb) % 2 == 0
    n_grp = n_kv * n_rep // hb
    resident = functools.partial(pl.BlockSpec, pipeline_mode=pl.Buffered(1))
    return pl.pallas_call(
        functools.partial(_prompt_attn_kernel, tq=tq, sc=sc, topk=topk, n_rep=n_rep, hd=hd, hb=hb),
        grid=(t // tq,),
        in_specs=[
            pl.BlockSpec((tq, iq.shape[1]), lambda i: (i, 0)),
            pl.BlockSpec((wt.shape[0], tq), lambda i: (0, i)),
            pl.BlockSpec((tq, dq), lambda i: (i, 0)),
            resident(ika.shape, lambda i: (0, 0)),
            resident(ikb.shape, lambda i: (0, 0)),
            resident(k_bf.shape, lambda i: (0, 0)),
            resident(vt.shape, lambda i: (0, 0, 0, 0)),
        ],
        out_specs=pl.BlockSpec((tq, dq), lambda i: (i, 0)),
        out_shape=jax.ShapeDtypeStruct((t, dq), BF16),
        scratch_shapes=[
            pltpu.VMEM((t, tq), I32),
            pltpu.VMEM((t, tq), jnp.int16),
            pltpu.VMEM((8, tq), I32),
            pltpu.VMEM((n_grp, hb * tq, hd), BF16),
            pltpu.VMEM((n_grp, 1, hb * tq), F32),
            pltpu.VMEM((n_grp, vt.shape[2], hb * tq), F32),
            pltpu.VMEM((sc, hb * tq), F32),
            pltpu.VMEM((sc, hb * tq), F32),
        ],
        compiler_params=_cparams(("arbitrary",)),
        name="prompt_attention",
    )(iq, wt, q, ika, ikb, k_bf, vt)


def _sample_score_kernel(pt_ref, iq_ref, w_ref, ikn_ref, cache_ref, o_ref, buf_ref, sem_ref,
                         *, n_pages, page):
    b = pl.program_id(0)
    nb = pl.num_programs(0)

    def page_copy(bb, slot, p):
        return pltpu.make_async_copy(cache_ref.at[pt_ref[bb, p]], buf_ref.at[slot, p],
                                     sem_ref.at[slot])

    def start_all(bb, slot):
        def body(p, carry):
            page_copy(bb, slot, p).start()
            return carry
        lax.fori_loop(0, n_pages, body, 0)

    slot = b % 2

    @pl.when(b == 0)
    def _():
        start_all(b, slot)

    @pl.when(b + 1 < nb)
    def _():
        start_all(b + 1, 1 - slot)

    def wait_body(p, carry):
        page_copy(b, slot, p).wait()
        return carry
    lax.fori_loop(0, n_pages, wait_body, 0)

    iq = iq_ref[0]
    w = w_ref[0]

    def head_sum(keys_t):
        s = jnp.maximum(_dot(iq, keys_t), 0.0) * w
        return jnp.sum(s, axis=0, keepdims=True)

    group = min(32, n_pages)

    def group_body(gi, carry):
        p0 = pl.multiple_of(gi * group, group)
        blk = buf_ref[slot, pl.ds(p0, group)]
        keys_t = jnp.concatenate([blk[r] for r in range(group)], axis=1).astype(BF16)
        s = head_sum(keys_t)
        for r in range(group):
            o_ref[0, pl.ds(p0 + r, 1), :] = s[:, r * page:(r + 1) * page]
        return carry
    lax.fori_loop(0, n_pages // group, group_body, 0)
    own = head_sum(jnp.broadcast_to(ikn_ref[0], (ikn_ref.shape[1], page)))
    lane = lax.broadcasted_iota(I32, (1, page), 1)
    o_ref[0, pl.ds(n_pages, 1), :] = jnp.where(lane == 0, own, -jnp.inf)
    o_ref[0, pl.ds(n_pages + 1, 7), :] = jnp.full((7, page), -jnp.inf, F32)


def _sample_scores(page_table, iq, w, ik_new, cache_ik_t):
    bs, n_pages = page_table.shape
    idim, page = cache_ik_t.shape[1:]
    rows = n_pages + 8
    grid_spec = pltpu.PrefetchScalarGridSpec(
        num_scalar_prefetch=1,
        grid=(bs,),
        in_specs=[
            pl.BlockSpec((1,) + iq.shape[1:], lambda b, pt: (b, 0, 0)),
            pl.BlockSpec((1,) + w.shape[1:], lambda b, pt: (b, 0, 0)),
            pl.BlockSpec((1,) + ik_new.shape[1:], lambda b, pt: (b, 0, 0)),
            pl.BlockSpec(memory_space=pl.ANY),
        ],
        out_specs=pl.BlockSpec((1, rows, page), lambda b, pt: (b, 0, 0)),
        scratch_shapes=[pltpu.VMEM((2, n_pages, idim, page), F32),
                        pltpu.SemaphoreType.DMA((2,))],
    )
    return pl.pallas_call(
        functools.partial(_sample_score_kernel, n_pages=n_pages, page=page),
        grid_spec=grid_spec,
        out_shape=jax.ShapeDtypeStruct((bs, rows, page), F32),
        compiler_params=_cparams(("arbitrary",)),
        name="sample_scores",
    )(page_table, iq, w, ik_new, cache_ik_t)


def _slab_pos(shape):
    nd = len(shape)
    return (lax.broadcasted_iota(I32, shape, nd - 2) * shape[-1]
            + lax.broadcasted_iota(I32, shape, nd - 1))


def _sample_thresh_kernel(s_ref, thr_ref, lim_ref, *, topk):
    key = _ordered_key(s_ref[...])
    bs, rows, page = key.shape
    pos = _slab_pos(key.shape)

    def count(pred):
        c = jnp.sum(jnp.where(pred, 1.0, 0.0), axis=1, keepdims=True)
        return jnp.sum(c, axis=2, keepdims=True).astype(I32)

    like = jnp.zeros((bs, 1, 1), I32)
    thr = _kth_largest(lambda v: count(key >= v), topk, like)
    need = topk - count(key > thr)
    lim = _tie_limit(lambda p: count((key == thr) & (pos < p)), need,
                     int(rows * page - 1).bit_length() + 1, like)
    thr_ref[...] = thr
    lim_ref[...] = lim


def _sample_compact_kernel(s_ref, thr_ref, lim_ref, idx_ref, code_ref, *, topk):
    key = _ordered_key(s_ref[0])
    rows, page = key.shape
    thr, lim = thr_ref[0], lim_ref[0]
    sel = ((key > thr) | ((key == thr) & (_slab_pos(key.shape) <= lim))) & (key > NEG_INF_KEY)
    self32 = jnp.where(sel, 1.0, 0.0)

    upper = (lax.broadcasted_iota(I32, (page, page), 0)
             < lax.broadcasted_iota(I32, (page, page), 1))
    within = _dot(self32.astype(BF16), jnp.where(upper, 1.0, 0.0).astype(BF16))
    kpad = 256
    row_tot = jnp.broadcast_to(jnp.sum(self32, axis=1, keepdims=True), (rows, page))
    row_tot = jnp.concatenate([row_tot, jnp.zeros((kpad - rows, page), F32)], axis=0)
    earlier = (lax.broadcasted_iota(I32, (rows, kpad), 1)
               < lax.broadcasted_iota(I32, (rows, kpad), 0))
    before = _dot(jnp.where(earlier, 1.0, 0.0).astype(BF16), row_tot.astype(BF16))
    code_ref[...] = jnp.where(sel, within + before, -1.0)

    slot = lax.broadcasted_iota(I32, (topk, page), 0).astype(F32)
    lane = lax.broadcasted_iota(I32, (topk, page), 1)
    unroll = 8

    def row_body(r8, found):
        for u in range(unroll):
            r = r8 * unroll + u
            hit = code_ref[pl.ds(r, 1), :] == slot
            found = jnp.maximum(found, jnp.where(hit, r * page + lane, -1))
        return found

    found = lax.fori_loop(0, rows // unroll, row_body, jnp.full((topk, page), -1, I32))
    idx_ref[0] = jnp.max(found.astype(F32), axis=1, keepdims=True).astype(I32)


def _sample_select(scores, *, topk):
    bs, rows, page = scores.shape
    assert rows % 8 == 0 and rows <= 256
    one = pl.BlockSpec((1, 1, 1), lambda b: (b, 0, 0))
    thr, lim = pl.pallas_call(
        functools.partial(_sample_thresh_kernel, topk=topk),
        out_shape=[jax.ShapeDtypeStruct((bs, 1, 1), I32)] * 2,
        compiler_params=pltpu.CompilerParams(vmem_limit_bytes=VMEM_LIMIT),
        name="sample_thresh",
    )(scores)
    return pl.pallas_call(
        functools.partial(_sample_compact_kernel, topk=topk),
        grid=(bs,),
        in_specs=[pl.BlockSpec((1, rows, page), lambda b: (b, 0, 0)), one, one],
        out_specs=pl.BlockSpec((1, topk, 1), lambda b: (b, 0, 0)),
        out_shape=jax.ShapeDtypeStruct((bs, topk, 1), I32),
        scratch_shapes=[pltpu.VMEM((rows, page), F32)],
        compiler_params=_cparams(("arbitrary",)),
        name="sample_compact",
    )(scores, thr, lim)


def _sample_attn_kernel(pt_ref, idx_ref, idxv_ref, q_ref, kn_ref, vn_ref, ck_ref, cv_ref, o_ref,
                        kbuf_ref, vbuf_ref, sem_ref, *, page, past, topk, n_rep):
    b = pl.program_id(0)
    nb = pl.num_programs(0)
    n_kv = kn_ref.shape[1]

    def row_copies(bb, slot, j):
        pos = jnp.clip(idx_ref[bb, j], 0, past - 1)
        if page & (page - 1) == 0:
            page_no = lax.shift_right_logical(pos, page.bit_length() - 1)
            off = pos & (page - 1)
        else:
            page_no, off = lax.div(pos, page), lax.rem(pos, page)
        phys = pt_ref[bb, page_no]
        return (pltpu.make_async_copy(ck_ref.at[phys, off], kbuf_ref.at[slot, j], sem_ref.at[0, slot]),
                pltpu.make_async_copy(cv_ref.at[phys, off], vbuf_ref.at[slot, j], sem_ref.at[1, slot]))

    unroll = 8

    def start_all(bb, slot):
        def body(j8, carry):
            for u in range(unroll):
                ck, cv = row_copies(bb, slot, j8 * unroll + u)
                ck.start()
                cv.start()
            return carry
        lax.fori_loop(0, topk // unroll, body, 0)

    slot = b % 2

    @pl.when(b == 0)
    def _():
        start_all(b, slot)

    @pl.when(b + 1 < nb)
    def _():
        start_all(b + 1, 1 - slot)

    for h in range(topk // page):
        slab = pl.ds(h * page, page)
        pltpu.make_async_copy(ck_ref.at[0], kbuf_ref.at[slot, slab], sem_ref.at[0, slot]).wait()
        pltpu.make_async_copy(cv_ref.at[0], vbuf_ref.at[slot, slab], sem_ref.at[1, slot]).wait()

    pos = idxv_ref[0]
    bias = jnp.where((pos >= 0) & (pos < past), 0.0, NEG_BIAS)
    own_sel = jnp.max(jnp.where(pos == past, 1.0, 0.0), axis=1, keepdims=True)
    own_bias = jnp.where(own_sel > 0.0, 0.0, NEG_BIAS)
    for g in range(n_kv):
        rows = slice(g * n_rep, (g + 1) * n_rep)
        qg = q_ref[0, rows, :]
        kg = kbuf_ref[slot, :, g, :].astype(BF16)
        vg = vbuf_ref[slot, :, g, :].astype(BF16)
        kn = kn_ref[0, g:g + 1, :].astype(BF16).astype(F32)
        vn = vn_ref[0, g:g + 1, :].astype(BF16).astype(F32)
        logit = _nt_dot(qg, kg) + bias
        own = jnp.sum(qg.astype(F32) * kn, axis=1, keepdims=True) + own_bias
        m = jnp.maximum(jnp.max(logit, axis=1, keepdims=True), own)
        p = jnp.exp2(logit - m)
        p_own = jnp.exp2(own - m)
        denom = jnp.sum(p, axis=1, keepdims=True) + p_own
        num = _dot(p.astype(BF16), vg) + p_own.astype(BF16).astype(F32) * vn
        o_ref[0, rows, :] = num / denom


def _sample_attention(page_table, idx, q, k_new, v_new, cache_k, cache_v, *, topk):
    bs, n_pages = page_table.shape
    page, n_kv, hd = cache_k.shape[1:]
    n_heads = q.shape[1]
    past = n_pages * page
    assert topk % page == 0 and topk % 8 == 0
    grid_spec = pltpu.PrefetchScalarGridSpec(
        num_scalar_prefetch=2,
        grid=(bs,),
        in_specs=[
            pl.BlockSpec((1, 1, topk), lambda b, pt, ix: (b, 0, 0)),
            pl.BlockSpec((1, n_heads, hd), lambda b, pt, ix: (b, 0, 0)),
            pl.BlockSpec((1, n_kv, hd), lambda b, pt, ix: (b, 0, 0)),
            pl.BlockSpec((1, n_kv, hd), lambda b, pt, ix: (b, 0, 0)),
            pl.BlockSpec(memory_space=pl.ANY),
            pl.BlockSpec(memory_space=pl.ANY),
        ],
        out_specs=pl.BlockSpec((1, n_heads, hd), lambda b, pt, ix: (b, 0, 0)),
        scratch_shapes=[pltpu.VMEM((2, topk, n_kv, hd), F32),
                        pltpu.VMEM((2, topk, n_kv, hd), F32),
                        pltpu.SemaphoreType.DMA((2, 2))],
    )
    return pl.pallas_call(
        functools.partial(_sample_attn_kernel, page=page, past=past, topk=topk,
                          n_rep=n_heads // n_kv),
        grid_spec=grid_spec,
        out_shape=jax.ShapeDtypeStruct((bs, n_heads, hd), F32),
        compiler_params=_cparams(("arbitrary",)),
        name="sample_attention",
    )(page_table, idx, idx.reshape(bs, 1, topk), q, k_new, v_new, cache_k, cache_v)


def _rope_tables(pos, dim, pad_lanes=0):
    rot = dim // ROT_DIV
    half = rot // 2
    inv = ROPE_THETA ** (-np.arange(half, dtype=np.float64) / half)
    ang = np.asarray(pos, np.float64)[:, None] * inv[None, :]
    cos, sin = jnp.asarray(np.cos(ang), F32), jnp.asarray(np.sin(ang), F32)
    width = dim + pad_lanes
    a = jnp.pad(jnp.concatenate([cos, cos], axis=1), ((0, 0), (0, width - rot)), constant_values=1.0)
    b = jnp.pad(-sin, ((0, 0), (0, width - half)))
    c = jnp.pad(sin, ((0, 0), (half, width - rot)))
    reps = LANES // width
    return tuple(jnp.tile(t, (1, reps)) if reps > 1 else t for t in (a, b, c))


def _in_proj_columns(d_model, d_conv, dq, dkv):
    names = ("x", "b", "c", "q", "k", "v", "iq", "ik", "iw", "gc", "ga")
    sizes = (d_conv, d_conv, d_conv, dq, dkv, dkv, IDX_HEADS * IDX_DIM, IDX_DIM, IDX_HEADS,
             d_model, d_model)
    starts = np.concatenate([[0], np.cumsum(sizes)[:-1]])
    return {n: int(s) for n, s in zip(names, starts)}


def _mixer_common(u, w_all, w_gates, cols, ktabs, itabs, *, hd, dq, dkv, tm, tn):
    q = _rope_proj(u, w_all, ktabs, first_col=cols["q"], n=dq, half=hd // ROT_DIV // 2,
                   scale=hd ** -0.5 * LOG2E, tm=tm, tn=tn)
    iq = _rope_proj(u, w_all, itabs[0], first_col=cols["iq"], n=IDX_HEADS * IDX_DIM,
                    half=IDX_DIM // ROT_DIV // 2, scale=IDX_DIM ** -0.5, tm=tm, tn=tn)
    assert cols["iw"] == cols["ik"] + IDX_DIM
    k, v, misc, k_bf, v_bf, ika, ikb = _kv_proj(
        u, w_all, ktabs, itabs[1], k_col=cols["k"], v_col=cols["v"], misc_col=cols["ik"],
        dkv=dkv, tm=tm)
    gc, ga = _gate_proj(u, w_gates, tm=tm, tn=tn)
    return q, iq, k, v, misc, k_bf, v_bf, ika, ikb, gc, ga


def kernel(x_prompt, x_sample, cache_k, cache_v, cache_idx_k, state_conv, page_table,
           norm_ffn1_pre, norm_ffn1_post, w_ffn1_gate_up, w_ffn1_down,
           norm_mix_pre, norm_mix_post, w_in, w_conv, w_conv_out, w_attn_out, w_out,
           norm_ffn2_pre, norm_ffn2_post, w_ffn2_gate_up, w_ffn2_down):
    bp, t, d = x_prompt.shape
    bs, ts, _ = x_sample.shape
    depth = w_in.shape[0]
    page, n_kv, hd = cache_k.shape[2:]
    n_pages = page_table.shape[1]
    past = n_pages * page
    d_conv = w_conv.shape[2]
    dq = w_attn_out.shape[1]
    dkv = n_kv * hd
    assert bp == 1 and ts == 1 and dq == N_HEADS * hd and n_kv == N_KV_HEADS

    tm = min(512, t)
    tf = 512
    tn = 1024
    tq = sc = min(256, t)
    cols = _in_proj_columns(d, d_conv, dq, dkv)
    topk_p = min(TOPK_MAX, t // 4)
    topk_s = min(TOPK_MAX, (past + ts) // 4)

    pos_p = np.arange(t)
    pos_s = np.full((bs,), past)
    ktabs_p, ktabs_s = _rope_tables(pos_p, hd), _rope_tables(pos_s, hd)
    itabs_p = (_rope_tables(pos_p, IDX_DIM), _rope_tables(pos_p, IDX_DIM, LANES - IDX_DIM))
    itabs_s = (_rope_tables(pos_s, IDX_DIM), _rope_tables(pos_s, IDX_DIM, LANES - IDX_DIM))

    hp = x_prompt.reshape(t, d)
    hs = x_sample.reshape(bs, d)
    outs = [[] for _ in range(8)]
    row = lambda a: a.reshape(1, -1)
    for l in range(depth):
        w1gu, w1d = w_ffn1_gate_up[l].astype(BF16), w_ffn1_down[l].astype(BF16)
        w2gu, w2d = w_ffn2_gate_up[l].astype(BF16), w_ffn2_down[l].astype(BF16)
        w_all, w_gates = w_in[l].astype(BF16), w_in[l][:, cols["gc"]:].astype(BF16)
        w_co, w_ao, w_o = (w_conv_out[l].astype(BF16), w_attn_out[l].astype(BF16),
                           w_out[l].astype(BF16))
        gain_mix = row(norm_mix_pre[l])

        hp = _ffn_half(hp, row(norm_ffn1_pre[l]), row(norm_ffn1_post[l]), w1gu, w1d, tm=tm, tf=tf)
        up = _rms_cast(hp, gain_mix, tm=tm)
        conv_y, tail = _conv_seq(up, w_all, w_conv[l], tm=tm, tn=tn)
        q, iq, k, v, misc, k_bf, v_bf, ika, ikb, gc, ga = _mixer_common(
            up, w_all, w_gates, cols, ktabs_p, itabs_p, hd=hd, dq=dq, dkv=dkv, tm=tm, tn=tn)
        wt = (misc[:, IDX_DIM:IDX_DIM + IDX_HEADS] * IDX_HEADS ** -0.5).T
        vt = v_bf.reshape(t // sc, sc, n_kv, hd).transpose(2, 0, 3, 1)
        vt = jnp.concatenate([vt, jnp.ones((n_kv, t // sc, 16, sc), BF16)], axis=2)
        attn_o = _prompt_attention(iq, wt, q, ika, ikb, k_bf, vt, tq=tq, sc=sc, topk=topk_p, hd=hd)
        hp = _merge(hp, conv_y, attn_o, gc, ga, row(norm_mix_post[l]), w_co, w_ao, w_o,
                    tm=tm, tn=tn // 2)
        outs[0].append(k.reshape(bp, t // page, page, n_kv, hd))
        outs[1].append(v.reshape(bp, t // page, page, n_kv, hd))
        outs[2].append(misc[:, :IDX_DIM].reshape(bp, t // page, page, IDX_DIM))
        outs[3].append(tail[tail.shape[0] - (CONV_WIDTH - 1):].reshape(bp, CONV_WIDTH - 1, d_conv))
        hp = _ffn_half(hp, row(norm_ffn2_pre[l]), row(norm_ffn2_post[l]), w2gu, w2d, tm=tm, tf=tf)

        hs = _ffn_half(hs, row(norm_ffn1_pre[l]), row(norm_ffn1_post[l]), w1gu, w1d, tm=bs, tf=tf)
        st = state_conv[l]
        us = _rms_cast(hs, gain_mix, tm=bs)
        conv_y, z = _conv_step(us, w_all, w_conv[l], st[:, 0, :], st[:, 1, :], tn=tn)
        q, iq, k, v, misc, k_bf, v_bf, ika, ikb, gc, ga = _mixer_common(
            us, w_all, w_gates, cols, ktabs_s, itabs_s, hd=hd, dq=dq, dkv=dkv, tm=bs, tn=tn)
        w_idx = (misc[:, IDX_DIM:IDX_DIM + IDX_HEADS] * IDX_HEADS ** -0.5).reshape(bs, IDX_HEADS, 1)
        scores = _sample_scores(page_table, iq.reshape(bs, IDX_HEADS, IDX_DIM), w_idx,
                                ika[:, :IDX_DIM].reshape(bs, IDX_DIM, 1),
                                cache_idx_k[l].transpose(0, 2, 1))
        idx = _sample_select(scores, topk=topk_s).reshape(bs, topk_s)
        attn_o = _sample_attention(page_table, idx, q.reshape(bs, N_HEADS, hd),
                                   k.reshape(bs, n_kv, hd), v.reshape(bs, n_kv, hd),
                                   cache_k[l], cache_v[l], topk=topk_s)
        hs = _merge(hs, conv_y, attn_o.reshape(bs, dq).astype(BF16), gc, ga, row(norm_mix_post[l]),
                    w_co, w_ao, w_o, tm=bs, tn=tn // 2)
        outs[4].append(k.reshape(bs, ts, n_kv, hd))
        outs[5].append(v.reshape(bs, ts, n_kv, hd))
        outs[6].append(misc[:, :IDX_DIM].reshape(bs, ts, IDX_DIM))
        outs[7].append(jnp.stack([st[:, 1, :], z], axis=1))
        hs = _ffn_half(hs, row(norm_ffn2_pre[l]), row(norm_ffn2_post[l]), w2gu, w2d, tm=bs, tf=tf)

    return (hp.reshape(bp, t, d), hs.reshape(bs, ts, d)) + tuple(jnp.stack(o) for o in outs)
```

```python
import functools

import numpy as np
import jax
import jax.numpy as jnp
from jax import lax
from jax.experimental import pallas as pl
from jax.experimental.pallas import tpu as pltpu

F32 = jnp.float32
BF16 = jnp.bfloat16
I32 = jnp.int32

N_HEADS = 16
N_KV_HEADS = 4
IDX_HEADS = 16
IDX_DIM = 64
TOPK_MAX = 256
CONV_WIDTH = 3
ROPE_THETA = 500000.0
ROT_DIV = 4
EPS = 1e-6

LANES = 128
VMEM_LIMIT = 56 * 1024 * 1024
LOG2E = 1.4426950408889634
NEG_BIAS = -1e30
INT_MIN = -2 ** 31
INT_MAX = 2 ** 31 - 1
NEG_INF_KEY = int(np.int32(np.uint32(0x807FFFFF)))


def _cparams(sem):
    return pltpu.CompilerParams(dimension_semantics=sem, vmem_limit_bytes=VMEM_LIMIT)


def _rms_scale(x):
    return lax.rsqrt(jnp.mean(x * x, axis=-1, keepdims=True) + EPS)


def _nt_dot(a, b):
    return lax.dot_general(a, b, (((1,), (1,)), ((), ())), preferred_element_type=F32)


def _dot(a, b):
    return jnp.dot(a, b, preferred_element_type=F32)


def _sigmoid(x):
    return 0.5 * jnp.tanh(0.5 * x) + 0.5


def _ffn_kernel(x_ref, pre_ref, post_ref, wg_ref, wu_ref, wd_ref, o_ref, xn_ref, acc_ref):
    j = pl.program_id(1)

    @pl.when(j == 0)
    def _():
        x = x_ref[...]
        xn_ref[...] = (x * _rms_scale(x) * pre_ref[...]).astype(BF16)
        acc_ref[...] = jnp.zeros_like(acc_ref)

    xn = xn_ref[...]
    g = _dot(xn, wg_ref[...])
    u = _dot(xn, wu_ref[...])
    h = (g * _sigmoid(g) * u).astype(BF16)
    acc_ref[...] += _dot(h, wd_ref[...])

    @pl.when(j == pl.num_programs(1) - 1)
    def _():
        y = acc_ref[...]
        o_ref[...] = x_ref[...] + 0.5 * (y * _rms_scale(y) * post_ref[...])


def _ffn_half(x, pre, post, w_gu, w_d, *, tm, tf):
    m, d = x.shape
    f = w_d.shape[0]
    nf = f // tf
    return pl.pallas_call(
        _ffn_kernel,
        grid=(m // tm, nf),
        in_specs=[
            pl.BlockSpec((tm, d), lambda i, j: (i, 0)),
            pl.BlockSpec((1, d), lambda i, j: (0, 0)),
            pl.BlockSpec((1, d), lambda i, j: (0, 0)),
            pl.BlockSpec((d, tf), lambda i, j: (0, j)),
            pl.BlockSpec((d, tf), lambda i, j: (0, nf + j)),
            pl.BlockSpec((tf, d), lambda i, j: (j, 0)),
        ],
        out_specs=pl.BlockSpec((tm, d), lambda i, j: (i, 0)),
        out_shape=jax.ShapeDtypeStruct((m, d), F32),
        scratch_shapes=[pltpu.VMEM((tm, d), BF16), pltpu.VMEM((tm, d), F32)],
        compiler_params=_cparams(("arbitrary", "arbitrary")),
        name="ffn_half",
    )(x, pre, post, w_gu, w_gu, w_d)


def _merge_kernel(h_ref, cy_ref, ao_ref, gc_ref, ga_ref, post_ref, wc_ref, wa_ref, wo_ref,
                  o_ref, acc_ref):
    j = pl.program_id(1)

    @pl.when(j == 0)
    def _():
        acc_ref[...] = jnp.zeros_like(acc_ref)

    mc = _dot(cy_ref[...], wc_ref[...])
    ma = _dot(ao_ref[...], wa_ref[...])
    mix = gc_ref[...].astype(F32) * mc + ga_ref[...].astype(F32) * ma
    acc_ref[...] += _dot(mix.astype(BF16), wo_ref[...])

    @pl.when(j == pl.num_programs(1) - 1)
    def _():
        y = acc_ref[...]
        o_ref[...] = h_ref[...] + y * _rms_scale(y) * post_ref[...]


def _merge(h, conv_y, attn_o, gc, ga, post, w_co, w_ao, w_out, *, tm, tn):
    m, d = h.shape
    return pl.pallas_call(
        _merge_kernel,
        grid=(m // tm, d // tn),
        in_specs=[
            pl.BlockSpec((tm, d), lambda i, j: (i, 0)),
            pl.BlockSpec((tm, d), lambda i, j: (i, 0)),
            pl.BlockSpec((tm, d), lambda i, j: (i, 0)),
            pl.BlockSpec((tm, tn), lambda i, j: (i, j)),
            pl.BlockSpec((tm, tn), lambda i, j: (i, j)),
            pl.BlockSpec((1, d), lambda i, j: (0, 0)),
            pl.BlockSpec((d, tn), lambda i, j: (0, j)),
            pl.BlockSpec((d, tn), lambda i, j: (0, j)),
            pl.BlockSpec((tn, d), lambda i, j: (j, 0)),
        ],
        out_specs=pl.BlockSpec((tm, d), lambda i, j: (i, 0)),
        out_shape=jax.ShapeDtypeStruct((m, d), F32),
        scratch_shapes=[pltpu.VMEM((tm, d), F32)],
        compiler_params=_cparams(("arbitrary", "arbitrary")),
        name="merge",
    )(h, conv_y, attn_o, gc, ga, post, w_co, w_ao, w_out)


def _rms_cast_kernel(h_ref, gain_ref, u_ref):
    x = h_ref[...]
    u_ref[...] = (x * _rms_scale(x) * gain_ref[...]).astype(BF16)


def _rms_cast(h, gain, *, tm):
    m, d = h.shape
    return pl.pallas_call(
        _rms_cast_kernel,
        grid=(m // tm,),
        in_specs=[pl.BlockSpec((tm, d), lambda i: (i, 0)), pl.BlockSpec((1, d), lambda i: (0, 0))],
        out_specs=pl.BlockSpec((tm, d), lambda i: (i, 0)),
        out_shape=jax.ShapeDtypeStruct((m, d), BF16),
        compiler_params=_cparams(("arbitrary",)),
        name="rms_cast",
    )(h, gain)


def _rope(x, a, b, c, half):
    return x * a + pltpu.roll(x, LANES - half, axis=1) * b + pltpu.roll(x, half, axis=1) * c


def _conv_seq_kernel(u_ref, wx_ref, wb_ref, wc_ref, wconv_ref, cy_ref, tail_ref, carry_ref):
    i = pl.program_id(0)
    j = pl.program_id(1)
    xn = u_ref[...]
    z = _dot(xn, wc_ref[...]) * _dot(xn, wx_ref[...])
    b = _dot(xn, wb_ref[...])
    tm = z.shape[0]

    @pl.when(i == 0)
    def _():
        carry_ref[j] = jnp.zeros(carry_ref.shape[1:], F32)

    prev = carry_ref[j]
    row = lax.broadcasted_iota(I32, z.shape, 0)
    z1 = jnp.where(row == 0, prev[1:2, :], pltpu.roll(z, 1, axis=0))
    z2 = jnp.where(row == 0, prev[0:1, :],
                   jnp.where(row == 1, prev[1:2, :], pltpu.roll(z, 2, axis=0)))
    w = wconv_ref[...]
    cy_ref[...] = (b * (z2 * w[0:1, :] + z1 * w[1:2, :] + z * w[2:3, :])).astype(BF16)
    tail = z[tm - 8:, :]
    carry_ref[j] = pltpu.roll(tail, 2, axis=0)
    tail_ref[...] = tail


def _col_blocks(d, tn, first_col):
    assert first_col % tn == 0
    return pl.BlockSpec((d, tn), lambda i, j: (0, first_col // tn + j))


def _conv_seq(u, w_all, w_conv, *, tm, tn):
    m, d = u.shape
    dc = w_conv.shape[1]
    return pl.pallas_call(
        _conv_seq_kernel,
        grid=(m // tm, dc // tn),
        in_specs=[
            pl.BlockSpec((tm, d), lambda i, j: (i, 0)),
            _col_blocks(d, tn, 0), _col_blocks(d, tn, dc), _col_blocks(d, tn, 2 * dc),
            pl.BlockSpec((CONV_WIDTH, tn), lambda i, j: (0, j)),
        ],
        out_specs=[
            pl.BlockSpec((tm, tn), lambda i, j: (i, j)),
            pl.BlockSpec((8, tn), lambda i, j: (i, j)),
        ],
        out_shape=[jax.ShapeDtypeStruct((m, dc), BF16), jax.ShapeDtypeStruct((m // tm * 8, dc), F32)],
        scratch_shapes=[pltpu.VMEM((dc // tn, 8, tn), F32)],
        compiler_params=_cparams(("arbitrary", "arbitrary")),
        name="conv_seq",
    )(u, w_all, w_all, w_all, w_conv)


def _conv_step_kernel(u_ref, wx_ref, wb_ref, wc_ref, wconv_ref, s0_ref, s1_ref, cy_ref, z_ref):
    xn = u_ref[...]
    z = _dot(xn, wc_ref[...]) * _dot(xn, wx_ref[...])
    b = _dot(xn, wb_ref[...])
    w = wconv_ref[...]
    cy_ref[...] = (b * (s0_ref[...] * w[0:1, :] + s1_ref[...] * w[1:2, :] + z * w[2:3, :])
                   ).astype(BF16)
    z_ref[...] = z


def _conv_step(u, w_all, w_conv, s0, s1, *, tn):
    m, d = u.shape
    dc = w_conv.shape[1]
    cspec = pl.BlockSpec((m, tn), lambda i, j: (0, j))
    return pl.pallas_call(
        _conv_step_kernel,
        grid=(1, dc // tn),
        in_specs=[
            pl.BlockSpec((m, d), lambda i, j: (0, 0)),
            _col_blocks(d, tn, 0), _col_blocks(d, tn, dc), _col_blocks(d, tn, 2 * dc),
            pl.BlockSpec((CONV_WIDTH, tn), lambda i, j: (0, j)),
            cspec, cspec,
        ],
        out_specs=[cspec, cspec],
        out_shape=[jax.ShapeDtypeStruct((m, dc), BF16), jax.ShapeDtypeStruct((m, dc), F32)],
        compiler_params=_cparams(("arbitrary", "arbitrary")),
        name="conv_step",
    )(u, w_all, w_all, w_all, w_conv, s0, s1)


def _rope_proj_kernel(u_ref, w_ref, a_ref, b_ref, c_ref, o_ref, *, half, scale):
    y = _dot(u_ref[...], w_ref[...])
    a, b, c = a_ref[...], b_ref[...], c_ref[...]
    for g in range(y.shape[1] // LANES):
        sl = slice(g * LANES, (g + 1) * LANES)
        o_ref[:, sl] = (_rope(y[:, sl], a, b, c, half) * scale).astype(o_ref.dtype)


def _rope_proj(u, w_all, tabs, *, first_col, n, half, scale, tm, tn):
    m, d = u.shape
    tspec = pl.BlockSpec((tm, LANES), lambda i, j: (i, 0))
    return pl.pallas_call(
        functools.partial(_rope_proj_kernel, half=half, scale=scale),
        grid=(m // tm, n // tn),
        in_specs=[
            pl.BlockSpec((tm, d), lambda i, j: (i, 0)),
            _col_blocks(d, tn, first_col),
            tspec, tspec, tspec,
        ],
        out_specs=pl.BlockSpec((tm, tn), lambda i, j: (i, j)),
        out_shape=jax.ShapeDtypeStruct((m, n), BF16),
        compiler_params=_cparams(("arbitrary", "arbitrary")),
        name="rope_proj",
    )(u, w_all, *tabs)


def _kv_proj_kernel(u_ref, wk_ref, wv_ref, wm_ref, ka_ref, kb_ref, kc_ref,
                    ia_ref, ib_ref, ic_ref,
                    k_ref, v_ref, misc_ref, kbf_ref, vbf_ref, ika_ref, ikb_ref, *, dkv):
    xn = u_ref[...]
    yk = _dot(xn, wk_ref[...])
    a, b, c = ka_ref[...], kb_ref[...], kc_ref[...]
    for g in range(dkv // LANES):
        sl = slice(g * LANES, (g + 1) * LANES)
        kg = _rope(yk[:, sl], a, b, c, LANES // ROT_DIV // 2)
        k_ref[:, sl] = kg
        kbf_ref[:, sl] = kg.astype(BF16)
    v = _dot(xn, wv_ref[...])
    v_ref[...] = v
    vbf_ref[...] = v.astype(BF16)
    misc = _rope(_dot(xn, wm_ref[...]), ia_ref[...], ib_ref[...], ic_ref[...],
                 IDX_DIM // ROT_DIV // 2)
    misc_ref[...] = misc
    lane = lax.broadcasted_iota(I32, misc.shape, 1)
    ik_lo = jnp.where(lane < IDX_DIM, misc, 0.0)
    ika_ref[...] = ik_lo.astype(BF16)
    ikb_ref[...] = pltpu.roll(ik_lo, IDX_DIM, axis=1).astype(BF16)


def _kv_proj(u, w_all, ktabs, itabs, *, k_col, v_col, misc_col, dkv, tm):
    m, d = u.shape
    assert k_col % dkv == 0 and v_col % dkv == 0 and misc_col % LANES == 0
    tspec = pl.BlockSpec((tm, LANES), lambda i: (i, 0))
    kvspec = pl.BlockSpec((tm, dkv), lambda i: (i, 0))
    mspec = pl.BlockSpec((tm, LANES), lambda i: (i, 0))
    return pl.pallas_call(
        functools.partial(_kv_proj_kernel, dkv=dkv),
        grid=(m // tm,),
        in_specs=[
            pl.BlockSpec((tm, d), lambda i: (i, 0)),
            pl.BlockSpec((d, dkv), lambda i: (0, k_col // dkv)),
            pl.BlockSpec((d, dkv), lambda i: (0, v_col // dkv)),
            pl.BlockSpec((d, LANES), lambda i: (0, misc_col // LANES)),
            tspec, tspec, tspec, tspec, tspec, tspec,
        ],
        out_specs=[kvspec, kvspec, mspec, kvspec, kvspec, mspec, mspec],
        out_shape=[
            jax.ShapeDtypeStruct((m, dkv), F32), jax.ShapeDtypeStruct((m, dkv), F32),
            jax.ShapeDtypeStruct((m, LANES), F32),
            jax.ShapeDtypeStruct((m, dkv), BF16), jax.ShapeDtypeStruct((m, dkv), BF16),
            jax.ShapeDtypeStruct((m, LANES), BF16), jax.ShapeDtypeStruct((m, LANES), BF16),
        ],
        compiler_params=_cparams(("arbitrary",)),
        name="kv_proj",
    )(u, w_all, w_all, w_all, *ktabs, *itabs)


def _gate_proj_kernel(u_ref, wc_ref, wa_ref, gc_ref, ga_ref):
    xn = u_ref[...]
    gc_ref[...] = _sigmoid(_dot(xn, wc_ref[...])).astype(BF16)
    ga_ref[...] = _sigmoid(_dot(xn, wa_ref[...])).astype(BF16)


def _gate_proj(u, w_gates, *, tm, tn):
    m, d = u.shape
    n = w_gates.shape[1] // 2
    nb = n // tn
    ospec = pl.BlockSpec((tm, tn), lambda i, j: (i, j))
    return pl.pallas_call(
        _gate_proj_kernel,
        grid=(m // tm, nb),
        in_specs=[
            pl.BlockSpec((tm, d), lambda i, j: (i, 0)),
            pl.BlockSpec((d, tn), lambda i, j: (0, j)),
            pl.BlockSpec((d, tn), lambda i, j: (0, nb + j)),
        ],
        out_specs=[ospec, ospec],
        out_shape=[jax.ShapeDtypeStruct((m, n), BF16), jax.ShapeDtypeStruct((m, n), BF16)],
        compiler_params=_cparams(("arbitrary", "arbitrary")),
        name="gate_proj",
    )(u, w_gates, w_gates)


def _ordered_key(score):
    bits = pltpu.bitcast(score, I32)
    return bits ^ ((bits >> 31) & INT_MAX)


def _kth_largest(count_ge, k, like):
    def body(b, lo):
        cand = lo + jnp.left_shift(jnp.int32(1), 31 - b)
        return jnp.where(count_ge(cand) >= k, cand, lo)
    return lax.fori_loop(0, 32, body, jnp.full_like(like, INT_MIN))


def _tie_limit(count_eq_before, need, nbits, like):
    def body(b, lo):
        cand = lo + jnp.left_shift(jnp.int32(1), nbits - 1 - b)
        return jnp.where(count_eq_before(cand) < need, cand, lo)
    return lax.fori_loop(0, nbits, body, jnp.zeros_like(like))


def _prompt_attn_kernel(iq_ref, wt_ref, q_ref, ika_ref, ikb_ref, k_ref, vt_ref, o_ref,
                        key_ref, half_ref, lim_ref, q4_ref, m_ref, acc_ref, lga_ref, lgb_ref,
                        *, tq, sc, topk, n_rep, hd, hb):
    i = pl.program_id(0)
    t0 = i * tq
    n_chunks = (t0 + tq) // sc
    n_pairs = iq_ref.shape[1] // LANES
    n_kv = k_ref.shape[1] // hd
    pos_bits = int(k_ref.shape[0] - 1).bit_length()
    i16_min = -2 ** 15

    def rows(c):
        return pl.ds(pl.multiple_of(c * sc, sc), sc)

    def score_chunk(c, carry):
        ka = ika_ref[rows(c), :]
        kb = ikb_ref[rows(c), :]
        acc = jnp.zeros((sc, tq), F32)
        for p in range(n_pairs):
            iq_p = iq_ref[:, p * LANES:(p + 1) * LANES]
            acc += jnp.maximum(_nt_dot(ka, iq_p), 0.0) * wt_ref[2 * p:2 * p + 1, :]
            acc += jnp.maximum(_nt_dot(kb, iq_p), 0.0) * wt_ref[2 * p + 1:2 * p + 2, :]
        spos = c * sc + lax.broadcasted_iota(I32, (sc, tq), 0)
        tpos = t0 + lax.broadcasted_iota(I32, (sc, tq), 1)
        key = _ordered_key(jnp.where(spos <= tpos, acc, -jnp.inf))
        key_ref[rows(c), :] = key
        half_ref[rows(c), :] = (key >> 16).astype(jnp.int16)
        return carry

    lax.fori_loop(0, n_chunks, score_chunk, 0)

    def half_count(cand, strict=False):
        cand16 = cand.astype(jnp.int16)

        def hits(c):
            blk = half_ref[rows(c), :]
            hit = jnp.where(blk > cand16 if strict else blk >= cand16,
                            jnp.bfloat16(1), jnp.bfloat16(0))
            part = hit[0:16]
            for r in range(1, sc // 16):
                part = part + hit[r * 16:(r + 1) * 16]
            return part

        def body2(c2, cnt):
            return cnt + (hits(2 * c2) + hits(2 * c2 + 1)).astype(F32)

        def body1(c, cnt):
            return cnt + hits(c).astype(F32)

        n2 = n_chunks // 2
        cnt = lax.fori_loop(0, n2, body2, jnp.zeros((16, tq), F32))
        cnt = lax.fori_loop(2 * n2, n_chunks, body1, cnt)
        return cnt.sum(axis=0, keepdims=True).astype(I32)

    def half_search(k):
        def body(b, carry):
            lo, cnt_lo = carry
            cand = lo + jnp.left_shift(jnp.int32(1), 15 - b)
            cnt = half_count(cand)
            ok = cnt >= k
            return jnp.where(ok, cand, lo), jnp.where(ok, cnt, cnt_lo)
        init = (jnp.full((1, tq), i16_min, I32), jnp.full((1, tq), n_chunks * sc, I32))
        return lax.fori_loop(0, 16, body, init)

    thr_hi, _ = half_search(topk)
    n_gt_hi = half_count(thr_hi, strict=True)

    def low_chunk(c, carry):
        key = key_ref[rows(c), :]
        low = (key & 0xFFFF) - 2 ** 15
        half_ref[rows(c), :] = jnp.where((key >> 16) == thr_hi, low, i16_min).astype(jnp.int16)
        return carry

    lax.fori_loop(0, n_chunks, low_chunk, 0)
    thr_lo, n_ge_lo = half_search(topk - n_gt_hi)
    thr = thr_hi * 65536 + (thr_lo + 2 ** 15)
    n_ge = n_gt_hi + n_ge_lo
    has_tie = jnp.max(jnp.where((n_ge > topk) & (thr > NEG_INF_KEY), 1.0, 0.0)) > 0.0
    lim_ref[...] = jnp.full(lim_ref.shape, INT_MAX, I32)

    @pl.when(has_tie)
    def _():
        def column_count(pred):
            def body(c, cnt):
                blk = key_ref[rows(c), :]
                spos = c * sc + lax.broadcasted_iota(I32, (sc, tq), 0)
                hit = jnp.where(pred(blk, spos), 1, 0).astype(I32)
                return cnt + hit.reshape(sc // 8, 8, tq).sum(axis=0)
            cnt = lax.fori_loop(0, n_chunks, body, jnp.zeros((8, tq), I32))
            return cnt.astype(F32).sum(axis=0, keepdims=True).astype(I32)

        need = topk - column_count(lambda blk, spos: blk > thr)
        lim = _tie_limit(
            lambda p: column_count(lambda blk, spos: (blk == thr) & (spos < p)),
            need, pos_bits + 1, jnp.zeros((1, tq), I32))
        lim_ref[...] = jnp.broadcast_to(lim, lim_ref.shape)

    lim = lim_ref[0:1, :]

    def bias_chunk(c, carry):
        blk = key_ref[rows(c), :]
        spos = c * sc + lax.broadcasted_iota(I32, (sc, tq), 0)
        sel = ((blk > thr) | ((blk == thr) & (spos <= lim))) & (blk > NEG_INF_KEY)
        key_ref[rows(c), :] = pltpu.bitcast(jnp.where(sel, 0.0, NEG_BIAS).astype(F32), I32)
        return carry

    lax.fori_loop(0, n_chunks, bias_chunk, 0)

    n_heads = n_kv * n_rep
    n_grp = n_heads // hb
    for j in range(n_grp):
        for r in range(hb):
            head = j * hb + r
            q4_ref[j, r * tq:(r + 1) * tq, :] = q_ref[:, head * hd:(head + 1) * hd]
    m_ref[...] = jnp.full(m_ref.shape, NEG_BIAS, F32)
    acc_ref[...] = jnp.zeros(acc_ref.shape, F32)

    def bias_of(c):
        bias = pltpu.bitcast(key_ref[rows(c), :], F32)
        return jnp.concatenate([bias] * hb, axis=1)

    def qk_store(c, j, bias_w, dst_ref):
        g = j * hb // n_rep
        dst_ref[...] = _nt_dot(k_ref[rows(c), g * hd:(g + 1) * hd], q4_ref[j]) + bias_w

    bufs = (lga_ref, lgb_ref)
    qk_store(0, 0, bias_of(0), bufs[0])

    def attn_chunk(c, carry):
        bias_w = bias_of(c)
        c_next = jnp.minimum(c + 1, n_chunks - 1)
        for j in range(n_grp):
            cur, nxt = bufs[j % 2], bufs[(j + 1) % 2]
            if j + 1 < n_grp:
                qk_store(c, j + 1, bias_w, nxt)
            else:
                qk_store(c_next, 0, bias_of(c_next), nxt)
            logit = cur[...]
            m_old = m_ref[j]
            m_new = jnp.maximum(m_old, jnp.max(logit, axis=0, keepdims=True))
            p = jnp.exp2(logit - m_new).astype(BF16)
            acc_ref[j] = (jnp.exp2(m_old - m_new) * acc_ref[j]
                          + _dot(vt_ref[j * hb // n_rep, c], p))
            m_ref[j] = m_new
        return carry

    lax.fori_loop(0, n_chunks, attn_chunk, 0)
    for j in range(n_grp):
        for r in range(hb):
            head = j * hb + r
            cols = slice(r * tq, (r + 1) * tq)
            out_t = acc_ref[j, 0:hd, cols] / acc_ref[j, hd:hd + 1, cols]
            o_ref[:, head * hd:(head + 1) * hd] = out_t.T.astype(o_ref.dtype)


def _prompt_attention(iq, wt, q, ika, ikb, k_bf, vt, *, tq, sc, topk, hd):
    t, dq = q.shape
    n_kv = k_bf.shape[1] // hd
    n_rep = dq // hd // n_kv
    assert tq >= topk and tq % sc == 0 and t % tq == 0
    hb = 4
    assert (n_kv * n_rep // hb) % 2 == 0
    n_grp = n_kv * n_rep // hb
    resident = functools.partial(pl.BlockSpec, pipeline_mode=pl.Buffered(1))
    return pl.pallas_call(
        functools.partial(_prompt_attn_kernel, tq=tq, sc=sc, topk=topk, n_rep=n_rep, hd=hd, hb=hb),
        grid=(t // tq,),
        in_specs=[
            pl.BlockSpec((tq, iq.shape[1]), lambda i: (i, 0)),
            pl.BlockSpec((wt.shape[0], tq), lambda i: (0, i)),
            pl.BlockSpec((tq, dq), lambda i: (i, 0)),
            resident(ika.shape, lambda i: (0, 0)),
            resident(ikb.shape, lambda i: (0, 0)),
            resident(k_bf.shape, lambda i: (0, 0)),
            resident(vt.shape, lambda i: (0, 0, 0, 0)),
        ],
        out_specs=pl.BlockSpec((tq, dq), lambda i: (i, 0)),
        out_shape=jax.ShapeDtypeStruct((t, dq), BF16),
        scratch_shapes=[
            pltpu.VMEM((t, tq), I32),
            pltpu.VMEM((t, tq), jnp.int16),
            pltpu.VMEM((8, tq), I32),
            pltpu.VMEM((n_grp, hb * tq, hd), BF16),
            pltpu.VMEM((n_grp, 1, hb * tq), F32),
            pltpu.VMEM((n_grp, vt.shape[2], hb * tq), F32),
            pltpu.VMEM((sc, hb * tq), F32),
            pltpu.VMEM((sc, hb * tq), F32),
        ],
        compiler_params=_cparams(("arbitrary",)),
        name="prompt_attention",
    )(iq, wt, q, ika, ikb, k_bf, vt)


def _sample_score_kernel(pt_ref, iq_ref, w_ref, ikn_ref, cache_ref, o_ref, buf_ref, sem_ref,
                         *, n_pages, page):
    b = pl.program_id(0)
    nb = pl.num_programs(0)

    def page_copy(bb, slot, p):
        return pltpu.make_async_copy(cache_ref.at[pt_ref[bb, p]], buf_ref.at[slot, p],
                                     sem_ref.at[slot])

    def start_all(bb, slot):
        def body(p, carry):
            page_copy(bb, slot, p).start()
            return carry
        lax.fori_loop(0, n_pages, body, 0)

    slot = b % 2

    @pl.when(b == 0)
    def _():
        start_all(b, slot)

    @pl.when(b + 1 < nb)
    def _():
        start_all(b + 1, 1 - slot)

    def wait_body(p, carry):
        page_copy(b, slot, p).wait()
        return carry
    lax.fori_loop(0, n_pages, wait_body, 0)

    iq = iq_ref[0]
    w = w_ref[0]

    def head_sum(keys_t):
        s = jnp.maximum(_dot(iq, keys_t), 0.0) * w
        return jnp.sum(s, axis=0, keepdims=True)

    group = min(32, n_pages)

    def group_body(gi, carry):
        p0 = pl.multiple_of(gi * group, group)
        blk = buf_ref[slot, pl.ds(p0, group)]
        keys_t = jnp.concatenate([blk[r] for r in range(group)], axis=1).astype(BF16)
        s = head_sum(keys_t)
        for r in range(group):
            o_ref[0, pl.ds(p0 + r, 1), :] = s[:, r * page:(r + 1) * page]
        return carry
    lax.fori_loop(0, n_pages // group, group_body, 0)
    own = head_sum(jnp.broadcast_to(ikn_ref[0], (ikn_ref.shape[1], page)))
    lane = lax.broadcasted_iota(I32, (1, page), 1)
    o_ref[0, pl.ds(n_pages, 1), :] = jnp.where(lane == 0, own, -jnp.inf)
    o_ref[0, pl.ds(n_pages + 1, 7), :] = jnp.full((7, page), -jnp.inf, F32)


def _sample_scores(page_table, iq, w, ik_new, cache_ik_t):
    bs, n_pages = page_table.shape
    idim, page = cache_ik_t.shape[1:]
    rows = n_pages + 8
    grid_spec = pltpu.PrefetchScalarGridSpec(
        num_scalar_prefetch=1,
        grid=(bs,),
        in_specs=[
            pl.BlockSpec((1,) + iq.shape[1:], lambda b, pt: (b, 0, 0)),
            pl.BlockSpec((1,) + w.shape[1:], lambda b, pt: (b, 0, 0)),
            pl.BlockSpec((1,) + ik_new.shape[1:], lambda b, pt: (b, 0, 0)),
            pl.BlockSpec(memory_space=pl.ANY),
        ],
        out_specs=pl.BlockSpec((1, rows, page), lambda b, pt: (b, 0, 0)),
        scratch_shapes=[pltpu.VMEM((2, n_pages, idim, page), F32),
                        pltpu.SemaphoreType.DMA((2,))],
    )
    return pl.pallas_call(
        functools.partial(_sample_score_kernel, n_pages=n_pages, page=page),
        grid_spec=grid_spec,
        out_shape=jax.ShapeDtypeStruct((bs, rows, page), F32),
        compiler_params=_cparams(("arbitrary",)),
        name="sample_scores",
    )(page_table, iq, w, ik_new, cache_ik_t)


def _slab_pos(shape):
    nd = len(shape)
    return (lax.broadcasted_iota(I32, shape, nd - 2) * shape[-1]
            + lax.broadcasted_iota(I32, shape, nd - 1))


def _sample_thresh_kernel(s_ref, thr_ref, lim_ref, *, topk):
    key = _ordered_key(s_ref[...])
    bs, rows, page = key.shape
    pos = _slab_pos(key.shape)

    def count(pred):
        c = jnp.sum(jnp.where(pred, 1.0, 0.0), axis=1, keepdims=True)
        return jnp.sum(c, axis=2, keepdims=True).astype(I32)

    like = jnp.zeros((bs, 1, 1), I32)
    thr = _kth_largest(lambda v: count(key >= v), topk, like)
    need = topk - count(key > thr)
    lim = _tie_limit(lambda p: count((key == thr) & (pos < p)), need,
                     int(rows * page - 1).bit_length() + 1, like)
    thr_ref[...] = thr
    lim_ref[...] = lim


def _sample_compact_kernel(s_ref, thr_ref, lim_ref, idx_ref, *, topk):
    key = _ordered_key(s_ref[0])
    rows, page = key.shape
    kpad = 256
    thr, lim = thr_ref[0], lim_ref[0]
    sel = ((key > thr) | ((key == thr) & (_slab_pos(key.shape) <= lim))) & (key > NEG_INF_KEY)
    self32 = jnp.where(sel, 1.0, 0.0)
    zpad = jnp.zeros((kpad - rows, page), F32)

    def ones_where(pred):
        return jnp.where(pred, 1.0, 0.0).astype(BF16)

    upper = lax.broadcasted_iota(I32, (page, page), 0) < lax.broadcasted_iota(I32, (page, page), 1)
    within = _dot(self32.astype(BF16), ones_where(upper))
    tot = jnp.broadcast_to(jnp.sum(self32, axis=1, keepdims=True), (rows, page))
    tot = jnp.concatenate([tot, zpad], axis=0)
    earlier = lax.broadcasted_iota(I32, (kpad, kpad), 1) < lax.broadcasted_iota(I32, (kpad, kpad), 0)
    before = _dot(ones_where(earlier), tot.astype(BF16))

    before_row = before.T[0:1, :]
    ends_row = (before + tot).T[0:1, :]
    slot_r = lax.broadcasted_iota(I32, (topk, kpad), 0).astype(F32)
    owner = ones_where((before_row <= slot_r) & (slot_r < ends_row))
    code = jnp.concatenate([jnp.where(sel, within, -1.0), zpad], axis=0)
    row_id = lax.broadcasted_iota(I32, (kpad, page), 0).astype(F32)
    fetched = _dot(owner, jnp.concatenate([code, before, row_id], axis=1).astype(BF16))
    code_g, before_g, row_g = (fetched[:, 0:page], fetched[:, page:2 * page],
                               fetched[:, 2 * page:3 * page])
    slot = lax.broadcasted_iota(I32, (topk, page), 0).astype(F32)
    lane = lax.broadcasted_iota(I32, (topk, page), 1).astype(F32)
    total = (before + tot)[kpad - 1:kpad, :]
    hit = (code_g == slot - before_g) & (slot < total)
    lane_sel = jnp.max(jnp.where(hit, lane, -1.0), axis=1, keepdims=True)
    pos = jnp.where(lane_sel >= 0.0, row_g[:, 0:1] * page + lane_sel, -1.0)
    idx_ref[0] = pos.astype(I32)


def _sample_select(scores, *, topk):
    bs, rows, page = scores.shape
    assert rows % 8 == 0 and rows <= 256
    one = pl.BlockSpec((1, 1, 1), lambda b: (b, 0, 0))
    thr, lim = pl.pallas_call(
        functools.partial(_sample_thresh_kernel, topk=topk),
        out_shape=[jax.ShapeDtypeStruct((bs, 1, 1), I32)] * 2,
        compiler_params=pltpu.CompilerParams(vmem_limit_bytes=VMEM_LIMIT),
        name="sample_thresh",
    )(scores)
    return pl.pallas_call(
        functools.partial(_sample_compact_kernel, topk=topk),
        grid=(bs,),
        in_specs=[pl.BlockSpec((1, rows, page), lambda b: (b, 0, 0)), one, one],
        out_specs=pl.BlockSpec((1, topk, 1), lambda b: (b, 0, 0)),
        out_shape=jax.ShapeDtypeStruct((bs, topk, 1), I32),
        compiler_params=_cparams(("arbitrary",)),
        name="sample_compact",
    )(scores, thr, lim)


def _sample_attn_kernel(pt_ref, idx_ref, idxv_ref, q_ref, kn_ref, vn_ref, ck_ref, cv_ref, o_ref,
                        kbuf_ref, vbuf_ref, sem_ref, *, page, past, topk, n_rep):
    b = pl.program_id(0)
    nb = pl.num_programs(0)
    n_kv = kn_ref.shape[1]

    def row_copies(bb, slot, j):
        pos = jnp.clip(idx_ref[bb * topk + j], 0, past - 1)
        if page & (page - 1) == 0:
            page_no = lax.shift_right_logical(pos, page.bit_length() - 1)
            off = pos & (page - 1)
        else:
            page_no, off = lax.div(pos, page), lax.rem(pos, page)
        phys = pt_ref[bb * (past // page) + page_no]
        return (pltpu.make_async_copy(ck_ref.at[phys, off], kbuf_ref.at[slot, j], sem_ref.at[0, slot]),
                pltpu.make_async_copy(cv_ref.at[phys, off], vbuf_ref.at[slot, j], sem_ref.at[1, slot]))

    unroll = 8

    def start_all(bb, slot):
        def body(j8, carry):
            for u in range(unroll):
                ck, cv = row_copies(bb, slot, j8 * unroll + u)
                ck.start()
                cv.start()
            return carry
        lax.fori_loop(0, topk // unroll, body, 0)

    slot = b % 2

    @pl.when(b == 0)
    def _():
        start_all(b, slot)

    @pl.when(b + 1 < nb)
    def _():
        start_all(b + 1, 1 - slot)

    for h in range(topk // page):
        slab = pl.ds(h * page, page)
        pltpu.make_async_copy(ck_ref.at[0], kbuf_ref.at[slot, slab], sem_ref.at[0, slot]).wait()
        pltpu.make_async_copy(cv_ref.at[0], vbuf_ref.at[slot, slab], sem_ref.at[1, slot]).wait()

    pos = idxv_ref[0]
    bias = jnp.where((pos >= 0) & (pos < past), 0.0, NEG_BIAS)
    own_sel = jnp.max(jnp.where(pos == past, 1.0, 0.0), axis=1, keepdims=True)
    own_bias = jnp.where(own_sel > 0.0, 0.0, NEG_BIAS)
    for g in range(n_kv):
        rows = slice(g * n_rep, (g + 1) * n_rep)
        qg = q_ref[0, rows, :]
        kg = kbuf_ref[slot, :, g, :].astype(BF16)
        vg = vbuf_ref[slot, :, g, :].astype(BF16)
        kn = kn_ref[0, g:g + 1, :].astype(BF16).astype(F32)
        vn = vn_ref[0, g:g + 1, :].astype(BF16).astype(F32)
        logit = _nt_dot(qg, kg) + bias
        own = jnp.sum(qg.astype(F32) * kn, axis=1, keepdims=True) + own_bias
        m = jnp.maximum(jnp.max(logit, axis=1, keepdims=True), own)
        p = jnp.exp2(logit - m)
        p_own = jnp.exp2(own - m)
        denom = jnp.sum(p, axis=1, keepdims=True) + p_own
        num = _dot(p.astype(BF16), vg) + p_own.astype(BF16).astype(F32) * vn
        o_ref[0, rows, :] = num / denom


def _sample_attention(page_table, idx, q, k_new, v_new, cache_k, cache_v, *, topk):
    bs, n_pages = page_table.shape
    page, n_kv, hd = cache_k.shape[1:]
    n_heads = q.shape[1]
    past = n_pages * page
    assert topk % page == 0 and topk % 8 == 0
    grid_spec = pltpu.PrefetchScalarGridSpec(
        num_scalar_prefetch=2,
        grid=(bs,),
        in_specs=[
            pl.BlockSpec((1, 1, topk), lambda b, pt, ix: (b, 0, 0)),
            pl.BlockSpec((1, n_heads, hd), lambda b, pt, ix: (b, 0, 0)),
            pl.BlockSpec((1, n_kv, hd), lambda b, pt, ix: (b, 0, 0)),
            pl.BlockSpec((1, n_kv, hd), lambda b, pt, ix: (b, 0, 0)),
            pl.BlockSpec(memory_space=pl.ANY),
            pl.BlockSpec(memory_space=pl.ANY),
        ],
        out_specs=pl.BlockSpec((1, n_heads, hd), lambda b, pt, ix: (b, 0, 0)),
        scratch_shapes=[pltpu.VMEM((2, topk, n_kv, hd), F32),
                        pltpu.VMEM((2, topk, n_kv, hd), F32),
                        pltpu.SemaphoreType.DMA((2, 2))],
    )
    return pl.pallas_call(
        functools.partial(_sample_attn_kernel, page=page, past=past, topk=topk,
                          n_rep=n_heads // n_kv),
        grid_spec=grid_spec,
        out_shape=jax.ShapeDtypeStruct((bs, n_heads, hd), F32),
        compiler_params=_cparams(("arbitrary",)),
        name="sample_attention",
    )(page_table.reshape(-1), idx.reshape(-1), idx.reshape(bs, 1, topk), q, k_new, v_new,
      cache_k, cache_v)


def _rope_tables(pos, dim, pad_lanes=0):
    rot = dim // ROT_DIV
    half = rot // 2
    inv = ROPE_THETA ** (-np.arange(half, dtype=np.float64) / half)
    ang = np.asarray(pos, np.float64)[:, None] * inv[None, :]
    cos, sin = jnp.asarray(np.cos(ang), F32), jnp.asarray(np.sin(ang), F32)
    width = dim + pad_lanes
    a = jnp.pad(jnp.concatenate([cos, cos], axis=1), ((0, 0), (0, width - rot)), constant_values=1.0)
    b = jnp.pad(-sin, ((0, 0), (0, width - half)))
    c = jnp.pad(sin, ((0, 0), (half, width - rot)))
    reps = LANES // width
    return tuple(jnp.tile(t, (1, reps)) if reps > 1 else t for t in (a, b, c))


def _cast_kernel(x_ref, o_ref):
    o_ref[...] = x_ref[...].astype(o_ref.dtype)


def _to_bf16(w, *, tn):
    d, n = w.shape
    return pl.pallas_call(
        _cast_kernel,
        grid=(pl.cdiv(n, tn),),
        in_specs=[pl.BlockSpec((d, tn), lambda j: (0, j))],
        out_specs=pl.BlockSpec((d, tn), lambda j: (0, j)),
        out_shape=jax.ShapeDtypeStruct((d, n), BF16),
        compiler_params=_cparams(("arbitrary",)),
        name="to_bf16",
    )(w)


def _in_proj_columns(d_model, d_conv, dq, dkv):
    names = ("x", "b", "c", "q", "k", "v", "iq", "ik", "iw", "gc", "ga")
    sizes = (d_conv, d_conv, d_conv, dq, dkv, dkv, IDX_HEADS * IDX_DIM, IDX_DIM, IDX_HEADS,
             d_model, d_model)
    starts = np.concatenate([[0], np.cumsum(sizes)[:-1]])
    return {n: int(s) for n, s in zip(names, starts)}


def _mixer_common(u, w_all, w_gates, cols, ktabs, itabs, *, hd, dq, dkv, tm, tn):
    q = _rope_proj(u, w_all, ktabs, first_col=cols["q"], n=dq, half=hd // ROT_DIV // 2,
                   scale=hd ** -0.5 * LOG2E, tm=tm, tn=tn)
    iq = _rope_proj(u, w_all, itabs[0], first_col=cols["iq"], n=IDX_HEADS * IDX_DIM,
                    half=IDX_DIM // ROT_DIV // 2, scale=IDX_DIM ** -0.5, tm=tm, tn=tn)
    assert cols["iw"] == cols["ik"] + IDX_DIM
    k, v, misc, k_bf, v_bf, ika, ikb = _kv_proj(
        u, w_all, ktabs, itabs[1], k_col=cols["k"], v_col=cols["v"], misc_col=cols["ik"],
        dkv=dkv, tm=tm)
    gc, ga = _gate_proj(u, w_gates, tm=tm, tn=tn)
    return q, iq, k, v, misc, k_bf, v_bf, ika, ikb, gc, ga


def kernel(x_prompt, x_sample, cache_k, cache_v, cache_idx_k, state_conv, page_table,
           norm_ffn1_pre, norm_ffn1_post, w_ffn1_gate_up, w_ffn1_down,
           norm_mix_pre, norm_mix_post, w_in, w_conv, w_conv_out, w_attn_out, w_out,
           norm_ffn2_pre, norm_ffn2_post, w_ffn2_gate_up, w_ffn2_down):
    bp, t, d = x_prompt.shape
    bs, ts, _ = x_sample.shape
    depth = w_in.shape[0]
    page, n_kv, hd = cache_k.shape[2:]
    n_pages = page_table.shape[1]
    past = n_pages * page
    d_conv = w_conv.shape[2]
    dq = w_attn_out.shape[1]
    dkv = n_kv * hd
    assert bp == 1 and ts == 1 and dq == N_HEADS * hd and n_kv == N_KV_HEADS

    tm = min(512, t)
    tf = 512
    tn = 1024
    tq = sc = min(256, t)
    cols = _in_proj_columns(d, d_conv, dq, dkv)
    topk_p = min(TOPK_MAX, t // 4)
    topk_s = min(TOPK_MAX, (past + ts) // 4)

    pos_p = np.arange(t)
    pos_s = np.full((bs,), past)
    ktabs_p, ktabs_s = _rope_tables(pos_p, hd), _rope_tables(pos_s, hd)
    itabs_p = (_rope_tables(pos_p, IDX_DIM), _rope_tables(pos_p, IDX_DIM, LANES - IDX_DIM))
    itabs_s = (_rope_tables(pos_s, IDX_DIM), _rope_tables(pos_s, IDX_DIM, LANES - IDX_DIM))

    hp = x_prompt.reshape(t, d)
    hs = x_sample.reshape(bs, d)
    outs = [[] for _ in range(8)]
    row = lambda a: a.reshape(1, -1)
    for l in range(depth):
        w1gu, w1d = w_ffn1_gate_up[l].astype(BF16), w_ffn1_down[l].astype(BF16)
        w2gu, w2d = w_ffn2_gate_up[l].astype(BF16), w_ffn2_down[l].astype(BF16)
        w_all, w_gates = _to_bf16(w_in[l], tn=tn), w_in[l][:, cols["gc"]:].astype(BF16)
        w_co, w_ao, w_o = (w_conv_out[l].astype(BF16), w_attn_out[l].astype(BF16),
                           w_out[l].astype(BF16))
        gain_mix = row(norm_mix_pre[l])

        hp = _ffn_half(hp, row(norm_ffn1_pre[l]), row(norm_ffn1_post[l]), w1gu, w1d, tm=tm, tf=tf)
        up = _rms_cast(hp, gain_mix, tm=tm)
        conv_y, tail = _conv_seq(up, w_all, w_conv[l], tm=tm, tn=tn)
        q, iq, k, v, misc, k_bf, v_bf, ika, ikb, gc, ga = _mixer_common(
            up, w_all, w_gates, cols, ktabs_p, itabs_p, hd=hd, dq=dq, dkv=dkv, tm=tm, tn=tn)
        wt = (misc[:, IDX_DIM:IDX_DIM + IDX_HEADS] * IDX_HEADS ** -0.5).T
        vt = v_bf.reshape(t // sc, sc, n_kv, hd).transpose(2, 0, 3, 1)
        vt = jnp.concatenate([vt, jnp.ones((n_kv, t // sc, 16, sc), BF16)], axis=2)
        attn_o = _prompt_attention(iq, wt, q, ika, ikb, k_bf, vt, tq=tq, sc=sc, topk=topk_p, hd=hd)
        hp = _merge(hp, conv_y, attn_o, gc, ga, row(norm_mix_post[l]), w_co, w_ao, w_o,
                    tm=tm, tn=tn // 2)
        outs[0].append(k.reshape(bp, t // page, page, n_kv, hd))
        outs[1].append(v.reshape(bp, t // page, page, n_kv, hd))
        outs[2].append(misc[:, :IDX_DIM].reshape(bp, t // page, page, IDX_DIM))
        outs[3].append(tail[tail.shape[0] - (CONV_WIDTH - 1):].reshape(bp, CONV_WIDTH - 1, d_conv))
        hp = _ffn_half(hp, row(norm_ffn2_pre[l]), row(norm_ffn2_post[l]), w2gu, w2d, tm=tm, tf=tf)

        hs = _ffn_half(hs, row(norm_ffn1_pre[l]), row(norm_ffn1_post[l]), w1gu, w1d, tm=bs, tf=tf)
        st = state_conv[l]
        us = _rms_cast(hs, gain_mix, tm=bs)
        conv_y, z = _conv_step(us, w_all, w_conv[l], st[:, 0, :], st[:, 1, :], tn=tn)
        q, iq, k, v, misc, k_bf, v_bf, ika, ikb, gc, ga = _mixer_common(
            us, w_all, w_gates, cols, ktabs_s, itabs_s, hd=hd, dq=dq, dkv=dkv, tm=bs, tn=tn)
        w_idx = (misc[:, IDX_DIM:IDX_DIM + IDX_HEADS] * IDX_HEADS ** -0.5).reshape(bs, IDX_HEADS, 1)
        scores = _sample_scores(page_table, iq.reshape(bs, IDX_HEADS, IDX_DIM), w_idx,
                                ika[:, :IDX_DIM].reshape(bs, IDX_DIM, 1),
                                cache_idx_k[l].transpose(0, 2, 1))
        idx = _sample_select(scores, topk=topk_s).reshape(bs, topk_s)
        attn_o = _sample_attention(page_table, idx, q.reshape(bs, N_HEADS, hd),
                                   k.reshape(bs, n_kv, hd), v.reshape(bs, n_kv, hd),
                                   cache_k[l], cache_v[l], topk=topk_s)
        hs = _merge(hs, conv_y, attn_o.reshape(bs, dq).astype(BF16), gc, ga, row(norm_mix_post[l]),
                    w_co, w_ao, w_o, tm=bs, tn=tn // 2)
        outs[4].append(k.reshape(bs, ts, n_kv, hd))
        outs[5].append(v.reshape(bs, ts, n_kv, hd))
        outs[6].append(misc[:, :IDX_DIM].reshape(bs, ts, IDX_DIM))
        outs[7].append(jnp.stack([st[:, 1, :], z], axis=1))
        hs = _ffn_half(hs, row(norm_ffn2_pre[l]), row(norm_ffn2_post[l]), w2gu, w2d, tm=bs, tf=tf)

    return (hp.reshape(bp, t, d), hs.reshape(bs, ts, d)) + tuple(jnp.stack(o) for o in outs)
```

```python
import functools

import numpy as np
import jax
import jax.numpy as jnp
from jax import lax
from jax.experimental import pallas as pl
from jax.experimental.pallas import tpu as pltpu

F32 = jnp.float32
BF16 = jnp.bfloat16
I32 = jnp.int32

N_HEADS = 16
N_KV_HEADS = 4
IDX_HEADS = 16
IDX_DIM = 64
TOPK_MAX = 256
CONV_WIDTH = 3
ROPE_THETA = 500000.0
ROT_DIV = 4
EPS = 1e-6

LANES = 128
VMEM_LIMIT = 56 * 1024 * 1024
LOG2E = 1.4426950408889634
NEG_BIAS = -1e30
INT_MIN = -2 ** 31
INT_MAX = 2 ** 31 - 1
NEG_INF_KEY = int(np.int32(np.uint32(0x807FFFFF)))


def _cparams(sem):
    return pltpu.CompilerParams(dimension_semantics=sem, vmem_limit_bytes=VMEM_LIMIT)


def _rms_scale(x):
    return lax.rsqrt(jnp.mean(x * x, axis=-1, keepdims=True) + EPS)


def _nt_dot(a, b):
    return lax.dot_general(a, b, (((1,), (1,)), ((), ())), preferred_element_type=F32)


def _dot(a, b):
    return jnp.dot(a, b, preferred_element_type=F32)


def _sigmoid(x):
    return 0.5 * jnp.tanh(0.5 * x) + 0.5


def _ffn_kernel(x_ref, pre_ref, post_ref, wg_ref, wu_ref, wd_ref, o_ref, xn_ref, acc_ref):
    j = pl.program_id(1)

    @pl.when(j == 0)
    def _():
        x = x_ref[...]
        xn_ref[...] = (x * _rms_scale(x) * pre_ref[...]).astype(BF16)
        acc_ref[...] = jnp.zeros_like(acc_ref)

    xn = xn_ref[...]
    g = _dot(xn, wg_ref[...])
    u = _dot(xn, wu_ref[...])
    h = (g * _sigmoid(g) * u).astype(BF16)
    acc_ref[...] += _dot(h, wd_ref[...])

    @pl.when(j == pl.num_programs(1) - 1)
    def _():
        y = acc_ref[...]
        o_ref[...] = x_ref[...] + 0.5 * (y * _rms_scale(y) * post_ref[...])


def _ffn_half(x, pre, post, w_gu, w_d, *, tm, tf):
    m, d = x.shape
    f = w_d.shape[0]
    nf = f // tf
    return pl.pallas_call(
        _ffn_kernel,
        grid=(m // tm, nf),
        in_specs=[
            pl.BlockSpec((tm, d), lambda i, j: (i, 0)),
            pl.BlockSpec((1, d), lambda i, j: (0, 0)),
            pl.BlockSpec((1, d), lambda i, j: (0, 0)),
            pl.BlockSpec((d, tf), lambda i, j: (0, j)),
            pl.BlockSpec((d, tf), lambda i, j: (0, nf + j)),
            pl.BlockSpec((tf, d), lambda i, j: (j, 0)),
        ],
        out_specs=pl.BlockSpec((tm, d), lambda i, j: (i, 0)),
        out_shape=jax.ShapeDtypeStruct((m, d), F32),
        scratch_shapes=[pltpu.VMEM((tm, d), BF16), pltpu.VMEM((tm, d), F32)],
        compiler_params=_cparams(("arbitrary", "arbitrary")),
        name="ffn_half",
    )(x, pre, post, w_gu, w_gu, w_d)


def _merge_kernel(h_ref, cy_ref, ao_ref, gc_ref, ga_ref, post_ref, wc_ref, wa_ref, wo_ref,
                  o_ref, acc_ref):
    j = pl.program_id(1)

    @pl.when(j == 0)
    def _():
        acc_ref[...] = jnp.zeros_like(acc_ref)

    mc = _dot(cy_ref[...], wc_ref[...])
    ma = _dot(ao_ref[...], wa_ref[...])
    mix = gc_ref[...].astype(F32) * mc + ga_ref[...].astype(F32) * ma
    acc_ref[...] += _dot(mix.astype(BF16), wo_ref[...])

    @pl.when(j == pl.num_programs(1) - 1)
    def _():
        y = acc_ref[...]
        o_ref[...] = h_ref[...] + y * _rms_scale(y) * post_ref[...]


def _merge(h, conv_y, attn_o, gc, ga, post, w_co, w_ao, w_out, *, tm, tn):
    m, d = h.shape
    return pl.pallas_call(
        _merge_kernel,
        grid=(m // tm, d // tn),
        in_specs=[
            pl.BlockSpec((tm, d), lambda i, j: (i, 0)),
            pl.BlockSpec((tm, d), lambda i, j: (i, 0)),
            pl.BlockSpec((tm, d), lambda i, j: (i, 0)),
            pl.BlockSpec((tm, tn), lambda i, j: (i, j)),
            pl.BlockSpec((tm, tn), lambda i, j: (i, j)),
            pl.BlockSpec((1, d), lambda i, j: (0, 0)),
            pl.BlockSpec((d, tn), lambda i, j: (0, j)),
            pl.BlockSpec((d, tn), lambda i, j: (0, j)),
            pl.BlockSpec((tn, d), lambda i, j: (j, 0)),
        ],
        out_specs=pl.BlockSpec((tm, d), lambda i, j: (i, 0)),
        out_shape=jax.ShapeDtypeStruct((m, d), F32),
        scratch_shapes=[pltpu.VMEM((tm, d), F32)],
        compiler_params=_cparams(("arbitrary", "arbitrary")),
        name="merge",
    )(h, conv_y, attn_o, gc, ga, post, w_co, w_ao, w_out)


def _rms_cast_kernel(h_ref, gain_ref, u_ref):
    x = h_ref[...]
    u_ref[...] = (x * _rms_scale(x) * gain_ref[...]).astype(BF16)


def _rms_cast(h, gain, *, tm):
    m, d = h.shape
    return pl.pallas_call(
        _rms_cast_kernel,
        grid=(m // tm,),
        in_specs=[pl.BlockSpec((tm, d), lambda i: (i, 0)), pl.BlockSpec((1, d), lambda i: (0, 0))],
        out_specs=pl.BlockSpec((tm, d), lambda i: (i, 0)),
        out_shape=jax.ShapeDtypeStruct((m, d), BF16),
        compiler_params=_cparams(("arbitrary",)),
        name="rms_cast",
    )(h, gain)


def _rope(x, a, b, c, half):
    return x * a + pltpu.roll(x, LANES - half, axis=1) * b + pltpu.roll(x, half, axis=1) * c


def _conv_seq_kernel(u_ref, wx_ref, wb_ref, wc_ref, wconv_ref, cy_ref, tail_ref, carry_ref):
    i = pl.program_id(0)
    j = pl.program_id(1)
    xn = u_ref[...]
    z = _dot(xn, wc_ref[...]) * _dot(xn, wx_ref[...])
    b = _dot(xn, wb_ref[...])
    tm = z.shape[0]

    @pl.when(i == 0)
    def _():
        carry_ref[j] = jnp.zeros(carry_ref.shape[1:], F32)

    prev = carry_ref[j]
    row = lax.broadcasted_iota(I32, z.shape, 0)
    z1 = jnp.where(row == 0, prev[1:2, :], pltpu.roll(z, 1, axis=0))
    z2 = jnp.where(row == 0, prev[0:1, :],
                   jnp.where(row == 1, prev[1:2, :], pltpu.roll(z, 2, axis=0)))
    w = wconv_ref[...]
    cy_ref[...] = (b * (z2 * w[0:1, :] + z1 * w[1:2, :] + z * w[2:3, :])).astype(BF16)
    tail = z[tm - 8:, :]
    carry_ref[j] = pltpu.roll(tail, 2, axis=0)
    tail_ref[...] = tail


def _col_blocks(d, tn, first_col):
    assert first_col % tn == 0
    return pl.BlockSpec((d, tn), lambda i, j: (0, first_col // tn + j))


def _conv_seq(u, w_all, w_conv, *, tm, tn):
    m, d = u.shape
    dc = w_conv.shape[1]
    return pl.pallas_call(
        _conv_seq_kernel,
        grid=(m // tm, dc // tn),
        in_specs=[
            pl.BlockSpec((tm, d), lambda i, j: (i, 0)),
            _col_blocks(d, tn, 0), _col_blocks(d, tn, dc), _col_blocks(d, tn, 2 * dc),
            pl.BlockSpec((CONV_WIDTH, tn), lambda i, j: (0, j)),
        ],
        out_specs=[
            pl.BlockSpec((tm, tn), lambda i, j: (i, j)),
            pl.BlockSpec((8, tn), lambda i, j: (i, j)),
        ],
        out_shape=[jax.ShapeDtypeStruct((m, dc), BF16), jax.ShapeDtypeStruct((m // tm * 8, dc), F32)],
        scratch_shapes=[pltpu.VMEM((dc // tn, 8, tn), F32)],
        compiler_params=_cparams(("arbitrary", "arbitrary")),
        name="conv_seq",
    )(u, w_all, w_all, w_all, w_conv)


def _conv_step_kernel(u_ref, wx_ref, wb_ref, wc_ref, wconv_ref, s0_ref, s1_ref, cy_ref, z_ref):
    xn = u_ref[...]
    z = _dot(xn, wc_ref[...]) * _dot(xn, wx_ref[...])
    b = _dot(xn, wb_ref[...])
    w = wconv_ref[...]
    cy_ref[...] = (b * (s0_ref[...] * w[0:1, :] + s1_ref[...] * w[1:2, :] + z * w[2:3, :])
                   ).astype(BF16)
    z_ref[...] = z


def _conv_step(u, w_all, w_conv, s0, s1, *, tn):
    m, d = u.shape
    dc = w_conv.shape[1]
    cspec = pl.BlockSpec((m, tn), lambda i, j: (0, j))
    return pl.pallas_call(
        _conv_step_kernel,
        grid=(1, dc // tn),
        in_specs=[
            pl.BlockSpec((m, d), lambda i, j: (0, 0)),
            _col_blocks(d, tn, 0), _col_blocks(d, tn, dc), _col_blocks(d, tn, 2 * dc),
            pl.BlockSpec((CONV_WIDTH, tn), lambda i, j: (0, j)),
            cspec, cspec,
        ],
        out_specs=[cspec, cspec],
        out_shape=[jax.ShapeDtypeStruct((m, dc), BF16), jax.ShapeDtypeStruct((m, dc), F32)],
        compiler_params=_cparams(("arbitrary", "arbitrary")),
        name="conv_step",
    )(u, w_all, w_all, w_all, w_conv, s0, s1)


def _rope_proj_kernel(u_ref, w_ref, a_ref, b_ref, c_ref, o_ref, *, half, scale):
    y = _dot(u_ref[...], w_ref[...])
    a, b, c = a_ref[...], b_ref[...], c_ref[...]
    for g in range(y.shape[1] // LANES):
        sl = slice(g * LANES, (g + 1) * LANES)
        o_ref[:, sl] = (_rope(y[:, sl], a, b, c, half) * scale).astype(o_ref.dtype)


def _rope_proj(u, w_all, tabs, *, first_col, n, half, scale, tm, tn):
    m, d = u.shape
    tspec = pl.BlockSpec((tm, LANES), lambda i, j: (i, 0))
    return pl.pallas_call(
        functools.partial(_rope_proj_kernel, half=half, scale=scale),
        grid=(m // tm, n // tn),
        in_specs=[
            pl.BlockSpec((tm, d), lambda i, j: (i, 0)),
            _col_blocks(d, tn, first_col),
            tspec, tspec, tspec,
        ],
        out_specs=pl.BlockSpec((tm, tn), lambda i, j: (i, j)),
        out_shape=jax.ShapeDtypeStruct((m, n), BF16),
        compiler_params=_cparams(("arbitrary", "arbitrary")),
        name="rope_proj",
    )(u, w_all, *tabs)


def _kv_proj_kernel(u_ref, wk_ref, wv_ref, wm_ref, ka_ref, kb_ref, kc_ref,
                    ia_ref, ib_ref, ic_ref,
                    k_ref, v_ref, misc_ref, kbf_ref, vbf_ref, ika_ref, ikb_ref, *, dkv):
    xn = u_ref[...]
    yk = _dot(xn, wk_ref[...])
    a, b, c = ka_ref[...], kb_ref[...], kc_ref[...]
    for g in range(dkv // LANES):
        sl = slice(g * LANES, (g + 1) * LANES)
        kg = _rope(yk[:, sl], a, b, c, LANES // ROT_DIV // 2)
        k_ref[:, sl] = kg
        kbf_ref[:, sl] = kg.astype(BF16)
    v = _dot(xn, wv_ref[...])
    v_ref[...] = v
    vbf_ref[...] = v.astype(BF16)
    misc = _rope(_dot(xn, wm_ref[...]), ia_ref[...], ib_ref[...], ic_ref[...],
                 IDX_DIM // ROT_DIV // 2)
    misc_ref[...] = misc
    lane = lax.broadcasted_iota(I32, misc.shape, 1)
    ik_lo = jnp.where(lane < IDX_DIM, misc, 0.0)
    ika_ref[...] = ik_lo.astype(BF16)
    ikb_ref[...] = pltpu.roll(ik_lo, IDX_DIM, axis=1).astype(BF16)


def _kv_proj(u, w_all, ktabs, itabs, *, k_col, v_col, misc_col, dkv, tm):
    m, d = u.shape
    assert k_col % dkv == 0 and v_col % dkv == 0 and misc_col % LANES == 0
    tspec = pl.BlockSpec((tm, LANES), lambda i: (i, 0))
    kvspec = pl.BlockSpec((tm, dkv), lambda i: (i, 0))
    mspec = pl.BlockSpec((tm, LANES), lambda i: (i, 0))
    return pl.pallas_call(
        functools.partial(_kv_proj_kernel, dkv=dkv),
        grid=(m // tm,),
        in_specs=[
            pl.BlockSpec((tm, d), lambda i: (i, 0)),
            pl.BlockSpec((d, dkv), lambda i: (0, k_col // dkv)),
            pl.BlockSpec((d, dkv), lambda i: (0, v_col // dkv)),
            pl.BlockSpec((d, LANES), lambda i: (0, misc_col // LANES)),
            tspec, tspec, tspec, tspec, tspec, tspec,
        ],
        out_specs=[kvspec, kvspec, mspec, kvspec, kvspec, mspec, mspec],
        out_shape=[
            jax.ShapeDtypeStruct((m, dkv), F32), jax.ShapeDtypeStruct((m, dkv), F32),
            jax.ShapeDtypeStruct((m, LANES), F32),
            jax.ShapeDtypeStruct((m, dkv), BF16), jax.ShapeDtypeStruct((m, dkv), BF16),
            jax.ShapeDtypeStruct((m, LANES), BF16), jax.ShapeDtypeStruct((m, LANES), BF16),
        ],
        compiler_params=_cparams(("arbitrary",)),
        name="kv_proj",
    )(u, w_all, w_all, w_all, *ktabs, *itabs)


def _gate_proj_kernel(u_ref, wc_ref, wa_ref, gc_ref, ga_ref):
    xn = u_ref[...]
    gc_ref[...] = _sigmoid(_dot(xn, wc_ref[...])).astype(BF16)
    ga_ref[...] = _sigmoid(_dot(xn, wa_ref[...])).astype(BF16)


def _gate_proj(u, w_gates, *, tm, tn):
    m, d = u.shape
    n = w_gates.shape[1] // 2
    nb = n // tn
    ospec = pl.BlockSpec((tm, tn), lambda i, j: (i, j))
    return pl.pallas_call(
        _gate_proj_kernel,
        grid=(m // tm, nb),
        in_specs=[
            pl.BlockSpec((tm, d), lambda i, j: (i, 0)),
            pl.BlockSpec((d, tn), lambda i, j: (0, j)),
            pl.BlockSpec((d, tn), lambda i, j: (0, nb + j)),
        ],
        out_specs=[ospec, ospec],
        out_shape=[jax.ShapeDtypeStruct((m, n), BF16), jax.ShapeDtypeStruct((m, n), BF16)],
        compiler_params=_cparams(("arbitrary", "arbitrary")),
        name="gate_proj",
    )(u, w_gates, w_gates)


def _ordered_key(score):
    bits = pltpu.bitcast(score, I32)
    return bits ^ ((bits >> 31) & INT_MAX)


def _kth_largest(count_ge, k, like):
    def body(b, lo):
        cand = lo + jnp.left_shift(jnp.int32(1), 31 - b)
        return jnp.where(count_ge(cand) >= k, cand, lo)
    return lax.fori_loop(0, 32, body, jnp.full_like(like, INT_MIN))


def _tie_limit(count_eq_before, need, nbits, like):
    def body(b, lo):
        cand = lo + jnp.left_shift(jnp.int32(1), nbits - 1 - b)
        return jnp.where(count_eq_before(cand) < need, cand, lo)
    return lax.fori_loop(0, nbits, body, jnp.zeros_like(like))


def _prompt_attn_kernel(iq_ref, wt_ref, q_ref, ika_ref, ikb_ref, k_ref, vt_ref, o_ref,
                        key_ref, half_ref, lim_ref, q4_ref, m_ref, acc_ref, lga_ref, lgb_ref,
                        *, tq, sc, topk, n_rep, hd, hb):
    i = pl.program_id(0)
    t0 = i * tq
    n_chunks = (t0 + tq) // sc
    n_pairs = iq_ref.shape[1] // LANES
    n_kv = k_ref.shape[1] // hd
    pos_bits = int(k_ref.shape[0] - 1).bit_length()
    i16_min = -2 ** 15

    def rows(c):
        return pl.ds(pl.multiple_of(c * sc, sc), sc)

    def score_chunk(c, carry):
        ka = ika_ref[rows(c), :]
        kb = ikb_ref[rows(c), :]
        acc = jnp.zeros((sc, tq), F32)
        for p in range(n_pairs):
            iq_p = iq_ref[:, p * LANES:(p + 1) * LANES]
            acc += jnp.maximum(_nt_dot(ka, iq_p), 0.0) * wt_ref[2 * p:2 * p + 1, :]
            acc += jnp.maximum(_nt_dot(kb, iq_p), 0.0) * wt_ref[2 * p + 1:2 * p + 2, :]
        spos = c * sc + lax.broadcasted_iota(I32, (sc, tq), 0)
        tpos = t0 + lax.broadcasted_iota(I32, (sc, tq), 1)
        key = _ordered_key(jnp.where(spos <= tpos, acc, -jnp.inf))
        key_ref[rows(c), :] = key
        half_ref[rows(c), :] = (key >> 16).astype(jnp.int16)
        return carry

    lax.fori_loop(0, n_chunks, score_chunk, 0)

    def half_count(cand, strict=False):
        cand16 = cand.astype(jnp.int16)

        def hits(c):
            blk = half_ref[rows(c), :]
            hit = jnp.where(blk > cand16 if strict else blk >= cand16,
                            jnp.bfloat16(1), jnp.bfloat16(0))
            part = hit[0:16]
            for r in range(1, sc // 16):
                part = part + hit[r * 16:(r + 1) * 16]
            return part

        def body2(c2, cnt):
            return cnt + (hits(2 * c2) + hits(2 * c2 + 1)).astype(F32)

        def body1(c, cnt):
            return cnt + hits(c).astype(F32)

        n2 = n_chunks // 2
        cnt = lax.fori_loop(0, n2, body2, jnp.zeros((16, tq), F32))
        cnt = lax.fori_loop(2 * n2, n_chunks, body1, cnt)
        return cnt.sum(axis=0, keepdims=True).astype(I32)

    def half_search(k):
        def body(b, carry):
            lo, cnt_lo = carry
            cand = lo + jnp.left_shift(jnp.int32(1), 15 - b)
            cnt = half_count(cand)
            ok = cnt >= k
            return jnp.where(ok, cand, lo), jnp.where(ok, cnt, cnt_lo)
        init = (jnp.full((1, tq), i16_min, I32), jnp.full((1, tq), n_chunks * sc, I32))
        return lax.fori_loop(0, 16, body, init)

    thr_hi, _ = half_search(topk)
    n_gt_hi = half_count(thr_hi, strict=True)

    def low_chunk(c, carry):
        key = key_ref[rows(c), :]
        low = (key & 0xFFFF) - 2 ** 15
        half_ref[rows(c), :] = jnp.where((key >> 16) == thr_hi, low, i16_min).astype(jnp.int16)
        return carry

    lax.fori_loop(0, n_chunks, low_chunk, 0)
    thr_lo, n_ge_lo = half_search(topk - n_gt_hi)
    thr = thr_hi * 65536 + (thr_lo + 2 ** 15)
    n_ge = n_gt_hi + n_ge_lo
    has_tie = jnp.max(jnp.where((n_ge > topk) & (thr > NEG_INF_KEY), 1.0, 0.0)) > 0.0
    lim_ref[...] = jnp.full(lim_ref.shape, INT_MAX, I32)

    @pl.when(has_tie)
    def _():
        def column_count(pred):
            def body(c, cnt):
                blk = key_ref[rows(c), :]
                spos = c * sc + lax.broadcasted_iota(I32, (sc, tq), 0)
                hit = jnp.where(pred(blk, spos), 1, 0).astype(I32)
                return cnt + hit.reshape(sc // 8, 8, tq).sum(axis=0)
            cnt = lax.fori_loop(0, n_chunks, body, jnp.zeros((8, tq), I32))
            return cnt.astype(F32).sum(axis=0, keepdims=True).astype(I32)

        need = topk - column_count(lambda blk, spos: blk > thr)
        lim = _tie_limit(
            lambda p: column_count(lambda blk, spos: (blk == thr) & (spos < p)),
            need, pos_bits + 1, jnp.zeros((1, tq), I32))
        lim_ref[...] = jnp.broadcast_to(lim, lim_ref.shape)

    lim = lim_ref[0:1, :]

    def bias_chunk(c, carry):
        blk = key_ref[rows(c), :]
        spos = c * sc + lax.broadcasted_iota(I32, (sc, tq), 0)
        sel = ((blk > thr) | ((blk == thr) & (spos <= lim))) & (blk > NEG_INF_KEY)
        key_ref[rows(c), :] = pltpu.bitcast(jnp.where(sel, 0.0, NEG_BIAS).astype(F32), I32)
        return carry

    lax.fori_loop(0, n_chunks, bias_chunk, 0)

    n_heads = n_kv * n_rep
    n_grp = n_heads // hb
    for j in range(n_grp):
        for r in range(hb):
            head = j * hb + r
            q4_ref[j, r * tq:(r + 1) * tq, :] = q_ref[:, head * hd:(head + 1) * hd]
    m_ref[...] = jnp.full(m_ref.shape, NEG_BIAS, F32)
    acc_ref[...] = jnp.zeros(acc_ref.shape, F32)

    def bias_of(c):
        bias = pltpu.bitcast(key_ref[rows(c), :], F32)
        return jnp.concatenate([bias] * hb, axis=1)

    def qk_store(c, j, bias_w, dst_ref):
        g = j * hb // n_rep
        dst_ref[...] = _nt_dot(k_ref[rows(c), g * hd:(g + 1) * hd], q4_ref[j]) + bias_w

    bufs = (lga_ref, lgb_ref)
    qk_store(0, 0, bias_of(0), bufs[0])

    def attn_chunk(c, carry):
        bias_w = bias_of(c)
        c_next = jnp.minimum(c + 1, n_chunks - 1)
        for j in range(n_grp):
            cur, nxt = bufs[j % 2], bufs[(j + 1) % 2]
            if j + 1 < n_grp:
                qk_store(c, j + 1, bias_w, nxt)
            else:
                qk_store(c_next, 0, bias_of(c_next), nxt)
            logit = cur[...]
            m_old = m_ref[j]
            m_new = jnp.maximum(m_old, jnp.max(logit, axis=0, keepdims=True))
            p = jnp.exp2(logit - m_new).astype(BF16)
            acc_ref[j] = (jnp.exp2(m_old - m_new) * acc_ref[j]
                          + _dot(vt_ref[j * hb // n_rep, c], p))
            m_ref[j] = m_new
        return carry

    lax.fori_loop(0, n_chunks, attn_chunk, 0)
    for j in range(n_grp):
        for r in range(hb):
            head = j * hb + r
            cols = slice(r * tq, (r + 1) * tq)
            out_t = acc_ref[j, 0:hd, cols] / acc_ref[j, hd:hd + 1, cols]
            o_ref[:, head * hd:(head + 1) * hd] = out_t.T.astype(o_ref.dtype)


def _prompt_attention(iq, wt, q, ika, ikb, k_bf, vt, *, tq, sc, topk, hd):
    t, dq = q.shape
    n_kv = k_bf.shape[1] // hd
    n_rep = dq // hd // n_kv
    assert tq >= topk and tq % sc == 0 and t % tq == 0
    hb = 4
    assert (n_kv * n_rep // hb) % 2 == 0
    n_grp = n_kv * n_rep // hb
    resident = functools.partial(pl.BlockSpec, pipeline_mode=pl.Buffered(1))
    return pl.pallas_call(
        functools.partial(_prompt_attn_kernel, tq=tq, sc=sc, topk=topk, n_rep=n_rep, hd=hd, hb=hb),
        grid=(t // tq,),
        in_specs=[
            pl.BlockSpec((tq, iq.shape[1]), lambda i: (i, 0)),
            pl.BlockSpec((wt.shape[0], tq), lambda i: (0, i)),
            pl.BlockSpec((tq, dq), lambda i: (i, 0)),
            resident(ika.shape, lambda i: (0, 0)),
            resident(ikb.shape, lambda i: (0, 0)),
            resident(k_bf.shape, lambda i: (0, 0)),
            resident(vt.shape, lambda i: (0, 0, 0, 0)),
        ],
        out_specs=pl.BlockSpec((tq, dq), lambda i: (i, 0)),
        out_shape=jax.ShapeDtypeStruct((t, dq), BF16),
        scratch_shapes=[
            pltpu.VMEM((t, tq), I32),
            pltpu.VMEM((t, tq), jnp.int16),
            pltpu.VMEM((8, tq), I32),
            pltpu.VMEM((n_grp, hb * tq, hd), BF16),
            pltpu.VMEM((n_grp, 1, hb * tq), F32),
            pltpu.VMEM((n_grp, vt.shape[2], hb * tq), F32),
            pltpu.VMEM((sc, hb * tq), F32),
            pltpu.VMEM((sc, hb * tq), F32),
        ],
        compiler_params=_cparams(("arbitrary",)),
        name="prompt_attention",
    )(iq, wt, q, ika, ikb, k_bf, vt)


def _sample_score_kernel(pt_ref, iq_ref, w_ref, ikn_ref, cache_ref, o_ref, buf_ref, sem_ref,
                         *, n_pages, page):
    b = pl.program_id(0)
    nb = pl.num_programs(0)

    def page_copy(bb, slot, p):
        return pltpu.make_async_copy(cache_ref.at[pt_ref[bb, p]], buf_ref.at[slot, p],
                                     sem_ref.at[slot])

    def start_all(bb, slot):
        def body(p, carry):
            page_copy(bb, slot, p).start()
            return carry
        lax.fori_loop(0, n_pages, body, 0)

    slot = b % 2

    @pl.when(b == 0)
    def _():
        start_all(b, slot)

    @pl.when(b + 1 < nb)
    def _():
        start_all(b + 1, 1 - slot)

    pltpu.make_async_copy(cache_ref.at[pl.ds(0, n_pages)], buf_ref.at[slot], sem_ref.at[slot]).wait()

    iq = iq_ref[0]
    w = w_ref[0]

    def head_sum(keys_t):
        s = jnp.maximum(_dot(iq, keys_t), 0.0) * w
        return jnp.sum(s, axis=0, keepdims=True)

    group = min(32, n_pages)

    def group_body(gi, carry):
        p0 = pl.multiple_of(gi * group, group)
        blk = buf_ref[slot, pl.ds(p0, group)]
        keys_t = jnp.concatenate([blk[r] for r in range(group)], axis=1).astype(BF16)
        s = head_sum(keys_t)
        for r in range(group):
            o_ref[0, pl.ds(p0 + r, 1), :] = s[:, r * page:(r + 1) * page]
        return carry
    lax.fori_loop(0, n_pages // group, group_body, 0)
    own = head_sum(jnp.broadcast_to(ikn_ref[0], (ikn_ref.shape[1], page)))
    lane = lax.broadcasted_iota(I32, (1, page), 1)
    o_ref[0, pl.ds(n_pages, 1), :] = jnp.where(lane == 0, own, -jnp.inf)
    o_ref[0, pl.ds(n_pages + 1, 7), :] = jnp.full((7, page), -jnp.inf, F32)


def _sample_scores(page_table, iq, w, ik_new, cache_ik_t):
    bs, n_pages = page_table.shape
    idim, page = cache_ik_t.shape[1:]
    rows = n_pages + 8
    grid_spec = pltpu.PrefetchScalarGridSpec(
        num_scalar_prefetch=1,
        grid=(bs,),
        in_specs=[
            pl.BlockSpec((1,) + iq.shape[1:], lambda b, pt: (b, 0, 0)),
            pl.BlockSpec((1,) + w.shape[1:], lambda b, pt: (b, 0, 0)),
            pl.BlockSpec((1,) + ik_new.shape[1:], lambda b, pt: (b, 0, 0)),
            pl.BlockSpec(memory_space=pl.ANY),
        ],
        out_specs=pl.BlockSpec((1, rows, page), lambda b, pt: (b, 0, 0)),
        scratch_shapes=[pltpu.VMEM((2, n_pages, idim, page), F32),
                        pltpu.SemaphoreType.DMA((2,))],
    )
    return pl.pallas_call(
        functools.partial(_sample_score_kernel, n_pages=n_pages, page=page),
        grid_spec=grid_spec,
        out_shape=jax.ShapeDtypeStruct((bs, rows, page), F32),
        compiler_params=_cparams(("arbitrary",)),
        name="sample_scores",
    )(page_table, iq, w, ik_new, cache_ik_t)


def _slab_pos(shape):
    nd = len(shape)
    return (lax.broadcasted_iota(I32, shape, nd - 2) * shape[-1]
            + lax.broadcasted_iota(I32, shape, nd - 1))


def _sample_thresh_kernel(s_ref, thr_ref, lim_ref, *, topk):
    key = _ordered_key(s_ref[...])
    bs, rows, page = key.shape
    pos = _slab_pos(key.shape)

    def count(pred):
        c = jnp.sum(jnp.where(pred, 1.0, 0.0), axis=1, keepdims=True)
        return jnp.sum(c, axis=2, keepdims=True).astype(I32)

    like = jnp.zeros((bs, 1, 1), I32)
    thr = _kth_largest(lambda v: count(key >= v), topk, like)
    need = topk - count(key > thr)
    lim = _tie_limit(lambda p: count((key == thr) & (pos < p)), need,
                     int(rows * page - 1).bit_length() + 1, like)
    thr_ref[...] = thr
    lim_ref[...] = lim


def _sample_compact_kernel(s_ref, thr_ref, lim_ref, idx_ref, *, topk):
    key = _ordered_key(s_ref[0])
    rows, page = key.shape
    kpad = 256
    thr, lim = thr_ref[0], lim_ref[0]
    sel = ((key > thr) | ((key == thr) & (_slab_pos(key.shape) <= lim))) & (key > NEG_INF_KEY)
    self32 = jnp.where(sel, 1.0, 0.0)
    zpad = jnp.zeros((kpad - rows, page), F32)

    def ones_where(pred):
        return jnp.where(pred, 1.0, 0.0).astype(BF16)

    upper = lax.broadcasted_iota(I32, (page, page), 0) < lax.broadcasted_iota(I32, (page, page), 1)
    within = _dot(self32.astype(BF16), ones_where(upper))
    tot = jnp.broadcast_to(jnp.sum(self32, axis=1, keepdims=True), (rows, page))
    tot = jnp.concatenate([tot, zpad], axis=0)
    earlier = lax.broadcasted_iota(I32, (kpad, kpad), 1) < lax.broadcasted_iota(I32, (kpad, kpad), 0)
    before = _dot(ones_where(earlier), tot.astype(BF16))

    before_row = before.T[0:1, :]
    ends_row = (before + tot).T[0:1, :]
    slot_r = lax.broadcasted_iota(I32, (topk, kpad), 0).astype(F32)
    owner = ones_where((before_row <= slot_r) & (slot_r < ends_row))
    code = jnp.concatenate([jnp.where(sel, within, -1.0), zpad], axis=0)
    row_id = lax.broadcasted_iota(I32, (kpad, page), 0).astype(F32)
    fetched = _dot(owner, jnp.concatenate([code, before, row_id], axis=1).astype(BF16))
    code_g, before_g, row_g = (fetched[:, 0:page], fetched[:, page:2 * page],
                               fetched[:, 2 * page:3 * page])
    slot = lax.broadcasted_iota(I32, (topk, page), 0).astype(F32)
    lane = lax.broadcasted_iota(I32, (topk, page), 1).astype(F32)
    total = (before + tot)[kpad - 1:kpad, :]
    hit = (code_g == slot - before_g) & (slot < total)
    lane_sel = jnp.max(jnp.where(hit, lane, -1.0), axis=1, keepdims=True)
    pos = jnp.where(lane_sel >= 0.0, row_g[:, 0:1] * page + lane_sel, -1.0)
    idx_ref[0] = pos.astype(I32)


def _sample_select(scores, *, topk):
    bs, rows, page = scores.shape
    assert rows % 8 == 0 and rows <= 256
    one = pl.BlockSpec((1, 1, 1), lambda b: (b, 0, 0))
    thr, lim = pl.pallas_call(
        functools.partial(_sample_thresh_kernel, topk=topk),
        out_shape=[jax.ShapeDtypeStruct((bs, 1, 1), I32)] * 2,
        compiler_params=pltpu.CompilerParams(vmem_limit_bytes=VMEM_LIMIT),
        name="sample_thresh",
    )(scores)
    return pl.pallas_call(
        functools.partial(_sample_compact_kernel, topk=topk),
        grid=(bs,),
        in_specs=[pl.BlockSpec((1, rows, page), lambda b: (b, 0, 0)), one, one],
        out_specs=pl.BlockSpec((1, topk, 1), lambda b: (b, 0, 0)),
        out_shape=jax.ShapeDtypeStruct((bs, topk, 1), I32),
        compiler_params=_cparams(("arbitrary",)),
        name="sample_compact",
    )(scores, thr, lim)


def _sample_attn_kernel(pt_ref, idx_ref, idxv_ref, q_ref, kn_ref, vn_ref, ck_ref, cv_ref, o_ref,
                        kbuf_ref, vbuf_ref, sem_ref, *, page, past, topk, n_rep):
    b = pl.program_id(0)
    nb = pl.num_programs(0)
    n_kv = kn_ref.shape[1]

    def row_copies(bb, slot, j):
        pos = jnp.clip(idx_ref[bb * topk + j], 0, past - 1)
        if page & (page - 1) == 0:
            page_no = lax.shift_right_logical(pos, page.bit_length() - 1)
            off = pos & (page - 1)
        else:
            page_no, off = lax.div(pos, page), lax.rem(pos, page)
        phys = pt_ref[bb * (past // page) + page_no]
        return (pltpu.make_async_copy(ck_ref.at[phys, off], kbuf_ref.at[slot, j], sem_ref.at[0, slot]),
                pltpu.make_async_copy(cv_ref.at[phys, off], vbuf_ref.at[slot, j], sem_ref.at[1, slot]))

    unroll = 8

    def start_all(bb, slot):
        def body(j8, carry):
            for u in range(unroll):
                ck, cv = row_copies(bb, slot, j8 * unroll + u)
                ck.start()
                cv.start()
            return carry
        lax.fori_loop(0, topk // unroll, body, 0)

    slot = b % 2

    @pl.when(b == 0)
    def _():
        start_all(b, slot)

    @pl.when(b + 1 < nb)
    def _():
        start_all(b + 1, 1 - slot)

    for h in range(topk // page):
        slab = pl.ds(h * page, page)
        pltpu.make_async_copy(ck_ref.at[0], kbuf_ref.at[slot, slab], sem_ref.at[0, slot]).wait()
        pltpu.make_async_copy(cv_ref.at[0], vbuf_ref.at[slot, slab], sem_ref.at[1, slot]).wait()

    pos = idxv_ref[0]
    bias = jnp.where((pos >= 0) & (pos < past), 0.0, NEG_BIAS)
    own_sel = jnp.max(jnp.where(pos == past, 1.0, 0.0), axis=1, keepdims=True)
    own_bias = jnp.where(own_sel > 0.0, 0.0, NEG_BIAS)
    for g in range(n_kv):
        rows = slice(g * n_rep, (g + 1) * n_rep)
        qg = q_ref[0, rows, :]
        kg = kbuf_ref[slot, :, g, :].astype(BF16)
        vg = vbuf_ref[slot, :, g, :].astype(BF16)
        kn = kn_ref[0, g:g + 1, :].astype(BF16).astype(F32)
        vn = vn_ref[0, g:g + 1, :].astype(BF16).astype(F32)
        logit = _nt_dot(qg, kg) + bias
        own = jnp.sum(qg.astype(F32) * kn, axis=1, keepdims=True) + own_bias
        m = jnp.maximum(jnp.max(logit, axis=1, keepdims=True), own)
        p = jnp.exp2(logit - m)
        p_own = jnp.exp2(own - m)
        denom = jnp.sum(p, axis=1, keepdims=True) + p_own
        num = _dot(p.astype(BF16), vg) + p_own.astype(BF16).astype(F32) * vn
        o_ref[0, rows, :] = num / denom


def _sample_attention(page_table, idx, q, k_new, v_new, cache_k, cache_v, *, topk):
    bs, n_pages = page_table.shape
    page, n_kv, hd = cache_k.shape[1:]
    n_heads = q.shape[1]
    past = n_pages * page
    assert topk % page == 0 and topk % 8 == 0
    grid_spec = pltpu.PrefetchScalarGridSpec(
        num_scalar_prefetch=2,
        grid=(bs,),
        in_specs=[
            pl.BlockSpec((1, 1, topk), lambda b, pt, ix: (b, 0, 0)),
            pl.BlockSpec((1, n_heads, hd), lambda b, pt, ix: (b, 0, 0)),
            pl.BlockSpec((1, n_kv, hd), lambda b, pt, ix: (b, 0, 0)),
            pl.BlockSpec((1, n_kv, hd), lambda b, pt, ix: (b, 0, 0)),
            pl.BlockSpec(memory_space=pl.ANY),
            pl.BlockSpec(memory_space=pl.ANY),
        ],
        out_specs=pl.BlockSpec((1, n_heads, hd), lambda b, pt, ix: (b, 0, 0)),
        scratch_shapes=[pltpu.VMEM((2, topk, n_kv, hd), F32),
                        pltpu.VMEM((2, topk, n_kv, hd), F32),
                        pltpu.SemaphoreType.DMA((2, 2))],
    )
    return pl.pallas_call(
        functools.partial(_sample_attn_kernel, page=page, past=past, topk=topk,
                          n_rep=n_heads // n_kv),
        grid_spec=grid_spec,
        out_shape=jax.ShapeDtypeStruct((bs, n_heads, hd), F32),
        compiler_params=_cparams(("arbitrary",)),
        name="sample_attention",
    )(page_table.reshape(-1), idx.reshape(-1), idx.reshape(bs, 1, topk), q, k_new, v_new,
      cache_k, cache_v)


def _rope_tables(pos, dim, pad_lanes=0):
    rot = dim // ROT_DIV
    half = rot // 2
    inv = ROPE_THETA ** (-np.arange(half, dtype=np.float64) / half)
    ang = np.asarray(pos, np.float64)[:, None] * inv[None, :]
    cos, sin = jnp.asarray(np.cos(ang), F32), jnp.asarray(np.sin(ang), F32)
    width = dim + pad_lanes
    a = jnp.pad(jnp.concatenate([cos, cos], axis=1), ((0, 0), (0, width - rot)), constant_values=1.0)
    b = jnp.pad(-sin, ((0, 0), (0, width - half)))
    c = jnp.pad(sin, ((0, 0), (half, width - rot)))
    reps = LANES // width
    return tuple(jnp.tile(t, (1, reps)) if reps > 1 else t for t in (a, b, c))


def _in_proj_columns(d_model, d_conv, dq, dkv):
    names = ("x", "b", "c", "q", "k", "v", "iq", "ik", "iw", "gc", "ga")
    sizes = (d_conv, d_conv, d_conv, dq, dkv, dkv, IDX_HEADS * IDX_DIM, IDX_DIM, IDX_HEADS,
             d_model, d_model)
    starts = np.concatenate([[0], np.cumsum(sizes)[:-1]])
    return {n: int(s) for n, s in zip(names, starts)}


def _mixer_common(u, w_all, w_gates, cols, ktabs, itabs, *, hd, dq, dkv, tm, tn):
    q = _rope_proj(u, w_all, ktabs, first_col=cols["q"], n=dq, half=hd // ROT_DIV // 2,
                   scale=hd ** -0.5 * LOG2E, tm=tm, tn=tn)
    iq = _rope_proj(u, w_all, itabs[0], first_col=cols["iq"], n=IDX_HEADS * IDX_DIM,
                    half=IDX_DIM // ROT_DIV // 2, scale=IDX_DIM ** -0.5, tm=tm, tn=tn)
    assert cols["iw"] == cols["ik"] + IDX_DIM
    k, v, misc, k_bf, v_bf, ika, ikb = _kv_proj(
        u, w_all, ktabs, itabs[1], k_col=cols["k"], v_col=cols["v"], misc_col=cols["ik"],
        dkv=dkv, tm=tm)
    gc, ga = _gate_proj(u, w_gates, tm=tm, tn=tn)
    return q, iq, k, v, misc, k_bf, v_bf, ika, ikb, gc, ga


def kernel(x_prompt, x_sample, cache_k, cache_v, cache_idx_k, state_conv, page_table,
           norm_ffn1_pre, norm_ffn1_post, w_ffn1_gate_up, w_ffn1_down,
           norm_mix_pre, norm_mix_post, w_in, w_conv, w_conv_out, w_attn_out, w_out,
           norm_ffn2_pre, norm_ffn2_post, w_ffn2_gate_up, w_ffn2_down):
    bp, t, d = x_prompt.shape
    bs, ts, _ = x_sample.shape
    depth = w_in.shape[0]
    page, n_kv, hd = cache_k.shape[2:]
    n_pages = page_table.shape[1]
    past = n_pages * page
    d_conv = w_conv.shape[2]
    dq = w_attn_out.shape[1]
    dkv = n_kv * hd
    assert bp == 1 and ts == 1 and dq == N_HEADS * hd and n_kv == N_KV_HEADS

    tm = min(512, t)
    tf = 512
    tn = 1024
    tq = sc = min(256, t)
    cols = _in_proj_columns(d, d_conv, dq, dkv)
    topk_p = min(TOPK_MAX, t // 4)
    topk_s = min(TOPK_MAX, (past + ts) // 4)

    pos_p = np.arange(t)
    pos_s = np.full((bs,), past)
    ktabs_p, ktabs_s = _rope_tables(pos_p, hd), _rope_tables(pos_s, hd)
    itabs_p = (_rope_tables(pos_p, IDX_DIM), _rope_tables(pos_p, IDX_DIM, LANES - IDX_DIM))
    itabs_s = (_rope_tables(pos_s, IDX_DIM), _rope_tables(pos_s, IDX_DIM, LANES - IDX_DIM))

    hp = x_prompt.reshape(t, d)
    hs = x_sample.reshape(bs, d)
    outs = [[] for _ in range(8)]
    row = lambda a: a.reshape(1, -1)
    for l in range(depth):
        w1gu, w1d = w_ffn1_gate_up[l].astype(BF16), w_ffn1_down[l].astype(BF16)
        w2gu, w2d = w_ffn2_gate_up[l].astype(BF16), w_ffn2_down[l].astype(BF16)
        w_all, w_gates = w_in[l].astype(BF16), w_in[l][:, cols["gc"]:].astype(BF16)
        w_co, w_ao, w_o = (w_conv_out[l].astype(BF16), w_attn_out[l].astype(BF16),
                           w_out[l].astype(BF16))
        gain_mix = row(norm_mix_pre[l])

        hp = _ffn_half(hp, row(norm_ffn1_pre[l]), row(norm_ffn1_post[l]), w1gu, w1d, tm=tm, tf=tf)
        up = _rms_cast(hp, gain_mix, tm=tm)
        conv_y, tail = _conv_seq(up, w_all, w_conv[l], tm=tm, tn=tn)
        q, iq, k, v, misc, k_bf, v_bf, ika, ikb, gc, ga = _mixer_common(
            up, w_all, w_gates, cols, ktabs_p, itabs_p, hd=hd, dq=dq, dkv=dkv, tm=tm, tn=tn)
        wt = (misc[:, IDX_DIM:IDX_DIM + IDX_HEADS] * IDX_HEADS ** -0.5).T
        vt = v_bf.reshape(t // sc, sc, n_kv, hd).transpose(2, 0, 3, 1)
        vt = jnp.concatenate([vt, jnp.ones((n_kv, t // sc, 16, sc), BF16)], axis=2)
        attn_o = _prompt_attention(iq, wt, q, ika, ikb, k_bf, vt, tq=tq, sc=sc, topk=topk_p, hd=hd)
        hp = _merge(hp, conv_y, attn_o, gc, ga, row(norm_mix_post[l]), w_co, w_ao, w_o,
                    tm=tm, tn=tn // 2)
        outs[0].append(k.reshape(bp, t // page, page, n_kv, hd))
        outs[1].append(v.reshape(bp, t // page, page, n_kv, hd))
        outs[2].append(misc[:, :IDX_DIM].reshape(bp, t // page, page, IDX_DIM))
        outs[3].append(tail[tail.shape[0] - (CONV_WIDTH - 1):].reshape(bp, CONV_WIDTH - 1, d_conv))
        hp = _ffn_half(hp, row(norm_ffn2_pre[l]), row(norm_ffn2_post[l]), w2gu, w2d, tm=tm, tf=tf)

        hs = _ffn_half(hs, row(norm_ffn1_pre[l]), row(norm_ffn1_post[l]), w1gu, w1d, tm=bs, tf=tf)
        st = state_conv[l]
        us = _rms_cast(hs, gain_mix, tm=bs)
        conv_y, z = _conv_step(us, w_all, w_conv[l], st[:, 0, :], st[:, 1, :], tn=tn)
        q, iq, k, v, misc, k_bf, v_bf, ika, ikb, gc, ga = _mixer_common(
            us, w_all, w_gates, cols, ktabs_s, itabs_s, hd=hd, dq=dq, dkv=dkv, tm=bs, tn=tn)
        w_idx = (misc[:, IDX_DIM:IDX_DIM + IDX_HEADS] * IDX_HEADS ** -0.5).reshape(bs, IDX_HEADS, 1)
        scores = _sample_scores(page_table, iq.reshape(bs, IDX_HEADS, IDX_DIM), w_idx,
                                ika[:, :IDX_DIM].reshape(bs, IDX_DIM, 1),
                                cache_idx_k[l].transpose(0, 2, 1))
        idx = _sample_select(scores, topk=topk_s).reshape(bs, topk_s)
        attn_o = _sample_attention(page_table, idx, q.reshape(bs, N_HEADS, hd),
                                   k.reshape(bs, n_kv, hd), v.reshape(bs, n_kv, hd),
                                   cache_k[l], cache_v[l], topk=topk_s)
        hs = _merge(hs, conv_y, attn_o.reshape(bs, dq).astype(BF16), gc, ga, row(norm_mix_post[l]),
                    w_co, w_ao, w_o, tm=bs, tn=tn // 2)
        outs[4].append(k.reshape(bs, ts, n_kv, hd))
        outs[5].append(v.reshape(bs, ts, n_kv, hd))
        outs[6].append(misc[:, :IDX_DIM].reshape(bs, ts, IDX_DIM))
        outs[7].append(jnp.stack([st[:, 1, :], z], axis=1))
        hs = _ffn_half(hs, row(norm_ffn2_pre[l]), row(norm_ffn2_post[l]), w2gu, w2d, tm=bs, tf=tf)

    return (hp.reshape(bp, t, d), hs.reshape(bs, ts, d)) + tuple(jnp.stack(o) for o in outs)
```

```python
import functools

import numpy as np
import jax
import jax.numpy as jnp
from jax import lax
from jax.experimental import pallas as pl
from jax.experimental.pallas import tpu as pltpu

F32 = jnp.float32
BF16 = jnp.bfloat16
I32 = jnp.int32

N_HEADS = 16
N_KV_HEADS = 4
IDX_HEADS = 16
IDX_DIM = 64
TOPK_MAX = 256
CONV_WIDTH = 3
ROPE_THETA = 500000.0
ROT_DIV = 4
EPS = 1e-6

LANES = 128
VMEM_LIMIT = 56 * 1024 * 1024
LOG2E = 1.4426950408889634
NEG_BIAS = -1e30
INT_MIN = -2 ** 31
INT_MAX = 2 ** 31 - 1
NEG_INF_KEY = int(np.int32(np.uint32(0x807FFFFF)))


def _cparams(sem):
    return pltpu.CompilerParams(dimension_semantics=sem, vmem_limit_bytes=VMEM_LIMIT)


def _rms_scale(x):
    return lax.rsqrt(jnp.mean(x * x, axis=-1, keepdims=True) + EPS)


def _nt_dot(a, b):
    return lax.dot_general(a, b, (((1,), (1,)), ((), ())), preferred_element_type=F32)


def _dot(a, b):
    return jnp.dot(a, b, preferred_element_type=F32)


def _sigmoid(x):
    return 0.5 * jnp.tanh(0.5 * x) + 0.5


def _ffn_kernel(x_ref, pre_ref, post_ref, wg_ref, wu_ref, wd_ref, o_ref, xn_ref, acc_ref):
    j = pl.program_id(1)

    @pl.when(j == 0)
    def _():
        x = x_ref[...]
        xn_ref[...] = (x * _rms_scale(x) * pre_ref[...]).astype(BF16)
        acc_ref[...] = jnp.zeros_like(acc_ref)

    xn = xn_ref[...]
    g = _dot(xn, wg_ref[...])
    u = _dot(xn, wu_ref[...])
    h = (g * _sigmoid(g) * u).astype(BF16)
    acc_ref[...] += _dot(h, wd_ref[...])

    @pl.when(j == pl.num_programs(1) - 1)
    def _():
        y = acc_ref[...]
        o_ref[...] = x_ref[...] + 0.5 * (y * _rms_scale(y) * post_ref[...])


def _ffn_half(x, pre, post, w_gu, w_d, *, tm, tf):
    m, d = x.shape
    f = w_d.shape[0]
    nf = f // tf
    return pl.pallas_call(
        _ffn_kernel,
        grid=(m // tm, nf),
        in_specs=[
            pl.BlockSpec((tm, d), lambda i, j: (i, 0)),
            pl.BlockSpec((1, d), lambda i, j: (0, 0)),
            pl.BlockSpec((1, d), lambda i, j: (0, 0)),
            pl.BlockSpec((d, tf), lambda i, j: (0, j)),
            pl.BlockSpec((d, tf), lambda i, j: (0, nf + j)),
            pl.BlockSpec((tf, d), lambda i, j: (j, 0)),
        ],
        out_specs=pl.BlockSpec((tm, d), lambda i, j: (i, 0)),
        out_shape=jax.ShapeDtypeStruct((m, d), F32),
        scratch_shapes=[pltpu.VMEM((tm, d), BF16), pltpu.VMEM((tm, d), F32)],
        compiler_params=_cparams(("arbitrary", "arbitrary")),
        name="ffn_half",
    )(x, pre, post, w_gu, w_gu, w_d)


def _merge_kernel(h_ref, cy_ref, ao_ref, gc_ref, ga_ref, post_ref, wc_ref, wa_ref, wo_ref,
                  o_ref, acc_ref):
    j = pl.program_id(1)

    @pl.when(j == 0)
    def _():
        acc_ref[...] = jnp.zeros_like(acc_ref)

    mc = _dot(cy_ref[...], wc_ref[...])
    ma = _dot(ao_ref[...], wa_ref[...])
    mix = gc_ref[...].astype(F32) * mc + ga_ref[...].astype(F32) * ma
    acc_ref[...] += _dot(mix.astype(BF16), wo_ref[...])

    @pl.when(j == pl.num_programs(1) - 1)
    def _():
        y = acc_ref[...]
        o_ref[...] = h_ref[...] + y * _rms_scale(y) * post_ref[...]


def _merge(h, conv_y, attn_o, gc, ga, post, w_co, w_ao, w_out, *, tm, tn):
    m, d = h.shape
    return pl.pallas_call(
        _merge_kernel,
        grid=(m // tm, d // tn),
        in_specs=[
            pl.BlockSpec((tm, d), lambda i, j: (i, 0)),
            pl.BlockSpec((tm, d), lambda i, j: (i, 0)),
            pl.BlockSpec((tm, d), lambda i, j: (i, 0)),
            pl.BlockSpec((tm, tn), lambda i, j: (i, j)),
            pl.BlockSpec((tm, tn), lambda i, j: (i, j)),
            pl.BlockSpec((1, d), lambda i, j: (0, 0)),
            pl.BlockSpec((d, tn), lambda i, j: (0, j)),
            pl.BlockSpec((d, tn), lambda i, j: (0, j)),
            pl.BlockSpec((tn, d), lambda i, j: (j, 0)),
        ],
        out_specs=pl.BlockSpec((tm, d), lambda i, j: (i, 0)),
        out_shape=jax.ShapeDtypeStruct((m, d), F32),
        scratch_shapes=[pltpu.VMEM((tm, d), F32)],
        compiler_params=_cparams(("arbitrary", "arbitrary")),
        name="merge",
    )(h, conv_y, attn_o, gc, ga, post, w_co, w_ao, w_out)


def _rms_cast_kernel(h_ref, gain_ref, u_ref):
    x = h_ref[...]
    u_ref[...] = (x * _rms_scale(x) * gain_ref[...]).astype(BF16)


def _rms_cast(h, gain, *, tm):
    m, d = h.shape
    return pl.pallas_call(
        _rms_cast_kernel,
        grid=(m // tm,),
        in_specs=[pl.BlockSpec((tm, d), lambda i: (i, 0)), pl.BlockSpec((1, d), lambda i: (0, 0))],
        out_specs=pl.BlockSpec((tm, d), lambda i: (i, 0)),
        out_shape=jax.ShapeDtypeStruct((m, d), BF16),
        compiler_params=_cparams(("arbitrary",)),
        name="rms_cast",
    )(h, gain)


def _rope(x, a, b, c, half):
    return x * a + pltpu.roll(x, LANES - half, axis=1) * b + pltpu.roll(x, half, axis=1) * c


def _conv_seq_kernel(u_ref, wx_ref, wb_ref, wc_ref, wconv_ref, cy_ref, tail_ref, carry_ref):
    i = pl.program_id(0)
    j = pl.program_id(1)
    xn = u_ref[...]
    z = _dot(xn, wc_ref[...]) * _dot(xn, wx_ref[...])
    b = _dot(xn, wb_ref[...])
    tm = z.shape[0]

    @pl.when(i == 0)
    def _():
        carry_ref[j] = jnp.zeros(carry_ref.shape[1:], F32)

    prev = carry_ref[j]
    row = lax.broadcasted_iota(I32, z.shape, 0)
    z1 = jnp.where(row == 0, prev[1:2, :], pltpu.roll(z, 1, axis=0))
    z2 = jnp.where(row == 0, prev[0:1, :],
                   jnp.where(row == 1, prev[1:2, :], pltpu.roll(z, 2, axis=0)))
    w = wconv_ref[...]
    cy_ref[...] = (b * (z2 * w[0:1, :] + z1 * w[1:2, :] + z * w[2:3, :])).astype(BF16)
    tail = z[tm - 8:, :]
    carry_ref[j] = pltpu.roll(tail, 2, axis=0)
    tail_ref[...] = tail


def _col_blocks(d, tn, first_col):
    assert first_col % tn == 0
    return pl.BlockSpec((d, tn), lambda i, j: (0, first_col // tn + j))


def _conv_seq(u, w_all, w_conv, *, tm, tn):
    m, d = u.shape
    dc = w_conv.shape[1]
    return pl.pallas_call(
        _conv_seq_kernel,
        grid=(m // tm, dc // tn),
        in_specs=[
            pl.BlockSpec((tm, d), lambda i, j: (i, 0)),
            _col_blocks(d, tn, 0), _col_blocks(d, tn, dc), _col_blocks(d, tn, 2 * dc),
            pl.BlockSpec((CONV_WIDTH, tn), lambda i, j: (0, j)),
        ],
        out_specs=[
            pl.BlockSpec((tm, tn), lambda i, j: (i, j)),
            pl.BlockSpec((8, tn), lambda i, j: (i, j)),
        ],
        out_shape=[jax.ShapeDtypeStruct((m, dc), BF16), jax.ShapeDtypeStruct((m // tm * 8, dc), F32)],
        scratch_shapes=[pltpu.VMEM((dc // tn, 8, tn), F32)],
        compiler_params=_cparams(("arbitrary", "arbitrary")),
        name="conv_seq",
    )(u, w_all, w_all, w_all, w_conv)


def _conv_step_kernel(u_ref, wx_ref, wb_ref, wc_ref, wconv_ref, s0_ref, s1_ref, cy_ref, z_ref):
    xn = u_ref[...]
    z = _dot(xn, wc_ref[...]) * _dot(xn, wx_ref[...])
    b = _dot(xn, wb_ref[...])
    w = wconv_ref[...]
    cy_ref[...] = (b * (s0_ref[...] * w[0:1, :] + s1_ref[...] * w[1:2, :] + z * w[2:3, :])
                   ).astype(BF16)
    z_ref[...] = z


def _conv_step(u, w_all, w_conv, s0, s1, *, tn):
    m, d = u.shape
    dc = w_conv.shape[1]
    cspec = pl.BlockSpec((m, tn), lambda i, j: (0, j))
    return pl.pallas_call(
        _conv_step_kernel,
        grid=(1, dc // tn),
        in_specs=[
            pl.BlockSpec((m, d), lambda i, j: (0, 0)),
            _col_blocks(d, tn, 0), _col_blocks(d, tn, dc), _col_blocks(d, tn, 2 * dc),
            pl.BlockSpec((CONV_WIDTH, tn), lambda i, j: (0, j)),
            cspec, cspec,
        ],
        out_specs=[cspec, cspec],
        out_shape=[jax.ShapeDtypeStruct((m, dc), BF16), jax.ShapeDtypeStruct((m, dc), F32)],
        compiler_params=_cparams(("arbitrary", "arbitrary")),
        name="conv_step",
    )(u, w_all, w_all, w_all, w_conv, s0, s1)


def _rope_proj_kernel(u_ref, w_ref, a_ref, b_ref, c_ref, o_ref, *, half, scale):
    y = _dot(u_ref[...], w_ref[...])
    a, b, c = a_ref[...], b_ref[...], c_ref[...]
    for g in range(y.shape[1] // LANES):
        sl = slice(g * LANES, (g + 1) * LANES)
        o_ref[:, sl] = (_rope(y[:, sl], a, b, c, half) * scale).astype(o_ref.dtype)


def _rope_proj(u, w_all, tabs, *, first_col, n, half, scale, tm, tn):
    m, d = u.shape
    tspec = pl.BlockSpec((tm, LANES), lambda i, j: (i, 0))
    return pl.pallas_call(
        functools.partial(_rope_proj_kernel, half=half, scale=scale),
        grid=(m // tm, n // tn),
        in_specs=[
            pl.BlockSpec((tm, d), lambda i, j: (i, 0)),
            _col_blocks(d, tn, first_col),
            tspec, tspec, tspec,
        ],
        out_specs=pl.BlockSpec((tm, tn), lambda i, j: (i, j)),
        out_shape=jax.ShapeDtypeStruct((m, n), BF16),
        compiler_params=_cparams(("arbitrary", "arbitrary")),
        name="rope_proj",
    )(u, w_all, *tabs)


def _kv_proj_kernel(u_ref, wk_ref, wv_ref, wm_ref, ka_ref, kb_ref, kc_ref,
                    ia_ref, ib_ref, ic_ref,
                    k_ref, v_ref, misc_ref, kbf_ref, vbf_ref, ika_ref, ikb_ref, *, dkv):
    xn = u_ref[...]
    yk = _dot(xn, wk_ref[...])
    a, b, c = ka_ref[...], kb_ref[...], kc_ref[...]
    for g in range(dkv // LANES):
        sl = slice(g * LANES, (g + 1) * LANES)
        kg = _rope(yk[:, sl], a, b, c, LANES // ROT_DIV // 2)
        k_ref[:, sl] = kg
        kbf_ref[:, sl] = kg.astype(BF16)
    v = _dot(xn, wv_ref[...])
    v_ref[...] = v
    vbf_ref[...] = v.astype(BF16)
    misc = _rope(_dot(xn, wm_ref[...]), ia_ref[...], ib_ref[...], ic_ref[...],
                 IDX_DIM // ROT_DIV // 2)
    misc_ref[...] = misc
    lane = lax.broadcasted_iota(I32, misc.shape, 1)
    ik_lo = jnp.where(lane < IDX_DIM, misc, 0.0)
    ika_ref[...] = ik_lo.astype(BF16)
    ikb_ref[...] = pltpu.roll(ik_lo, IDX_DIM, axis=1).astype(BF16)


def _kv_proj(u, w_all, ktabs, itabs, *, k_col, v_col, misc_col, dkv, tm):
    m, d = u.shape
    assert k_col % dkv == 0 and v_col % dkv == 0 and misc_col % LANES == 0
    tspec = pl.BlockSpec((tm, LANES), lambda i: (i, 0))
    kvspec = pl.BlockSpec((tm, dkv), lambda i: (i, 0))
    mspec = pl.BlockSpec((tm, LANES), lambda i: (i, 0))
    return pl.pallas_call(
        functools.partial(_kv_proj_kernel, dkv=dkv),
        grid=(m // tm,),
        in_specs=[
            pl.BlockSpec((tm, d), lambda i: (i, 0)),
            pl.BlockSpec((d, dkv), lambda i: (0, k_col // dkv)),
            pl.BlockSpec((d, dkv), lambda i: (0, v_col // dkv)),
            pl.BlockSpec((d, LANES), lambda i: (0, misc_col // LANES)),
            tspec, tspec, tspec, tspec, tspec, tspec,
        ],
        out_specs=[kvspec, kvspec, mspec, kvspec, kvspec, mspec, mspec],
        out_shape=[
            jax.ShapeDtypeStruct((m, dkv), F32), jax.ShapeDtypeStruct((m, dkv), F32),
            jax.ShapeDtypeStruct((m, LANES), F32),
            jax.ShapeDtypeStruct((m, dkv), BF16), jax.ShapeDtypeStruct((m, dkv), BF16),
            jax.ShapeDtypeStruct((m, LANES), BF16), jax.ShapeDtypeStruct((m, LANES), BF16),
        ],
        compiler_params=_cparams(("arbitrary",)),
        name="kv_proj",
    )(u, w_all, w_all, w_all, *ktabs, *itabs)


def _gate_proj_kernel(u_ref, wc_ref, wa_ref, gc_ref, ga_ref):
    xn = u_ref[...]
    gc_ref[...] = _sigmoid(_dot(xn, wc_ref[...])).astype(BF16)
    ga_ref[...] = _sigmoid(_dot(xn, wa_ref[...])).astype(BF16)


def _gate_proj(u, w_gates, *, tm, tn):
    m, d = u.shape
    n = w_gates.shape[1] // 2
    nb = n // tn
    ospec = pl.BlockSpec((tm, tn), lambda i, j: (i, j))
    return pl.pallas_call(
        _gate_proj_kernel,
        grid=(m // tm, nb),
        in_specs=[
            pl.BlockSpec((tm, d), lambda i, j: (i, 0)),
            pl.BlockSpec((d, tn), lambda i, j: (0, j)),
            pl.BlockSpec((d, tn), lambda i, j: (0, nb + j)),
        ],
        out_specs=[ospec, ospec],
        out_shape=[jax.ShapeDtypeStruct((m, n), BF16), jax.ShapeDtypeStruct((m, n), BF16)],
        compiler_params=_cparams(("arbitrary", "arbitrary")),
        name="gate_proj",
    )(u, w_gates, w_gates)


def _ordered_key(score):
    bits = pltpu.bitcast(score, I32)
    return bits ^ ((bits >> 31) & INT_MAX)


def _kth_largest(count_ge, k, like):
    def body(b, lo):
        cand = lo + jnp.left_shift(jnp.int32(1), 31 - b)
        return jnp.where(count_ge(cand) >= k, cand, lo)
    return lax.fori_loop(0, 32, body, jnp.full_like(like, INT_MIN))


def _tie_limit(count_eq_before, need, nbits, like):
    def body(b, lo):
        cand = lo + jnp.left_shift(jnp.int32(1), nbits - 1 - b)
        return jnp.where(count_eq_before(cand) < need, cand, lo)
    return lax.fori_loop(0, nbits, body, jnp.zeros_like(like))


def _prompt_attn_kernel(iq_ref, wt_ref, q_ref, ika_ref, ikb_ref, k_ref, vt_ref, o_ref,
                        key_ref, half_ref, lim_ref, q4_ref, m_ref, acc_ref, lga_ref, lgb_ref,
                        *, tq, sc, topk, n_rep, hd, hb):
    i = pl.program_id(0)
    t0 = i * tq
    n_chunks = (t0 + tq + sc - 1) // sc
    n_pairs = iq_ref.shape[1] // LANES
    n_kv = k_ref.shape[1] // hd
    pos_bits = int(k_ref.shape[0] - 1).bit_length()
    i16_min = -2 ** 15

    def rows(c):
        return pl.ds(pl.multiple_of(c * sc, sc), sc)

    def score_chunk(c, carry):
        ka = ika_ref[rows(c), :]
        kb = ikb_ref[rows(c), :]
        acc = jnp.zeros((sc, tq), F32)
        for p in range(n_pairs):
            iq_p = iq_ref[:, p * LANES:(p + 1) * LANES]
            acc += jnp.maximum(_nt_dot(ka, iq_p), 0.0) * wt_ref[2 * p:2 * p + 1, :]
            acc += jnp.maximum(_nt_dot(kb, iq_p), 0.0) * wt_ref[2 * p + 1:2 * p + 2, :]
        spos = c * sc + lax.broadcasted_iota(I32, (sc, tq), 0)
        tpos = t0 + lax.broadcasted_iota(I32, (sc, tq), 1)
        key = _ordered_key(jnp.where(spos <= tpos, acc, -jnp.inf))
        key_ref[rows(c), :] = key
        half_ref[rows(c), :] = (key >> 16).astype(jnp.int16)
        return carry

    lax.fori_loop(0, n_chunks, score_chunk, 0)

    def half_count(cand, strict=False):
        cand16 = cand.astype(jnp.int16)

        def hits(c):
            blk = half_ref[rows(c), :]
            hit = jnp.where(blk > cand16 if strict else blk >= cand16,
                            jnp.bfloat16(1), jnp.bfloat16(0))
            part = hit[0:16]
            for r in range(1, sc // 16):
                part = part + hit[r * 16:(r + 1) * 16]
            return part

        def body2(c2, cnt):
            return cnt + (hits(2 * c2) + hits(2 * c2 + 1)).astype(F32)

        def body1(c, cnt):
            return cnt + hits(c).astype(F32)

        n2 = n_chunks // 2
        cnt = lax.fori_loop(0, n2, body2, jnp.zeros((16, tq), F32))
        cnt = lax.fori_loop(2 * n2, n_chunks, body1, cnt)
        return cnt.sum(axis=0, keepdims=True).astype(I32)

    def half_search(k):
        def body(b, carry):
            lo, cnt_lo = carry
            cand = lo + jnp.left_shift(jnp.int32(1), 15 - b)
            cnt = half_count(cand)
            ok = cnt >= k
            return jnp.where(ok, cand, lo), jnp.where(ok, cnt, cnt_lo)
        init = (jnp.full((1, tq), i16_min, I32), jnp.full((1, tq), n_chunks * sc, I32))
        return lax.fori_loop(0, 16, body, init)

    thr_hi, _ = half_search(topk)
    n_gt_hi = half_count(thr_hi, strict=True)

    def low_chunk(c, carry):
        key = key_ref[rows(c), :]
        low = (key & 0xFFFF) - 2 ** 15
        half_ref[rows(c), :] = jnp.where((key >> 16) == thr_hi, low, i16_min).astype(jnp.int16)
        return carry

    lax.fori_loop(0, n_chunks, low_chunk, 0)
    thr_lo, n_ge_lo = half_search(topk - n_gt_hi)
    thr = thr_hi * 65536 + (thr_lo + 2 ** 15)
    n_ge = n_gt_hi + n_ge_lo
    has_tie = jnp.max(jnp.where((n_ge > topk) & (thr > NEG_INF_KEY), 1.0, 0.0)) > 0.0
    lim_ref[...] = jnp.full(lim_ref.shape, INT_MAX, I32)

    @pl.when(has_tie)
    def _():
        def column_count(pred):
            def body(c, cnt):
                blk = key_ref[rows(c), :]
                spos = c * sc + lax.broadcasted_iota(I32, (sc, tq), 0)
                hit = jnp.where(pred(blk, spos), 1, 0).astype(I32)
                return cnt + hit.reshape(sc // 8, 8, tq).sum(axis=0)
            cnt = lax.fori_loop(0, n_chunks, body, jnp.zeros((8, tq), I32))
            return cnt.astype(F32).sum(axis=0, keepdims=True).astype(I32)

        need = topk - column_count(lambda blk, spos: blk > thr)
        lim = _tie_limit(
            lambda p: column_count(lambda blk, spos: (blk == thr) & (spos < p)),
            need, pos_bits + 1, jnp.zeros((1, tq), I32))
        lim_ref[...] = jnp.broadcast_to(lim, lim_ref.shape)

    lim = lim_ref[0:1, :]

    def bias_chunk(c, carry):
        blk = key_ref[rows(c), :]
        spos = c * sc + lax.broadcasted_iota(I32, (sc, tq), 0)
        sel = ((blk > thr) | ((blk == thr) & (spos <= lim))) & (blk > NEG_INF_KEY)
        key_ref[rows(c), :] = pltpu.bitcast(jnp.where(sel, 0.0, NEG_BIAS).astype(F32), I32)
        return carry

    lax.fori_loop(0, n_chunks, bias_chunk, 0)

    n_heads = n_kv * n_rep
    n_grp = n_heads // hb
    for j in range(n_grp):
        for r in range(hb):
            head = j * hb + r
            q4_ref[j, r * tq:(r + 1) * tq, :] = q_ref[:, head * hd:(head + 1) * hd]
    m_ref[...] = jnp.full(m_ref.shape, NEG_BIAS, F32)
    acc_ref[...] = jnp.zeros(acc_ref.shape, F32)

    def bias_of(c):
        bias = pltpu.bitcast(key_ref[rows(c), :], F32)
        return jnp.concatenate([bias] * hb, axis=1)

    def qk_store(c, j, bias_w, dst_ref):
        g = j * hb // n_rep
        dst_ref[...] = _nt_dot(k_ref[rows(c), g * hd:(g + 1) * hd], q4_ref[j]) + bias_w

    bufs = (lga_ref, lgb_ref)
    qk_store(0, 0, bias_of(0), bufs[0])

    def attn_chunk(c, carry):
        bias_w = bias_of(c)
        c_next = jnp.minimum(c + 1, n_chunks - 1)
        for j in range(n_grp):
            cur, nxt = bufs[j % 2], bufs[(j + 1) % 2]
            if j + 1 < n_grp:
                qk_store(c, j + 1, bias_w, nxt)
            else:
                qk_store(c_next, 0, bias_of(c_next), nxt)
            logit = cur[...]
            m_old = m_ref[j]
            m_new = jnp.maximum(m_old, jnp.max(logit, axis=0, keepdims=True))
            p = jnp.exp2(logit - m_new).astype(BF16)
            acc_ref[j] = (jnp.exp2(m_old - m_new) * acc_ref[j]
                          + _dot(vt_ref[j * hb // n_rep, c], p))
            m_ref[j] = m_new
        return carry

    lax.fori_loop(0, n_chunks, attn_chunk, 0)
    for j in range(n_grp):
        for r in range(hb):
            head = j * hb + r
            cols = slice(r * tq, (r + 1) * tq)
            out_t = acc_ref[j, 0:hd, cols] / acc_ref[j, hd:hd + 1, cols]
            o_ref[:, head * hd:(head + 1) * hd] = out_t.T.astype(o_ref.dtype)


def _prompt_attention(iq, wt, q, ika, ikb, k_bf, vt, *, tq, sc, topk, hd):
    t, dq = q.shape
    n_kv = k_bf.shape[1] // hd
    n_rep = dq // hd // n_kv
    assert tq >= topk and t % tq == 0 and t % sc == 0 and sc % 16 == 0
    hb = 4
    assert (n_kv * n_rep // hb) % 2 == 0
    n_grp = n_kv * n_rep // hb
    resident = functools.partial(pl.BlockSpec, pipeline_mode=pl.Buffered(1))
    return pl.pallas_call(
        functools.partial(_prompt_attn_kernel, tq=tq, sc=sc, topk=topk, n_rep=n_rep, hd=hd, hb=hb),
        grid=(t // tq,),
        in_specs=[
            pl.BlockSpec((tq, iq.shape[1]), lambda i: (i, 0)),
            pl.BlockSpec((wt.shape[0], tq), lambda i: (0, i)),
            pl.BlockSpec((tq, dq), lambda i: (i, 0)),
            resident(ika.shape, lambda i: (0, 0)),
            resident(ikb.shape, lambda i: (0, 0)),
            resident(k_bf.shape, lambda i: (0, 0)),
            resident(vt.shape, lambda i: (0, 0, 0, 0)),
        ],
        out_specs=pl.BlockSpec((tq, dq), lambda i: (i, 0)),
        out_shape=jax.ShapeDtypeStruct((t, dq), BF16),
        scratch_shapes=[
            pltpu.VMEM((t, tq), I32),
            pltpu.VMEM((t, tq), jnp.int16),
            pltpu.VMEM((8, tq), I32),
            pltpu.VMEM((n_grp, hb * tq, hd), BF16),
            pltpu.VMEM((n_grp, 1, hb * tq), F32),
            pltpu.VMEM((n_grp, vt.shape[2], hb * tq), F32),
            pltpu.VMEM((sc, hb * tq), F32),
            pltpu.VMEM((sc, hb * tq), F32),
        ],
        compiler_params=_cparams(("arbitrary",)),
        name="prompt_attention",
    )(iq, wt, q, ika, ikb, k_bf, vt)


def _sample_score_kernel(pt_ref, iq_ref, w_ref, ikn_ref, cache_ref, o_ref, buf_ref, sem_ref,
                         *, n_pages, page):
    b = pl.program_id(0)
    nb = pl.num_programs(0)

    def page_copy(bb, slot, p):
        return pltpu.make_async_copy(cache_ref.at[pt_ref[bb, p]], buf_ref.at[slot, p],
                                     sem_ref.at[slot])

    def start_all(bb, slot):
        def body(p, carry):
            page_copy(bb, slot, p).start()
            return carry
        lax.fori_loop(0, n_pages, body, 0)

    slot = b % 2

    @pl.when(b == 0)
    def _():
        start_all(b, slot)

    @pl.when(b + 1 < nb)
    def _():
        start_all(b + 1, 1 - slot)

    pltpu.make_async_copy(cache_ref.at[pl.ds(0, n_pages)], buf_ref.at[slot], sem_ref.at[slot]).wait()

    iq = iq_ref[0]
    w = w_ref[0]

    def head_sum(keys_t):
        s = jnp.maximum(_dot(iq, keys_t), 0.0) * w
        return jnp.sum(s, axis=0, keepdims=True)

    group = min(32, n_pages)

    def group_body(gi, carry):
        p0 = pl.multiple_of(gi * group, group)
        blk = buf_ref[slot, pl.ds(p0, group)]
        keys_t = jnp.concatenate([blk[r] for r in range(group)], axis=1).astype(BF16)
        s = head_sum(keys_t)
        for r in range(group):
            o_ref[0, pl.ds(p0 + r, 1), :] = s[:, r * page:(r + 1) * page]
        return carry
    lax.fori_loop(0, n_pages // group, group_body, 0)
    own = head_sum(jnp.broadcast_to(ikn_ref[0], (ikn_ref.shape[1], page)))
    lane = lax.broadcasted_iota(I32, (1, page), 1)
    o_ref[0, pl.ds(n_pages, 1), :] = jnp.where(lane == 0, own, -jnp.inf)
    o_ref[0, pl.ds(n_pages + 1, 7), :] = jnp.full((7, page), -jnp.inf, F32)


def _sample_scores(page_table, iq, w, ik_new, cache_ik_t):
    bs, n_pages = page_table.shape
    idim, page = cache_ik_t.shape[1:]
    rows = n_pages + 8
    grid_spec = pltpu.PrefetchScalarGridSpec(
        num_scalar_prefetch=1,
        grid=(bs,),
        in_specs=[
            pl.BlockSpec((1,) + iq.shape[1:], lambda b, pt: (b, 0, 0)),
            pl.BlockSpec((1,) + w.shape[1:], lambda b, pt: (b, 0, 0)),
            pl.BlockSpec((1,) + ik_new.shape[1:], lambda b, pt: (b, 0, 0)),
            pl.BlockSpec(memory_space=pl.ANY),
        ],
        out_specs=pl.BlockSpec((1, rows, page), lambda b, pt: (b, 0, 0)),
        scratch_shapes=[pltpu.VMEM((2, n_pages, idim, page), F32),
                        pltpu.SemaphoreType.DMA((2,))],
    )
    return pl.pallas_call(
        functools.partial(_sample_score_kernel, n_pages=n_pages, page=page),
        grid_spec=grid_spec,
        out_shape=jax.ShapeDtypeStruct((bs, rows, page), F32),
        compiler_params=_cparams(("arbitrary",)),
        name="sample_scores",
    )(page_table, iq, w, ik_new, cache_ik_t)


def _slab_pos(shape):
    nd = len(shape)
    return (lax.broadcasted_iota(I32, shape, nd - 2) * shape[-1]
            + lax.broadcasted_iota(I32, shape, nd - 1))


def _sample_thresh_kernel(s_ref, thr_ref, lim_ref, *, topk):
    key = _ordered_key(s_ref[...])
    bs, rows, page = key.shape
    pos = _slab_pos(key.shape)

    def count(pred):
        c = jnp.sum(jnp.where(pred, 1.0, 0.0), axis=1, keepdims=True)
        return jnp.sum(c, axis=2, keepdims=True).astype(I32)

    like = jnp.zeros((bs, 1, 1), I32)
    thr = _kth_largest(lambda v: count(key >= v), topk, like)
    need = topk - count(key > thr)
    lim = _tie_limit(lambda p: count((key == thr) & (pos < p)), need,
                     int(rows * page - 1).bit_length() + 1, like)
    thr_ref[...] = thr
    lim_ref[...] = lim


def _sample_compact_kernel(s_ref, thr_ref, lim_ref, idx_ref, *, topk):
    key = _ordered_key(s_ref[0])
    rows, page = key.shape
    kpad = 256
    thr, lim = thr_ref[0], lim_ref[0]
    sel = ((key > thr) | ((key == thr) & (_slab_pos(key.shape) <= lim))) & (key > NEG_INF_KEY)
    self32 = jnp.where(sel, 1.0, 0.0)
    zpad = jnp.zeros((kpad - rows, page), F32)

    def ones_where(pred):
        return jnp.where(pred, 1.0, 0.0).astype(BF16)

    upper = lax.broadcasted_iota(I32, (page, page), 0) < lax.broadcasted_iota(I32, (page, page), 1)
    within = _dot(self32.astype(BF16), ones_where(upper))
    tot = jnp.broadcast_to(jnp.sum(self32, axis=1, keepdims=True), (rows, page))
    tot = jnp.concatenate([tot, zpad], axis=0)
    earlier = lax.broadcasted_iota(I32, (kpad, kpad), 1) < lax.broadcasted_iota(I32, (kpad, kpad), 0)
    before = _dot(ones_where(earlier), tot.astype(BF16))

    before_row = before.T[0:1, :]
    ends_row = (before + tot).T[0:1, :]
    slot_r = lax.broadcasted_iota(I32, (topk, kpad), 0).astype(F32)
    owner = ones_where((before_row <= slot_r) & (slot_r < ends_row))
    code = jnp.concatenate([jnp.where(sel, within, -1.0), zpad], axis=0)
    row_id = lax.broadcasted_iota(I32, (kpad, page), 0).astype(F32)
    fetched = _dot(owner, jnp.concatenate([code, before, row_id], axis=1).astype(BF16))
    code_g, before_g, row_g = (fetched[:, 0:page], fetched[:, page:2 * page],
                               fetched[:, 2 * page:3 * page])
    slot = lax.broadcasted_iota(I32, (topk, page), 0).astype(F32)
    lane = lax.broadcasted_iota(I32, (topk, page), 1).astype(F32)
    total = (before + tot)[kpad - 1:kpad, :]
    hit = (code_g == slot - before_g) & (slot < total)
    lane_sel = jnp.max(jnp.where(hit, lane, -1.0), axis=1, keepdims=True)
    pos = jnp.where(lane_sel >= 0.0, row_g[:, 0:1] * page + lane_sel, -1.0)
    idx_ref[0] = pos.astype(I32)


def _sample_select(scores, *, topk):
    bs, rows, page = scores.shape
    assert rows % 8 == 0 and rows <= 256
    one = pl.BlockSpec((1, 1, 1), lambda b: (b, 0, 0))
    thr, lim = pl.pallas_call(
        functools.partial(_sample_thresh_kernel, topk=topk),
        out_shape=[jax.ShapeDtypeStruct((bs, 1, 1), I32)] * 2,
        compiler_params=pltpu.CompilerParams(vmem_limit_bytes=VMEM_LIMIT),
        name="sample_thresh",
    )(scores)
    return pl.pallas_call(
        functools.partial(_sample_compact_kernel, topk=topk),
        grid=(bs,),
        in_specs=[pl.BlockSpec((1, rows, page), lambda b: (b, 0, 0)), one, one],
        out_specs=pl.BlockSpec((1, topk, 1), lambda b: (b, 0, 0)),
        out_shape=jax.ShapeDtypeStruct((bs, topk, 1), I32),
        compiler_params=_cparams(("arbitrary",)),
        name="sample_compact",
    )(scores, thr, lim)


def _sample_attn_kernel(pt_ref, idx_ref, idxv_ref, q_ref, kn_ref, vn_ref, ck_ref, cv_ref, o_ref,
                        kbuf_ref, vbuf_ref, sem_ref, *, page, past, topk, n_rep):
    b = pl.program_id(0)
    nb = pl.num_programs(0)
    n_kv = kn_ref.shape[1]

    def row_copies(bb, slot, j):
        pos = jnp.clip(idx_ref[bb * topk + j], 0, past - 1)
        if page & (page - 1) == 0:
            page_no = lax.shift_right_logical(pos, page.bit_length() - 1)
            off = pos & (page - 1)
        else:
            page_no, off = lax.div(pos, page), lax.rem(pos, page)
        phys = pt_ref[bb * (past // page) + page_no]
        return (pltpu.make_async_copy(ck_ref.at[phys, off], kbuf_ref.at[slot, j], sem_ref.at[0, slot]),
                pltpu.make_async_copy(cv_ref.at[phys, off], vbuf_ref.at[slot, j], sem_ref.at[1, slot]))

    unroll = 8

    def start_all(bb, slot):
        def body(j8, carry):
            for u in range(unroll):
                ck, cv = row_copies(bb, slot, j8 * unroll + u)
                ck.start()
                cv.start()
            return carry
        lax.fori_loop(0, topk // unroll, body, 0)

    slot = b % 2

    @pl.when(b == 0)
    def _():
        start_all(b, slot)

    @pl.when(b + 1 < nb)
    def _():
        start_all(b + 1, 1 - slot)

    for h in range(topk // page):
        slab = pl.ds(h * page, page)
        pltpu.make_async_copy(ck_ref.at[0], kbuf_ref.at[slot, slab], sem_ref.at[0, slot]).wait()
        pltpu.make_async_copy(cv_ref.at[0], vbuf_ref.at[slot, slab], sem_ref.at[1, slot]).wait()

    pos = idxv_ref[0]
    bias = jnp.where((pos >= 0) & (pos < past), 0.0, NEG_BIAS)
    own_sel = jnp.max(jnp.where(pos == past, 1.0, 0.0), axis=1, keepdims=True)
    own_bias = jnp.where(own_sel > 0.0, 0.0, NEG_BIAS)
    for g in range(n_kv):
        rows = slice(g * n_rep, (g + 1) * n_rep)
        qg = q_ref[0, rows, :]
        kg = kbuf_ref[slot, :, g, :].astype(BF16)
        vg = vbuf_ref[slot, :, g, :].astype(BF16)
        kn = kn_ref[0, g:g + 1, :].astype(BF16).astype(F32)
        vn = vn_ref[0, g:g + 1, :].astype(BF16).astype(F32)
        logit = _nt_dot(qg, kg) + bias
        own = jnp.sum(qg.astype(F32) * kn, axis=1, keepdims=True) + own_bias
        m = jnp.maximum(jnp.max(logit, axis=1, keepdims=True), own)
        p = jnp.exp2(logit - m)
        p_own = jnp.exp2(own - m)
        denom = jnp.sum(p, axis=1, keepdims=True) + p_own
        num = _dot(p.astype(BF16), vg) + p_own.astype(BF16).astype(F32) * vn
        o_ref[0, rows, :] = num / denom


def _sample_attention(page_table, idx, q, k_new, v_new, cache_k, cache_v, *, topk):
    bs, n_pages = page_table.shape
    page, n_kv, hd = cache_k.shape[1:]
    n_heads = q.shape[1]
    past = n_pages * page
    assert topk % page == 0 and topk % 8 == 0
    grid_spec = pltpu.PrefetchScalarGridSpec(
        num_scalar_prefetch=2,
        grid=(bs,),
        in_specs=[
            pl.BlockSpec((1, 1, topk), lambda b, pt, ix: (b, 0, 0)),
            pl.BlockSpec((1, n_heads, hd), lambda b, pt, ix: (b, 0, 0)),
            pl.BlockSpec((1, n_kv, hd), lambda b, pt, ix: (b, 0, 0)),
            pl.BlockSpec((1, n_kv, hd), lambda b, pt, ix: (b, 0, 0)),
            pl.BlockSpec(memory_space=pl.ANY),
            pl.BlockSpec(memory_space=pl.ANY),
        ],
        out_specs=pl.BlockSpec((1, n_heads, hd), lambda b, pt, ix: (b, 0, 0)),
        scratch_shapes=[pltpu.VMEM((2, topk, n_kv, hd), F32),
                        pltpu.VMEM((2, topk, n_kv, hd), F32),
                        pltpu.SemaphoreType.DMA((2, 2))],
    )
    return pl.pallas_call(
        functools.partial(_sample_attn_kernel, page=page, past=past, topk=topk,
                          n_rep=n_heads // n_kv),
        grid_spec=grid_spec,
        out_shape=jax.ShapeDtypeStruct((bs, n_heads, hd), F32),
        compiler_params=_cparams(("arbitrary",)),
        name="sample_attention",
    )(page_table.reshape(-1), idx.reshape(-1), idx.reshape(bs, 1, topk), q, k_new, v_new,
      cache_k, cache_v)


def _rope_tables(pos, dim, pad_lanes=0):
    rot = dim // ROT_DIV
    half = rot // 2
    inv = ROPE_THETA ** (-np.arange(half, dtype=np.float64) / half)
    ang = np.asarray(pos, np.float64)[:, None] * inv[None, :]
    cos, sin = jnp.asarray(np.cos(ang), F32), jnp.asarray(np.sin(ang), F32)
    width = dim + pad_lanes
    a = jnp.pad(jnp.concatenate([cos, cos], axis=1), ((0, 0), (0, width - rot)), constant_values=1.0)
    b = jnp.pad(-sin, ((0, 0), (0, width - half)))
    c = jnp.pad(sin, ((0, 0), (half, width - rot)))
    reps = LANES // width
    return tuple(jnp.tile(t, (1, reps)) if reps > 1 else t for t in (a, b, c))


def _in_proj_columns(d_model, d_conv, dq, dkv):
    names = ("x", "b", "c", "q", "k", "v", "iq", "ik", "iw", "gc", "ga")
    sizes = (d_conv, d_conv, d_conv, dq, dkv, dkv, IDX_HEADS * IDX_DIM, IDX_DIM, IDX_HEADS,
             d_model, d_model)
    starts = np.concatenate([[0], np.cumsum(sizes)[:-1]])
    return {n: int(s) for n, s in zip(names, starts)}


def _mixer_common(u, w_all, w_gates, cols, ktabs, itabs, *, hd, dq, dkv, tm, tn):
    q = _rope_proj(u, w_all, ktabs, first_col=cols["q"], n=dq, half=hd // ROT_DIV // 2,
                   scale=hd ** -0.5 * LOG2E, tm=tm, tn=tn)
    iq = _rope_proj(u, w_all, itabs[0], first_col=cols["iq"], n=IDX_HEADS * IDX_DIM,
                    half=IDX_DIM // ROT_DIV // 2, scale=IDX_DIM ** -0.5, tm=tm, tn=tn)
    assert cols["iw"] == cols["ik"] + IDX_DIM
    k, v, misc, k_bf, v_bf, ika, ikb = _kv_proj(
        u, w_all, ktabs, itabs[1], k_col=cols["k"], v_col=cols["v"], misc_col=cols["ik"],
        dkv=dkv, tm=tm)
    gc, ga = _gate_proj(u, w_gates, tm=tm, tn=tn)
    return q, iq, k, v, misc, k_bf, v_bf, ika, ikb, gc, ga


def kernel(x_prompt, x_sample, cache_k, cache_v, cache_idx_k, state_conv, page_table,
           norm_ffn1_pre, norm_ffn1_post, w_ffn1_gate_up, w_ffn1_down,
           norm_mix_pre, norm_mix_post, w_in, w_conv, w_conv_out, w_attn_out, w_out,
           norm_ffn2_pre, norm_ffn2_post, w_ffn2_gate_up, w_ffn2_down):
    bp, t, d = x_prompt.shape
    bs, ts, _ = x_sample.shape
    depth = w_in.shape[0]
    page, n_kv, hd = cache_k.shape[2:]
    n_pages = page_table.shape[1]
    past = n_pages * page
    d_conv = w_conv.shape[2]
    dq = w_attn_out.shape[1]
    dkv = n_kv * hd
    assert bp == 1 and ts == 1 and dq == N_HEADS * hd and n_kv == N_KV_HEADS

    tm = min(512, t)
    tf = 512
    tn = 1024
    tq, sc = min(256, t), min(512, t)
    cols = _in_proj_columns(d, d_conv, dq, dkv)
    topk_p = min(TOPK_MAX, t // 4)
    topk_s = min(TOPK_MAX, (past + ts) // 4)

    pos_p = np.arange(t)
    pos_s = np.full((bs,), past)
    ktabs_p, ktabs_s = _rope_tables(pos_p, hd), _rope_tables(pos_s, hd)
    itabs_p = (_rope_tables(pos_p, IDX_DIM), _rope_tables(pos_p, IDX_DIM, LANES - IDX_DIM))
    itabs_s = (_rope_tables(pos_s, IDX_DIM), _rope_tables(pos_s, IDX_DIM, LANES - IDX_DIM))

    hp = x_prompt.reshape(t, d)
    hs = x_sample.reshape(bs, d)
    outs = [[] for _ in range(8)]
    row = lambda a: a.reshape(1, -1)
    for l in range(depth):
        w1gu, w1d = w_ffn1_gate_up[l].astype(BF16), w_ffn1_down[l].astype(BF16)
        w2gu, w2d = w_ffn2_gate_up[l].astype(BF16), w_ffn2_down[l].astype(BF16)
        w_all, w_gates = w_in[l].astype(BF16), w_in[l][:, cols["gc"]:].astype(BF16)
        w_co, w_ao, w_o = (w_conv_out[l].astype(BF16), w_attn_out[l].astype(BF16),
                           w_out[l].astype(BF16))
        gain_mix = row(norm_mix_pre[l])

        hp = _ffn_half(hp, row(norm_ffn1_pre[l]), row(norm_ffn1_post[l]), w1gu, w1d, tm=tm, tf=tf)
        up = _rms_cast(hp, gain_mix, tm=tm)
        conv_y, tail = _conv_seq(up, w_all, w_conv[l], tm=tm, tn=tn)
        q, iq, k, v, misc, k_bf, v_bf, ika, ikb, gc, ga = _mixer_common(
            up, w_all, w_gates, cols, ktabs_p, itabs_p, hd=hd, dq=dq, dkv=dkv, tm=tm, tn=tn)
        wt = (misc[:, IDX_DIM:IDX_DIM + IDX_HEADS] * IDX_HEADS ** -0.5).T
        vt = v_bf.reshape(t // sc, sc, n_kv, hd).transpose(2, 0, 3, 1)
        vt = jnp.concatenate([vt, jnp.ones((n_kv, t // sc, 16, sc), BF16)], axis=2)
        attn_o = _prompt_attention(iq, wt, q, ika, ikb, k_bf, vt, tq=tq, sc=sc, topk=topk_p, hd=hd)
        hp = _merge(hp, conv_y, attn_o, gc, ga, row(norm_mix_post[l]), w_co, w_ao, w_o,
                    tm=tm, tn=tn // 2)
        outs[0].append(k.reshape(bp, t // page, page, n_kv, hd))
        outs[1].append(v.reshape(bp, t // page, page, n_kv, hd))
        outs[2].append(misc[:, :IDX_DIM].reshape(bp, t // page, page, IDX_DIM))
        outs[3].append(tail[tail.shape[0] - (CONV_WIDTH - 1):].reshape(bp, CONV_WIDTH - 1, d_conv))
        hp = _ffn_half(hp, row(norm_ffn2_pre[l]), row(norm_ffn2_post[l]), w2gu, w2d, tm=tm, tf=tf)

        hs = _ffn_half(hs, row(norm_ffn1_pre[l]), row(norm_ffn1_post[l]), w1gu, w1d, tm=bs, tf=tf)
        st = state_conv[l]
        us = _rms_cast(hs, gain_mix, tm=bs)
        conv_y, z = _conv_step(us, w_all, w_conv[l], st[:, 0, :], st[:, 1, :], tn=tn)
        q, iq, k, v, misc, k_bf, v_bf, ika, ikb, gc, ga = _mixer_common(
            us, w_all, w_gates, cols, ktabs_s, itabs_s, hd=hd, dq=dq, dkv=dkv, tm=bs, tn=tn)
        w_idx = (misc[:, IDX_DIM:IDX_DIM + IDX_HEADS] * IDX_HEADS ** -0.5).reshape(bs, IDX_HEADS, 1)
        scores = _sample_scores(page_table, iq.reshape(bs, IDX_HEADS, IDX_DIM), w_idx,
                                ika[:, :IDX_DIM].reshape(bs, IDX_DIM, 1),
                                cache_idx_k[l].transpose(0, 2, 1))
        idx = _sample_select(scores, topk=topk_s).reshape(bs, topk_s)
        attn_o = _sample_attention(page_table, idx, q.reshape(bs, N_HEADS, hd),
                                   k.reshape(bs, n_kv, hd), v.reshape(bs, n_kv, hd),
                                   cache_k[l], cache_v[l], topk=topk_s)
        hs = _merge(hs, conv_y, attn_o.reshape(bs, dq).astype(BF16), gc, ga, row(norm_mix_post[l]),
                    w_co, w_ao, w_o, tm=bs, tn=tn // 2)
        outs[4].append(k.reshape(bs, ts, n_kv, hd))
        outs[5].append(v.reshape(bs, ts, n_kv, hd))
        outs[6].append(misc[:, :IDX_DIM].reshape(bs, ts, IDX_DIM))
        outs[7].append(jnp.stack([st[:, 1, :], z], axis=1))
        hs = _ffn_half(hs, row(norm_ffn2_pre[l]), row(norm_ffn2_post[l]), w2gu, w2d, tm=bs, tf=tf)

    return (hp.reshape(bp, t, d), hs.reshape(bs, ts, d)) + tuple(jnp.stack(o) for o in outs)
```

```python
import functools

import numpy as np
import jax
import jax.numpy as jnp
from jax import lax
from jax.experimental import pallas as pl
from jax.experimental.pallas import tpu as pltpu

F32 = jnp.float32
BF16 = jnp.bfloat16
I32 = jnp.int32

N_HEADS = 16
N_KV_HEADS = 4
IDX_HEADS = 16
IDX_DIM = 64
TOPK_MAX = 256
CONV_WIDTH = 3
ROPE_THETA = 500000.0
ROT_DIV = 4
EPS = 1e-6

LANES = 128
VMEM_LIMIT = 56 * 1024 * 1024
LOG2E = 1.4426950408889634
NEG_BIAS = -1e30
INT_MIN = -2 ** 31
INT_MAX = 2 ** 31 - 1
NEG_INF_KEY = int(np.int32(np.uint32(0x807FFFFF)))


def _cparams(sem):
    return pltpu.CompilerParams(dimension_semantics=sem, vmem_limit_bytes=VMEM_LIMIT)


def _rms_scale(x):
    return lax.rsqrt(jnp.mean(x * x, axis=-1, keepdims=True) + EPS)


def _nt_dot(a, b):
    return lax.dot_general(a, b, (((1,), (1,)), ((), ())), preferred_element_type=F32)


def _dot(a, b):
    return jnp.dot(a, b, preferred_element_type=F32)


def _sigmoid(x):
    return 0.5 * jnp.tanh(0.5 * x) + 0.5


def _ffn_accumulate(x_ref, pre_ref, wg_ref, wu_ref, wd_ref, xn_ref, acc_ref):
    @pl.when(pl.program_id(1) == 0)
    def _():
        x = x_ref[...]
        xn_ref[...] = (x * _rms_scale(x) * pre_ref[...]).astype(BF16)
        acc_ref[...] = jnp.zeros_like(acc_ref)

    xn = xn_ref[...]
    g = _dot(xn, wg_ref[...])
    u = _dot(xn, wu_ref[...])
    h = (g * _sigmoid(g) * u).astype(BF16)
    acc_ref[...] += _dot(h, wd_ref[...])


def _ffn_result(x_ref, post_ref, acc_ref):
    y = acc_ref[...]
    return x_ref[...] + 0.5 * (y * _rms_scale(y) * post_ref[...])


def _ffn_kernel(x_ref, pre_ref, post_ref, wg_ref, wu_ref, wd_ref, o_ref, xn_ref, acc_ref):
    _ffn_accumulate(x_ref, pre_ref, wg_ref, wu_ref, wd_ref, xn_ref, acc_ref)

    @pl.when(pl.program_id(1) == pl.num_programs(1) - 1)
    def _():
        o_ref[...] = _ffn_result(x_ref, post_ref, acc_ref)


def _ffn_norm_kernel(x_ref, pre_ref, post_ref, ng_ref, wg_ref, wu_ref, wd_ref, o_ref, u_ref,
                     xn_ref, acc_ref):
    _ffn_accumulate(x_ref, pre_ref, wg_ref, wu_ref, wd_ref, xn_ref, acc_ref)

    @pl.when(pl.program_id(1) == pl.num_programs(1) - 1)
    def _():
        out = _ffn_result(x_ref, post_ref, acc_ref)
        o_ref[...] = out
        u_ref[...] = (out * _rms_scale(out) * ng_ref[...]).astype(BF16)


def _ffn_half(x, pre, post, w_gu, w_d, *, tm, tf, next_gain=None):
    m, d = x.shape
    f = w_d.shape[0]
    nf = f // tf
    row_spec = pl.BlockSpec((tm, d), lambda i, j: (i, 0))
    vec_spec = pl.BlockSpec((1, d), lambda i, j: (0, 0))
    w_specs = [
        pl.BlockSpec((d, tf), lambda i, j: (0, j)),
        pl.BlockSpec((d, tf), lambda i, j: (0, nf + j)),
        pl.BlockSpec((tf, d), lambda i, j: (j, 0)),
    ]
    common = dict(
        grid=(m // tm, nf),
        scratch_shapes=[pltpu.VMEM((tm, d), BF16), pltpu.VMEM((tm, d), F32)],
        compiler_params=_cparams(("arbitrary", "arbitrary")),
    )
    if next_gain is None:
        return pl.pallas_call(
            _ffn_kernel,
            in_specs=[row_spec, vec_spec, vec_spec] + w_specs,
            out_specs=row_spec,
            out_shape=jax.ShapeDtypeStruct((m, d), F32),
            name="ffn_half", **common,
        )(x, pre, post, w_gu, w_gu, w_d)
    return pl.pallas_call(
        _ffn_norm_kernel,
        in_specs=[row_spec, vec_spec, vec_spec, vec_spec] + w_specs,
        out_specs=[row_spec, row_spec],
        out_shape=[jax.ShapeDtypeStruct((m, d), F32), jax.ShapeDtypeStruct((m, d), BF16)],
        name="ffn_half_norm", **common,
    )(x, pre, post, next_gain, w_gu, w_gu, w_d)


def _merge_kernel(h_ref, cy_ref, ao_ref, gc_ref, ga_ref, post_ref, wc_ref, wa_ref, wo_ref,
                  o_ref, acc_ref):
    j = pl.program_id(1)

    @pl.when(j == 0)
    def _():
        acc_ref[...] = jnp.zeros_like(acc_ref)

    mc = _dot(cy_ref[...], wc_ref[...])
    ma = _dot(ao_ref[...], wa_ref[...])
    mix = gc_ref[...].astype(F32) * mc + ga_ref[...].astype(F32) * ma
    acc_ref[...] += _dot(mix.astype(BF16), wo_ref[...])

    @pl.when(j == pl.num_programs(1) - 1)
    def _():
        y = acc_ref[...]
        o_ref[...] = h_ref[...] + y * _rms_scale(y) * post_ref[...]


def _merge(h, conv_y, attn_o, gc, ga, post, w_co, w_ao, w_out, *, tm, tn):
    m, d = h.shape
    return pl.pallas_call(
        _merge_kernel,
        grid=(m // tm, d // tn),
        in_specs=[
            pl.BlockSpec((tm, d), lambda i, j: (i, 0)),
            pl.BlockSpec((tm, d), lambda i, j: (i, 0)),
            pl.BlockSpec((tm, d), lambda i, j: (i, 0)),
            pl.BlockSpec((tm, tn), lambda i, j: (i, j)),
            pl.BlockSpec((tm, tn), lambda i, j: (i, j)),
            pl.BlockSpec((1, d), lambda i, j: (0, 0)),
            pl.BlockSpec((d, tn), lambda i, j: (0, j)),
            pl.BlockSpec((d, tn), lambda i, j: (0, j)),
            pl.BlockSpec((tn, d), lambda i, j: (j, 0)),
        ],
        out_specs=pl.BlockSpec((tm, d), lambda i, j: (i, 0)),
        out_shape=jax.ShapeDtypeStruct((m, d), F32),
        scratch_shapes=[pltpu.VMEM((tm, d), F32)],
        compiler_params=_cparams(("arbitrary", "arbitrary")),
        name="merge",
    )(h, conv_y, attn_o, gc, ga, post, w_co, w_ao, w_out)


def _rope(x, a, b, c, half):
    return x * a + pltpu.roll(x, LANES - half, axis=1) * b + pltpu.roll(x, half, axis=1) * c


def _conv_seq_kernel(u_ref, wx_ref, wb_ref, wc_ref, wconv_ref, cy_ref, tail_ref, carry_ref):
    i = pl.program_id(0)
    j = pl.program_id(1)
    xn = u_ref[...]
    z = _dot(xn, wc_ref[...]) * _dot(xn, wx_ref[...])
    b = _dot(xn, wb_ref[...])
    tm = z.shape[0]

    @pl.when(i == 0)
    def _():
        carry_ref[j] = jnp.zeros(carry_ref.shape[1:], F32)

    prev = carry_ref[j]
    row = lax.broadcasted_iota(I32, z.shape, 0)
    z1 = jnp.where(row == 0, prev[1:2, :], pltpu.roll(z, 1, axis=0))
    z2 = jnp.where(row == 0, prev[0:1, :],
                   jnp.where(row == 1, prev[1:2, :], pltpu.roll(z, 2, axis=0)))
    w = wconv_ref[...]
    cy_ref[...] = (b * (z2 * w[0:1, :] + z1 * w[1:2, :] + z * w[2:3, :])).astype(BF16)
    tail = z[tm - 8:, :]
    carry_ref[j] = pltpu.roll(tail, 2, axis=0)
    tail_ref[...] = tail


def _col_blocks(d, tn, first_col):
    assert first_col % tn == 0
    return pl.BlockSpec((d, tn), lambda i, j: (0, first_col // tn + j))


def _conv_seq(u, w_all, w_conv, *, tm, tn):
    m, d = u.shape
    dc = w_conv.shape[1]
    return pl.pallas_call(
        _conv_seq_kernel,
        grid=(m // tm, dc // tn),
        in_specs=[
            pl.BlockSpec((tm, d), lambda i, j: (i, 0)),
            _col_blocks(d, tn, 0), _col_blocks(d, tn, dc), _col_blocks(d, tn, 2 * dc),
            pl.BlockSpec((CONV_WIDTH, tn), lambda i, j: (0, j)),
        ],
        out_specs=[
            pl.BlockSpec((tm, tn), lambda i, j: (i, j)),
            pl.BlockSpec((8, tn), lambda i, j: (i, j)),
        ],
        out_shape=[jax.ShapeDtypeStruct((m, dc), BF16), jax.ShapeDtypeStruct((m // tm * 8, dc), F32)],
        scratch_shapes=[pltpu.VMEM((dc // tn, 8, tn), F32)],
        compiler_params=_cparams(("arbitrary", "arbitrary")),
        name="conv_seq",
    )(u, w_all, w_all, w_all, w_conv)


def _conv_step_kernel(u_ref, wx_ref, wb_ref, wc_ref, wconv_ref, s0_ref, s1_ref, cy_ref, z_ref):
    xn = u_ref[...]
    z = _dot(xn, wc_ref[...]) * _dot(xn, wx_ref[...])
    b = _dot(xn, wb_ref[...])
    w = wconv_ref[...]
    cy_ref[...] = (b * (s0_ref[...] * w[0:1, :] + s1_ref[...] * w[1:2, :] + z * w[2:3, :])
                   ).astype(BF16)
    z_ref[...] = z


def _conv_step(u, w_all, w_conv, s0, s1, *, tn):
    m, d = u.shape
    dc = w_conv.shape[1]
    cspec = pl.BlockSpec((m, tn), lambda i, j: (0, j))
    return pl.pallas_call(
        _conv_step_kernel,
        grid=(1, dc // tn),
        in_specs=[
            pl.BlockSpec((m, d), lambda i, j: (0, 0)),
            _col_blocks(d, tn, 0), _col_blocks(d, tn, dc), _col_blocks(d, tn, 2 * dc),
            pl.BlockSpec((CONV_WIDTH, tn), lambda i, j: (0, j)),
            cspec, cspec,
        ],
        out_specs=[cspec, cspec],
        out_shape=[jax.ShapeDtypeStruct((m, dc), BF16), jax.ShapeDtypeStruct((m, dc), F32)],
        compiler_params=_cparams(("arbitrary", "arbitrary")),
        name="conv_step",
    )(u, w_all, w_all, w_all, w_conv, s0, s1)


def _rope_proj_kernel(u_ref, w_ref, a_ref, b_ref, c_ref, o_ref, *, half, scale):
    y = _dot(u_ref[...], w_ref[...])
    a, b, c = a_ref[...], b_ref[...], c_ref[...]
    for g in range(y.shape[1] // LANES):
        sl = slice(g * LANES, (g + 1) * LANES)
        o_ref[:, sl] = (_rope(y[:, sl], a, b, c, half) * scale).astype(o_ref.dtype)


def _rope_proj(u, w_all, tabs, *, first_col, n, half, scale, tm, tn):
    m, d = u.shape
    tspec = pl.BlockSpec((tm, LANES), lambda i, j: (i, 0))
    return pl.pallas_call(
        functools.partial(_rope_proj_kernel, half=half, scale=scale),
        grid=(m // tm, n // tn),
        in_specs=[
            pl.BlockSpec((tm, d), lambda i, j: (i, 0)),
            _col_blocks(d, tn, first_col),
            tspec, tspec, tspec,
        ],
        out_specs=pl.BlockSpec((tm, tn), lambda i, j: (i, j)),
        out_shape=jax.ShapeDtypeStruct((m, n), BF16),
        compiler_params=_cparams(("arbitrary", "arbitrary")),
        name="rope_proj",
    )(u, w_all, *tabs)


def _kv_proj_kernel(u_ref, wk_ref, wv_ref, wm_ref, ka_ref, kb_ref, kc_ref,
                    ia_ref, ib_ref, ic_ref,
                    k_ref, v_ref, misc_ref, kbf_ref, vbf_ref, ika_ref, ikb_ref, *, dkv):
    xn = u_ref[...]
    yk = _dot(xn, wk_ref[...])
    a, b, c = ka_ref[...], kb_ref[...], kc_ref[...]
    for g in range(dkv // LANES):
        sl = slice(g * LANES, (g + 1) * LANES)
        kg = _rope(yk[:, sl], a, b, c, LANES // ROT_DIV // 2)
        k_ref[:, sl] = kg
        kbf_ref[:, sl] = kg.astype(BF16)
    v = _dot(xn, wv_ref[...])
    v_ref[...] = v
    vbf_ref[...] = v.astype(BF16)
    misc = _rope(_dot(xn, wm_ref[...]), ia_ref[...], ib_ref[...], ic_ref[...],
                 IDX_DIM // ROT_DIV // 2)
    misc_ref[...] = misc
    lane = lax.broadcasted_iota(I32, misc.shape, 1)
    ik_lo = jnp.where(lane < IDX_DIM, misc, 0.0)
    ika_ref[...] = ik_lo.astype(BF16)
    ikb_ref[...] = pltpu.roll(ik_lo, IDX_DIM, axis=1).astype(BF16)


def _kv_proj(u, w_all, ktabs, itabs, *, k_col, v_col, misc_col, dkv, tm):
    m, d = u.shape
    assert k_col % dkv == 0 and v_col % dkv == 0 and misc_col % LANES == 0
    tspec = pl.BlockSpec((tm, LANES), lambda i: (i, 0))
    kvspec = pl.BlockSpec((tm, dkv), lambda i: (i, 0))
    mspec = pl.BlockSpec((tm, LANES), lambda i: (i, 0))
    return pl.pallas_call(
        functools.partial(_kv_proj_kernel, dkv=dkv),
        grid=(m // tm,),
        in_specs=[
            pl.BlockSpec((tm, d), lambda i: (i, 0)),
            pl.BlockSpec((d, dkv), lambda i: (0, k_col // dkv)),
            pl.BlockSpec((d, dkv), lambda i: (0, v_col // dkv)),
            pl.BlockSpec((d, LANES), lambda i: (0, misc_col // LANES)),
            tspec, tspec, tspec, tspec, tspec, tspec,
        ],
        out_specs=[kvspec, kvspec, mspec, kvspec, kvspec, mspec, mspec],
        out_shape=[
            jax.ShapeDtypeStruct((m, dkv), F32), jax.ShapeDtypeStruct((m, dkv), F32),
            jax.ShapeDtypeStruct((m, LANES), F32),
            jax.ShapeDtypeStruct((m, dkv), BF16), jax.ShapeDtypeStruct((m, dkv), BF16),
            jax.ShapeDtypeStruct((m, LANES), BF16), jax.ShapeDtypeStruct((m, LANES), BF16),
        ],
        compiler_params=_cparams(("arbitrary",)),
        name="kv_proj",
    )(u, w_all, w_all, w_all, *ktabs, *itabs)


def _gate_proj_kernel(u_ref, wc_ref, wa_ref, gc_ref, ga_ref):
    xn = u_ref[...]
    gc_ref[...] = _sigmoid(_dot(xn, wc_ref[...])).astype(BF16)
    ga_ref[...] = _sigmoid(_dot(xn, wa_ref[...])).astype(BF16)


def _gate_proj(u, w_gates, *, tm, tn):
    m, d = u.shape
    n = w_gates.shape[1] // 2
    nb = n // tn
    ospec = pl.BlockSpec((tm, tn), lambda i, j: (i, j))
    return pl.pallas_call(
        _gate_proj_kernel,
        grid=(m // tm, nb),
        in_specs=[
            pl.BlockSpec((tm, d), lambda i, j: (i, 0)),
            pl.BlockSpec((d, tn), lambda i, j: (0, j)),
            pl.BlockSpec((d, tn), lambda i, j: (0, nb + j)),
        ],
        out_specs=[ospec, ospec],
        out_shape=[jax.ShapeDtypeStruct((m, n), BF16), jax.ShapeDtypeStruct((m, n), BF16)],
        compiler_params=_cparams(("arbitrary", "arbitrary")),
        name="gate_proj",
    )(u, w_gates, w_gates)


def _ordered_key(score):
    bits = pltpu.bitcast(score, I32)
    return bits ^ ((bits >> 31) & INT_MAX)


def _kth_largest(count_ge, k, like):
    def body(b, lo):
        cand = lo + jnp.left_shift(jnp.int32(1), 31 - b)
        return jnp.where(count_ge(cand) >= k, cand, lo)
    return lax.fori_loop(0, 32, body, jnp.full_like(like, INT_MIN))


def _tie_limit(count_eq_before, need, nbits, like):
    def body(b, lo):
        cand = lo + jnp.left_shift(jnp.int32(1), nbits - 1 - b)
        return jnp.where(count_eq_before(cand) < need, cand, lo)
    return lax.fori_loop(0, nbits, body, jnp.zeros_like(like))


def _prompt_attn_kernel(iq_ref, wt_ref, q_ref, ika_ref, ikb_ref, k_ref, vt_ref, o_ref,
                        key_ref, half_ref, lim_ref, q4_ref, m_ref, acc_ref, lga_ref, lgb_ref,
                        *, tq, sc, topk, n_rep, hd, hb):
    i = pl.program_id(0)
    t0 = i * tq
    n_chunks = (t0 + tq + sc - 1) // sc
    n_pairs = iq_ref.shape[1] // LANES
    n_kv = k_ref.shape[1] // hd
    pos_bits = int(k_ref.shape[0] - 1).bit_length()
    i16_min = -2 ** 15

    def rows(c):
        return pl.ds(pl.multiple_of(c * sc, sc), sc)

    def score_chunk(c, carry):
        ka = ika_ref[rows(c), :]
        kb = ikb_ref[rows(c), :]
        acc = jnp.zeros((sc, tq), F32)
        for p in range(n_pairs):
            iq_p = iq_ref[:, p * LANES:(p + 1) * LANES]
            acc += jnp.maximum(_nt_dot(ka, iq_p), 0.0) * wt_ref[2 * p:2 * p + 1, :]
            acc += jnp.maximum(_nt_dot(kb, iq_p), 0.0) * wt_ref[2 * p + 1:2 * p + 2, :]
        spos = c * sc + lax.broadcasted_iota(I32, (sc, tq), 0)
        tpos = t0 + lax.broadcasted_iota(I32, (sc, tq), 1)
        key = _ordered_key(jnp.where(spos <= tpos, acc, -jnp.inf))
        key_ref[rows(c), :] = key
        half_ref[rows(c), :] = (key >> 16).astype(jnp.int16)
        return carry

    lax.fori_loop(0, n_chunks, score_chunk, 0)

    def half_count(cand, strict=False):
        cand16 = cand.astype(jnp.int16)

        def hits(c):
            blk = half_ref[rows(c), :]
            hit = jnp.where(blk > cand16 if strict else blk >= cand16,
                            jnp.bfloat16(1), jnp.bfloat16(0))
            part = hit[0:16]
            for r in range(1, sc // 16):
                part = part + hit[r * 16:(r + 1) * 16]
            return part

        def body2(c2, cnt):
            return cnt + (hits(2 * c2) + hits(2 * c2 + 1)).astype(F32)

        def body1(c, cnt):
            return cnt + hits(c).astype(F32)

        n2 = n_chunks // 2
        cnt = lax.fori_loop(0, n2, body2, jnp.zeros((16, tq), F32))
        cnt = lax.fori_loop(2 * n2, n_chunks, body1, cnt)
        return cnt.sum(axis=0, keepdims=True).astype(I32)

    def half_search(k):
        def body(b, carry):
            lo, cnt_lo = carry
            cand = lo + jnp.left_shift(jnp.int32(1), 15 - b)
            cnt = half_count(cand)
            ok = cnt >= k
            return jnp.where(ok, cand, lo), jnp.where(ok, cnt, cnt_lo)
        init = (jnp.full((1, tq), i16_min, I32), jnp.full((1, tq), n_chunks * sc, I32))
        return lax.fori_loop(0, 16, body, init)

    thr_hi, _ = half_search(topk)
    n_gt_hi = half_count(thr_hi, strict=True)

    def low_chunk(c, carry):
        key = key_ref[rows(c), :]
        low = (key & 0xFFFF) - 2 ** 15
        half_ref[rows(c), :] = jnp.where((key >> 16) == thr_hi, low, i16_min).astype(jnp.int16)
        return carry

    lax.fori_loop(0, n_chunks, low_chunk, 0)
    thr_lo, n_ge_lo = half_search(topk - n_gt_hi)
    thr = thr_hi * 65536 + (thr_lo + 2 ** 15)
    n_ge = n_gt_hi + n_ge_lo
    has_tie = jnp.max(jnp.where((n_ge > topk) & (thr > NEG_INF_KEY), 1.0, 0.0)) > 0.0
    lim_ref[...] = jnp.full(lim_ref.shape, INT_MAX, I32)

    @pl.when(has_tie)
    def _():
        def column_count(pred):
            def body(c, cnt):
                blk = key_ref[rows(c), :]
                spos = c * sc + lax.broadcasted_iota(I32, (sc, tq), 0)
                hit = jnp.where(pred(blk, spos), 1, 0).astype(I32)
                return cnt + hit.reshape(sc // 8, 8, tq).sum(axis=0)
            cnt = lax.fori_loop(0, n_chunks, body, jnp.zeros((8, tq), I32))
            return cnt.astype(F32).sum(axis=0, keepdims=True).astype(I32)

        need = topk - column_count(lambda blk, spos: blk > thr)
        lim = _tie_limit(
            lambda p: column_count(lambda blk, spos: (blk == thr) & (spos < p)),
            need, pos_bits + 1, jnp.zeros((1, tq), I32))
        lim_ref[...] = jnp.broadcast_to(lim, lim_ref.shape)

    lim = lim_ref[0:1, :]

    def bias_chunk(c, carry):
        blk = key_ref[rows(c), :]
        spos = c * sc + lax.broadcasted_iota(I32, (sc, tq), 0)
        sel = ((blk > thr) | ((blk == thr) & (spos <= lim))) & (blk > NEG_INF_KEY)
        key_ref[rows(c), :] = pltpu.bitcast(jnp.where(sel, 0.0, NEG_BIAS).astype(F32), I32)
        return carry

    lax.fori_loop(0, n_chunks, bias_chunk, 0)

    n_heads = n_kv * n_rep
    n_grp = n_heads // hb
    for j in range(n_grp):
        for r in range(hb):
            head = j * hb + r
            q4_ref[j, r * tq:(r + 1) * tq, :] = q_ref[:, head * hd:(head + 1) * hd]
    m_ref[...] = jnp.full(m_ref.shape, NEG_BIAS, F32)
    acc_ref[...] = jnp.zeros(acc_ref.shape, F32)

    def bias_of(c):
        bias = pltpu.bitcast(key_ref[rows(c), :], F32)
        return jnp.concatenate([bias] * hb, axis=1)

    def qk_store(c, j, bias_w, dst_ref):
        g = j * hb // n_rep
        dst_ref[...] = _nt_dot(k_ref[rows(c), g * hd:(g + 1) * hd], q4_ref[j]) + bias_w

    bufs = (lga_ref, lgb_ref)
    qk_store(0, 0, bias_of(0), bufs[0])

    def attn_chunk(c, carry):
        bias_w = bias_of(c)
        c_next = jnp.minimum(c + 1, n_chunks - 1)
        for j in range(n_grp):
            cur, nxt = bufs[j % 2], bufs[(j + 1) % 2]
            if j + 1 < n_grp:
                qk_store(c, j + 1, bias_w, nxt)
            else:
                qk_store(c_next, 0, bias_of(c_next), nxt)
            logit = cur[...]
            m_old = m_ref[j]
            m_new = jnp.maximum(m_old, jnp.max(logit, axis=0, keepdims=True))
            p = jnp.exp2(logit - m_new).astype(BF16)
            acc_ref[j] = (jnp.exp2(m_old - m_new) * acc_ref[j]
                          + _dot(vt_ref[j * hb // n_rep, c], p))
            m_ref[j] = m_new
        return carry

    lax.fori_loop(0, n_chunks, attn_chunk, 0)
    for j in range(n_grp):
        for r in range(hb):
            head = j * hb + r
            cols = slice(r * tq, (r + 1) * tq)
            out_t = acc_ref[j, 0:hd, cols] / acc_ref[j, hd:hd + 1, cols]
            o_ref[:, head * hd:(head + 1) * hd] = out_t.T.astype(o_ref.dtype)


def _prompt_attention(iq, wt, q, ika, ikb, k_bf, vt, *, tq, sc, topk, hd):
    t, dq = q.shape
    n_kv = k_bf.shape[1] // hd
    n_rep = dq // hd // n_kv
    assert tq >= topk and t % tq == 0 and t % sc == 0 and sc % 16 == 0
    hb = 4
    assert (n_kv * n_rep // hb) % 2 == 0
    n_grp = n_kv * n_rep // hb
    resident = functools.partial(pl.BlockSpec, pipeline_mode=pl.Buffered(1))
    return pl.pallas_call(
        functools.partial(_prompt_attn_kernel, tq=tq, sc=sc, topk=topk, n_rep=n_rep, hd=hd, hb=hb),
        grid=(t // tq,),
        in_specs=[
            pl.BlockSpec((tq, iq.shape[1]), lambda i: (i, 0)),
            pl.BlockSpec((wt.shape[0], tq), lambda i: (0, i)),
            pl.BlockSpec((tq, dq), lambda i: (i, 0)),
            resident(ika.shape, lambda i: (0, 0)),
            resident(ikb.shape, lambda i: (0, 0)),
            resident(k_bf.shape, lambda i: (0, 0)),
            resident(vt.shape, lambda i: (0, 0, 0, 0)),
        ],
        out_specs=pl.BlockSpec((tq, dq), lambda i: (i, 0)),
        out_shape=jax.ShapeDtypeStruct((t, dq), BF16),
        scratch_shapes=[
            pltpu.VMEM((t, tq), I32),
            pltpu.VMEM((t, tq), jnp.int16),
            pltpu.VMEM((8, tq), I32),
            pltpu.VMEM((n_grp, hb * tq, hd), BF16),
            pltpu.VMEM((n_grp, 1, hb * tq), F32),
            pltpu.VMEM((n_grp, vt.shape[2], hb * tq), F32),
            pltpu.VMEM((sc, hb * tq), F32),
            pltpu.VMEM((sc, hb * tq), F32),
        ],
        compiler_params=_cparams(("arbitrary",)),
        name="prompt_attention",
    )(iq, wt, q, ika, ikb, k_bf, vt)


def _sample_score_kernel(pt_ref, iq_ref, w_ref, ikn_ref, cache_ref, o_ref, buf_ref, sem_ref,
                         *, n_pages, page):
    b = pl.program_id(0)
    nb = pl.num_programs(0)

    def page_copy(bb, slot, p):
        return pltpu.make_async_copy(cache_ref.at[pt_ref[bb, p]], buf_ref.at[slot, p],
                                     sem_ref.at[slot])

    def start_all(bb, slot):
        def body(p, carry):
            page_copy(bb, slot, p).start()
            return carry
        lax.fori_loop(0, n_pages, body, 0)

    slot = b % 2

    @pl.when(b == 0)
    def _():
        start_all(b, slot)

    @pl.when(b + 1 < nb)
    def _():
        start_all(b + 1, 1 - slot)

    pltpu.make_async_copy(cache_ref.at[pl.ds(0, n_pages)], buf_ref.at[slot], sem_ref.at[slot]).wait()

    iq = iq_ref[0]
    w = w_ref[0]

    def head_sum(keys_t):
        s = jnp.maximum(_dot(iq, keys_t), 0.0) * w
        return jnp.sum(s, axis=0, keepdims=True)

    group = min(32, n_pages)

    def group_body(gi, carry):
        p0 = pl.multiple_of(gi * group, group)
        blk = buf_ref[slot, pl.ds(p0, group)]
        keys_t = jnp.concatenate([blk[r] for r in range(group)], axis=1).astype(BF16)
        s = head_sum(keys_t)
        for r in range(group):
            o_ref[0, pl.ds(p0 + r, 1), :] = s[:, r * page:(r + 1) * page]
        return carry
    lax.fori_loop(0, n_pages // group, group_body, 0)
    own = head_sum(jnp.broadcast_to(ikn_ref[0], (ikn_ref.shape[1], page)))
    lane = lax.broadcasted_iota(I32, (1, page), 1)
    o_ref[0, pl.ds(n_pages, 1), :] = jnp.where(lane == 0, own, -jnp.inf)
    o_ref[0, pl.ds(n_pages + 1, 7), :] = jnp.full((7, page), -jnp.inf, F32)


def _sample_scores(page_table, iq, w, ik_new, cache_ik_t):
    bs, n_pages = page_table.shape
    idim, page = cache_ik_t.shape[1:]
    rows = n_pages + 8
    grid_spec = pltpu.PrefetchScalarGridSpec(
        num_scalar_prefetch=1,
        grid=(bs,),
        in_specs=[
            pl.BlockSpec((1,) + iq.shape[1:], lambda b, pt: (b, 0, 0)),
            pl.BlockSpec((1,) + w.shape[1:], lambda b, pt: (b, 0, 0)),
            pl.BlockSpec((1,) + ik_new.shape[1:], lambda b, pt: (b, 0, 0)),
            pl.BlockSpec(memory_space=pl.ANY),
        ],
        out_specs=pl.BlockSpec((1, rows, page), lambda b, pt: (b, 0, 0)),
        scratch_shapes=[pltpu.VMEM((2, n_pages, idim, page), F32),
                        pltpu.SemaphoreType.DMA((2,))],
    )
    return pl.pallas_call(
        functools.partial(_sample_score_kernel, n_pages=n_pages, page=page),
        grid_spec=grid_spec,
        out_shape=jax.ShapeDtypeStruct((bs, rows, page), F32),
        compiler_params=_cparams(("arbitrary",)),
        name="sample_scores",
    )(page_table, iq, w, ik_new, cache_ik_t)


def _slab_pos(shape):
    nd = len(shape)
    return (lax.broadcasted_iota(I32, shape, nd - 2) * shape[-1]
            + lax.broadcasted_iota(I32, shape, nd - 1))


def _sample_thresh_kernel(s_ref, thr_ref, lim_ref, *, topk):
    key = _ordered_key(s_ref[...])
    bs, rows, page = key.shape
    pos = _slab_pos(key.shape)

    def count(pred):
        c = jnp.sum(jnp.where(pred, 1.0, 0.0), axis=1, keepdims=True)
        return jnp.sum(c, axis=2, keepdims=True).astype(I32)

    like = jnp.zeros((bs, 1, 1), I32)
    thr = _kth_largest(lambda v: count(key >= v), topk, like)
    need = topk - count(key > thr)
    lim = _tie_limit(lambda p: count((key == thr) & (pos < p)), need,
                     int(rows * page - 1).bit_length() + 1, like)
    thr_ref[...] = thr
    lim_ref[...] = lim


def _sample_compact_kernel(s_ref, thr_ref, lim_ref, idx_ref, *, topk):
    key = _ordered_key(s_ref[0])
    rows, page = key.shape
    kpad = 256
    thr, lim = thr_ref[0], lim_ref[0]
    sel = ((key > thr) | ((key == thr) & (_slab_pos(key.shape) <= lim))) & (key > NEG_INF_KEY)
    self32 = jnp.where(sel, 1.0, 0.0)
    zpad = jnp.zeros((kpad - rows, page), F32)

    def ones_where(pred):
        return jnp.where(pred, 1.0, 0.0).astype(BF16)

    upper = lax.broadcasted_iota(I32, (page, page), 0) < lax.broadcasted_iota(I32, (page, page), 1)
    within = _dot(self32.astype(BF16), ones_where(upper))
    tot = jnp.broadcast_to(jnp.sum(self32, axis=1, keepdims=True), (rows, page))
    tot = jnp.concatenate([tot, zpad], axis=0)
    earlier = lax.broadcasted_iota(I32, (kpad, kpad), 1) < lax.broadcasted_iota(I32, (kpad, kpad), 0)
    before = _dot(ones_where(earlier), tot.astype(BF16))

    before_row = before.T[0:1, :]
    ends_row = (before + tot).T[0:1, :]
    slot_r = lax.broadcasted_iota(I32, (topk, kpad), 0).astype(F32)
    owner = ones_where((before_row <= slot_r) & (slot_r < ends_row))
    code = jnp.concatenate([jnp.where(sel, within, -1.0), zpad], axis=0)
    row_id = lax.broadcasted_iota(I32, (kpad, page), 0).astype(F32)
    fetched = _dot(owner, jnp.concatenate([code, before, row_id], axis=1).astype(BF16))
    code_g, before_g, row_g = (fetched[:, 0:page], fetched[:, page:2 * page],
                               fetched[:, 2 * page:3 * page])
    slot = lax.broadcasted_iota(I32, (topk, page), 0).astype(F32)
    lane = lax.broadcasted_iota(I32, (topk, page), 1).astype(F32)
    total = (before + tot)[kpad - 1:kpad, :]
    hit = (code_g == slot - before_g) & (slot < total)
    lane_sel = jnp.max(jnp.where(hit, lane, -1.0), axis=1, keepdims=True)
    pos = jnp.where(lane_sel >= 0.0, row_g[:, 0:1] * page + lane_sel, -1.0)
    idx_ref[0] = pos.astype(I32)


def _sample_select(scores, *, topk):
    bs, rows, page = scores.shape
    assert rows % 8 == 0 and rows <= 256
    one = pl.BlockSpec((1, 1, 1), lambda b: (b, 0, 0))
    thr, lim = pl.pallas_call(
        functools.partial(_sample_thresh_kernel, topk=topk),
        out_shape=[jax.ShapeDtypeStruct((bs, 1, 1), I32)] * 2,
        compiler_params=pltpu.CompilerParams(vmem_limit_bytes=VMEM_LIMIT),
        name="sample_thresh",
    )(scores)
    return pl.pallas_call(
        functools.partial(_sample_compact_kernel, topk=topk),
        grid=(bs,),
        in_specs=[pl.BlockSpec((1, rows, page), lambda b: (b, 0, 0)), one, one],
        out_specs=pl.BlockSpec((1, topk, 1), lambda b: (b, 0, 0)),
        out_shape=jax.ShapeDtypeStruct((bs, topk, 1), I32),
        compiler_params=_cparams(("arbitrary",)),
        name="sample_compact",
    )(scores, thr, lim)


def _sample_attn_kernel(pt_ref, idx_ref, idxv_ref, q_ref, kn_ref, vn_ref, ck_ref, cv_ref, o_ref,
                        kbuf_ref, vbuf_ref, sem_ref, *, page, past, topk, n_rep):
    b = pl.program_id(0)
    nb = pl.num_programs(0)
    n_kv = kn_ref.shape[1]

    def row_copies(bb, slot, j):
        pos = jnp.clip(idx_ref[bb * topk + j], 0, past - 1)
        if page & (page - 1) == 0:
            page_no = lax.shift_right_logical(pos, page.bit_length() - 1)
            off = pos & (page - 1)
        else:
            page_no, off = lax.div(pos, page), lax.rem(pos, page)
        phys = pt_ref[bb * (past // page) + page_no]
        return (pltpu.make_async_copy(ck_ref.at[phys, off], kbuf_ref.at[slot, j], sem_ref.at[0, slot]),
                pltpu.make_async_copy(cv_ref.at[phys, off], vbuf_ref.at[slot, j], sem_ref.at[1, slot]))

    unroll = 8

    def start_all(bb, slot):
        def body(j8, carry):
            for u in range(unroll):
                ck, cv = row_copies(bb, slot, j8 * unroll + u)
                ck.start()
                cv.start()
            return carry
        lax.fori_loop(0, topk // unroll, body, 0)

    slot = b % 2

    @pl.when(b == 0)
    def _():
        start_all(b, slot)

    @pl.when(b + 1 < nb)
    def _():
        start_all(b + 1, 1 - slot)

    for h in range(topk // page):
        slab = pl.ds(h * page, page)
        pltpu.make_async_copy(ck_ref.at[0], kbuf_ref.at[slot, slab], sem_ref.at[0, slot]).wait()
        pltpu.make_async_copy(cv_ref.at[0], vbuf_ref.at[slot, slab], sem_ref.at[1, slot]).wait()

    pos = idxv_ref[0]
    bias = jnp.where((pos >= 0) & (pos < past), 0.0, NEG_BIAS)
    own_sel = jnp.max(jnp.where(pos == past, 1.0, 0.0), axis=1, keepdims=True)
    own_bias = jnp.where(own_sel > 0.0, 0.0, NEG_BIAS)
    k_all = kbuf_ref[slot].astype(BF16)
    v_all = vbuf_ref[slot].astype(BF16)
    for g in range(n_kv):
        rows = slice(g * n_rep, (g + 1) * n_rep)
        qg = q_ref[0, rows, :]
        kg = k_all[:, g, :]
        vg = v_all[:, g, :]
        kn = kn_ref[0, g:g + 1, :].astype(BF16).astype(F32)
        vn = vn_ref[0, g:g + 1, :].astype(BF16).astype(F32)
        logit = _nt_dot(qg, kg) + bias
        own = jnp.sum(qg.astype(F32) * kn, axis=1, keepdims=True) + own_bias
        m = jnp.maximum(jnp.max(logit, axis=1, keepdims=True), own)
        p = jnp.exp2(logit - m)
        p_own = jnp.exp2(own - m)
        denom = jnp.sum(p, axis=1, keepdims=True) + p_own
        num = _dot(p.astype(BF16), vg) + p_own.astype(BF16).astype(F32) * vn
        o_ref[0, rows, :] = num / denom


def _sample_attention(page_table, idx, q, k_new, v_new, cache_k, cache_v, *, topk):
    bs, n_pages = page_table.shape
    page, n_kv, hd = cache_k.shape[1:]
    n_heads = q.shape[1]
    past = n_pages * page
    assert topk % page == 0 and topk % 8 == 0
    grid_spec = pltpu.PrefetchScalarGridSpec(
        num_scalar_prefetch=2,
        grid=(bs,),
        in_specs=[
            pl.BlockSpec((1, 1, topk), lambda b, pt, ix: (b, 0, 0)),
            pl.BlockSpec((1, n_heads, hd), lambda b, pt, ix: (b, 0, 0)),
            pl.BlockSpec((1, n_kv, hd), lambda b, pt, ix: (b, 0, 0)),
            pl.BlockSpec((1, n_kv, hd), lambda b, pt, ix: (b, 0, 0)),
            pl.BlockSpec(memory_space=pl.ANY),
            pl.BlockSpec(memory_space=pl.ANY),
        ],
        out_specs=pl.BlockSpec((1, n_heads, hd), lambda b, pt, ix: (b, 0, 0)),
        scratch_shapes=[pltpu.VMEM((2, topk, n_kv, hd), F32),
                        pltpu.VMEM((2, topk, n_kv, hd), F32),
                        pltpu.SemaphoreType.DMA((2, 2))],
    )
    return pl.pallas_call(
        functools.partial(_sample_attn_kernel, page=page, past=past, topk=topk,
                          n_rep=n_heads // n_kv),
        grid_spec=grid_spec,
        out_shape=jax.ShapeDtypeStruct((bs, n_heads, hd), F32),
        compiler_params=_cparams(("arbitrary",)),
        name="sample_attention",
    )(page_table.reshape(-1), idx.reshape(-1), idx.reshape(bs, 1, topk), q, k_new, v_new,
      cache_k, cache_v)


def _rope_tables(pos, dim, pad_lanes=0):
    rot = dim // ROT_DIV
    half = rot // 2
    inv = ROPE_THETA ** (-np.arange(half, dtype=np.float64) / half)
    ang = np.asarray(pos, np.float64)[:, None] * inv[None, :]
    cos, sin = jnp.asarray(np.cos(ang), F32), jnp.asarray(np.sin(ang), F32)
    width = dim + pad_lanes
    a = jnp.pad(jnp.concatenate([cos, cos], axis=1), ((0, 0), (0, width - rot)), constant_values=1.0)
    b = jnp.pad(-sin, ((0, 0), (0, width - half)))
    c = jnp.pad(sin, ((0, 0), (half, width - rot)))
    reps = LANES // width
    return tuple(jnp.tile(t, (1, reps)) if reps > 1 else t for t in (a, b, c))


def _in_proj_columns(d_model, d_conv, dq, dkv):
    names = ("x", "b", "c", "q", "k", "v", "iq", "ik", "iw", "gc", "ga")
    sizes = (d_conv, d_conv, d_conv, dq, dkv, dkv, IDX_HEADS * IDX_DIM, IDX_DIM, IDX_HEADS,
             d_model, d_model)
    starts = np.concatenate([[0], np.cumsum(sizes)[:-1]])
    return {n: int(s) for n, s in zip(names, starts)}


def _mixer_common(u, w_all, w_gates, cols, ktabs, itabs, *, hd, dq, dkv, tm, tn):
    q = _rope_proj(u, w_all, ktabs, first_col=cols["q"], n=dq, half=hd // ROT_DIV // 2,
                   scale=hd ** -0.5 * LOG2E, tm=tm, tn=tn)
    iq = _rope_proj(u, w_all, itabs[0], first_col=cols["iq"], n=IDX_HEADS * IDX_DIM,
                    half=IDX_DIM // ROT_DIV // 2, scale=IDX_DIM ** -0.5, tm=tm, tn=tn)
    assert cols["iw"] == cols["ik"] + IDX_DIM
    k, v, misc, k_bf, v_bf, ika, ikb = _kv_proj(
        u, w_all, ktabs, itabs[1], k_col=cols["k"], v_col=cols["v"], misc_col=cols["ik"],
        dkv=dkv, tm=tm)
    gc, ga = _gate_proj(u, w_gates, tm=tm, tn=tn)
    return q, iq, k, v, misc, k_bf, v_bf, ika, ikb, gc, ga


def kernel(x_prompt, x_sample, cache_k, cache_v, cache_idx_k, state_conv, page_table,
           norm_ffn1_pre, norm_ffn1_post, w_ffn1_gate_up, w_ffn1_down,
           norm_mix_pre, norm_mix_post, w_in, w_conv, w_conv_out, w_attn_out, w_out,
           norm_ffn2_pre, norm_ffn2_post, w_ffn2_gate_up, w_ffn2_down):
    bp, t, d = x_prompt.shape
    bs, ts, _ = x_sample.shape
    depth = w_in.shape[0]
    page, n_kv, hd = cache_k.shape[2:]
    n_pages = page_table.shape[1]
    past = n_pages * page
    d_conv = w_conv.shape[2]
    dq = w_attn_out.shape[1]
    dkv = n_kv * hd
    assert bp == 1 and ts == 1 and dq == N_HEADS * hd and n_kv == N_KV_HEADS

    tm = min(512, t)
    tf = 512
    tn = 1024
    tq, sc = min(256, t), min(512, t)
    cols = _in_proj_columns(d, d_conv, dq, dkv)
    topk_p = min(TOPK_MAX, t // 4)
    topk_s = min(TOPK_MAX, (past + ts) // 4)

    pos_p = np.arange(t)
    pos_s = np.full((bs,), past)
    ktabs_p, ktabs_s = _rope_tables(pos_p, hd), _rope_tables(pos_s, hd)
    itabs_p = (_rope_tables(pos_p, IDX_DIM), _rope_tables(pos_p, IDX_DIM, LANES - IDX_DIM))
    itabs_s = (_rope_tables(pos_s, IDX_DIM), _rope_tables(pos_s, IDX_DIM, LANES - IDX_DIM))

    hp = x_prompt.reshape(t, d)
    hs = x_sample.reshape(bs, d)
    outs = [[] for _ in range(8)]
    row = lambda a: a.reshape(1, -1)
    for l in range(depth):
        w1gu, w1d = w_ffn1_gate_up[l].astype(BF16), w_ffn1_down[l].astype(BF16)
        w2gu, w2d = w_ffn2_gate_up[l].astype(BF16), w_ffn2_down[l].astype(BF16)
        w_all, w_gates = w_in[l].astype(BF16), w_in[l][:, cols["gc"]:].astype(BF16)
        w_co, w_ao, w_o = (w_conv_out[l].astype(BF16), w_attn_out[l].astype(BF16),
                           w_out[l].astype(BF16))
        gain_mix = row(norm_mix_pre[l])

        hp, up = _ffn_half(hp, row(norm_ffn1_pre[l]), row(norm_ffn1_post[l]), w1gu, w1d,
                           tm=tm, tf=tf, next_gain=gain_mix)
        conv_y, tail = _conv_seq(up, w_all, w_conv[l], tm=tm, tn=tn)
        q, iq, k, v, misc, k_bf, v_bf, ika, ikb, gc, ga = _mixer_common(
            up, w_all, w_gates, cols, ktabs_p, itabs_p, hd=hd, dq=dq, dkv=dkv, tm=tm, tn=tn)
        wt = (misc[:, IDX_DIM:IDX_DIM + IDX_HEADS] * IDX_HEADS ** -0.5).T
        vt = v_bf.reshape(t // sc, sc, n_kv, hd).transpose(2, 0, 3, 1)
        vt = jnp.concatenate([vt, jnp.ones((n_kv, t // sc, 16, sc), BF16)], axis=2)
        attn_o = _prompt_attention(iq, wt, q, ika, ikb, k_bf, vt, tq=tq, sc=sc, topk=topk_p, hd=hd)
        hp = _merge(hp, conv_y, attn_o, gc, ga, row(norm_mix_post[l]), w_co, w_ao, w_o,
                    tm=tm, tn=tn // 2)
        outs[0].append(k.reshape(bp, t // page, page, n_kv, hd))
        outs[1].append(v.reshape(bp, t // page, page, n_kv, hd))
        outs[2].append(misc[:, :IDX_DIM].reshape(bp, t // page, page, IDX_DIM))
        outs[3].append(tail[tail.shape[0] - (CONV_WIDTH - 1):].reshape(bp, CONV_WIDTH - 1, d_conv))
        hp = _ffn_half(hp, row(norm_ffn2_pre[l]), row(norm_ffn2_post[l]), w2gu, w2d, tm=tm, tf=tf)

        hs, us = _ffn_half(hs, row(norm_ffn1_pre[l]), row(norm_ffn1_post[l]), w1gu, w1d,
                           tm=bs, tf=tf, next_gain=gain_mix)
        st = state_conv[l]
        conv_y, z = _conv_step(us, w_all, w_conv[l], st[:, 0, :], st[:, 1, :], tn=tn)
        q, iq, k, v, misc, k_bf, v_bf, ika, ikb, gc, ga = _mixer_common(
            us, w_all, w_gates, cols, ktabs_s, itabs_s, hd=hd, dq=dq, dkv=dkv, tm=bs, tn=tn)
        w_idx = (misc[:, IDX_DIM:IDX_DIM + IDX_HEADS] * IDX_HEADS ** -0.5).reshape(bs, IDX_HEADS, 1)
        scores = _sample_scores(page_table, iq.reshape(bs, IDX_HEADS, IDX_DIM), w_idx,
                                ika[:, :IDX_DIM].reshape(bs, IDX_DIM, 1),
                                cache_idx_k[l].transpose(0, 2, 1))
        idx = _sample_select(scores, topk=topk_s).reshape(bs, topk_s)
        attn_o = _sample_attention(page_table, idx, q.reshape(bs, N_HEADS, hd),
                                   k.reshape(bs, n_kv, hd), v.reshape(bs, n_kv, hd),
                                   cache_k[l], cache_v[l], topk=topk_s)
        hs = _merge(hs, conv_y, attn_o.reshape(bs, dq).astype(BF16), gc, ga, row(norm_mix_post[l]),
                    w_co, w_ao, w_o, tm=bs, tn=tn // 2)
        outs[4].append(k.reshape(bs, ts, n_kv, hd))
        outs[5].append(v.reshape(bs, ts, n_kv, hd))
        outs[6].append(misc[:, :IDX_DIM].reshape(bs, ts, IDX_DIM))
        outs[7].append(jnp.stack([st[:, 1, :], z], axis=1))
        hs = _ffn_half(hs, row(norm_ffn2_pre[l]), row(norm_ffn2_post[l]), w2gu, w2d, tm=bs, tf=tf)

    return (hp.reshape(bp, t, d), hs.reshape(bs, ts, d)) + tuple(jnp.stack(o) for o in outs)
```

```python
import functools

import numpy as np
import jax
import jax.numpy as jnp
from jax import lax
from jax.experimental import pallas as pl
from jax.experimental.pallas import tpu as pltpu

F32 = jnp.float32
BF16 = jnp.bfloat16
I32 = jnp.int32

N_HEADS = 16
N_KV_HEADS = 4
IDX_HEADS = 16
IDX_DIM = 64
TOPK_MAX = 256
CONV_WIDTH = 3
ROPE_THETA = 500000.0
ROT_DIV = 4
EPS = 1e-6

LANES = 128
VMEM_LIMIT = 56 * 1024 * 1024
LOG2E = 1.4426950408889634
NEG_BIAS = -1e30
INT_MIN = -2 ** 31
INT_MAX = 2 ** 31 - 1
NEG_INF_KEY = int(np.int32(np.uint32(0x807FFFFF)))


def _cparams(sem):
    return pltpu.CompilerParams(dimension_semantics=sem, vmem_limit_bytes=VMEM_LIMIT)


def _rms_scale(x):
    return lax.rsqrt(jnp.mean(x * x, axis=-1, keepdims=True) + EPS)


def _nt_dot(a, b):
    return lax.dot_general(a, b, (((1,), (1,)), ((), ())), preferred_element_type=F32)


def _dot(a, b):
    return jnp.dot(a, b, preferred_element_type=F32)


def _sigmoid(x):
    return 0.5 * jnp.tanh(0.5 * x) + 0.5


def _ffn_accumulate(x_ref, pre_ref, wg_ref, wu_ref, wd_ref, xn_ref, acc_ref):
    @pl.when(pl.program_id(1) == 0)
    def _():
        x = x_ref[...]
        xn_ref[...] = (x * _rms_scale(x) * pre_ref[...]).astype(BF16)
        acc_ref[...] = jnp.zeros_like(acc_ref)

    xn = xn_ref[...]
    g = _dot(xn, wg_ref[...])
    u = _dot(xn, wu_ref[...])
    h = (g * _sigmoid(g) * u).astype(BF16)
    acc_ref[...] += _dot(h, wd_ref[...])


def _ffn_result(x_ref, post_ref, acc_ref):
    y = acc_ref[...]
    return x_ref[...] + 0.5 * (y * _rms_scale(y) * post_ref[...])


def _ffn_kernel(x_ref, pre_ref, post_ref, wg_ref, wu_ref, wd_ref, o_ref, xn_ref, acc_ref):
    _ffn_accumulate(x_ref, pre_ref, wg_ref, wu_ref, wd_ref, xn_ref, acc_ref)

    @pl.when(pl.program_id(1) == pl.num_programs(1) - 1)
    def _():
        o_ref[...] = _ffn_result(x_ref, post_ref, acc_ref)


def _ffn_norm_kernel(x_ref, pre_ref, post_ref, ng_ref, wg_ref, wu_ref, wd_ref, o_ref, u_ref,
                     xn_ref, acc_ref):
    _ffn_accumulate(x_ref, pre_ref, wg_ref, wu_ref, wd_ref, xn_ref, acc_ref)

    @pl.when(pl.program_id(1) == pl.num_programs(1) - 1)
    def _():
        out = _ffn_result(x_ref, post_ref, acc_ref)
        o_ref[...] = out
        u_ref[...] = (out * _rms_scale(out) * ng_ref[...]).astype(BF16)


def _ffn_half(x, pre, post, w_gu, w_d, *, tm, tf, next_gain=None):
    m, d = x.shape
    f = w_d.shape[0]
    nf = f // tf
    row_spec = pl.BlockSpec((tm, d), lambda i, j: (i, 0))
    vec_spec = pl.BlockSpec((1, d), lambda i, j: (0, 0))
    w_specs = [
        pl.BlockSpec((d, tf), lambda i, j: (0, j)),
        pl.BlockSpec((d, tf), lambda i, j: (0, nf + j)),
        pl.BlockSpec((tf, d), lambda i, j: (j, 0)),
    ]
    common = dict(
        grid=(m // tm, nf),
        scratch_shapes=[pltpu.VMEM((tm, d), BF16), pltpu.VMEM((tm, d), F32)],
        compiler_params=_cparams(("arbitrary", "arbitrary")),
    )
    if next_gain is None:
        return pl.pallas_call(
            _ffn_kernel,
            in_specs=[row_spec, vec_spec, vec_spec] + w_specs,
            out_specs=row_spec,
            out_shape=jax.ShapeDtypeStruct((m, d), F32),
            name="ffn_half", **common,
        )(x, pre, post, w_gu, w_gu, w_d)
    return pl.pallas_call(
        _ffn_norm_kernel,
        in_specs=[row_spec, vec_spec, vec_spec, vec_spec] + w_specs,
        out_specs=[row_spec, row_spec],
        out_shape=[jax.ShapeDtypeStruct((m, d), F32), jax.ShapeDtypeStruct((m, d), BF16)],
        name="ffn_half_norm", **common,
    )(x, pre, post, next_gain, w_gu, w_gu, w_d)


def _merge_kernel(h_ref, cy_ref, ao_ref, gc_ref, ga_ref, post_ref, wc_ref, wa_ref, wo_ref,
                  o_ref, mix_ref):
    j = pl.program_id(1)
    nj = mix_ref.shape[0]
    mc = _dot(cy_ref[...], wc_ref[...])
    ma = _dot(ao_ref[...], wa_ref[...])
    mix_ref[j] = (gc_ref[...].astype(F32) * mc + ga_ref[...].astype(F32) * ma).astype(BF16)

    @pl.when(j == nj - 1)
    def _():
        mix = jnp.concatenate([mix_ref[jj] for jj in range(nj)], axis=1)
        y = _dot(mix, wo_ref[...])
        o_ref[...] = h_ref[...] + y * _rms_scale(y) * post_ref[...]


def _merge(h, conv_y, attn_o, gc, ga, post, w_co, w_ao, w_out, *, tm, tn):
    m, d = h.shape
    return pl.pallas_call(
        _merge_kernel,
        grid=(m // tm, d // tn),
        in_specs=[
            pl.BlockSpec((tm, d), lambda i, j: (i, 0)),
            pl.BlockSpec((tm, d), lambda i, j: (i, 0)),
            pl.BlockSpec((tm, d), lambda i, j: (i, 0)),
            pl.BlockSpec((tm, tn), lambda i, j: (i, j)),
            pl.BlockSpec((tm, tn), lambda i, j: (i, j)),
            pl.BlockSpec((1, d), lambda i, j: (0, 0)),
            pl.BlockSpec((d, tn), lambda i, j: (0, j)),
            pl.BlockSpec((d, tn), lambda i, j: (0, j)),
            pl.BlockSpec((d, d), lambda i, j: (0, 0), pipeline_mode=pl.Buffered(1)),
        ],
        out_specs=pl.BlockSpec((tm, d), lambda i, j: (i, 0)),
        out_shape=jax.ShapeDtypeStruct((m, d), F32),
        scratch_shapes=[pltpu.VMEM((d // tn, tm, tn), BF16)],
        compiler_params=_cparams(("arbitrary", "arbitrary")),
        name="merge",
    )(h, conv_y, attn_o, gc, ga, post, w_co, w_ao, w_out)


def _rope(x, a, b, c, half):
    return x * a + pltpu.roll(x, LANES - half, axis=1) * b + pltpu.roll(x, half, axis=1) * c


def _conv_seq_kernel(u_ref, wx_ref, wb_ref, wc_ref, wconv_ref, cy_ref, tail_ref, carry_ref):
    i = pl.program_id(0)
    j = pl.program_id(1)
    xn = u_ref[...]
    z = _dot(xn, wc_ref[...]) * _dot(xn, wx_ref[...])
    b = _dot(xn, wb_ref[...])
    tm = z.shape[0]

    @pl.when(i == 0)
    def _():
        carry_ref[j] = jnp.zeros(carry_ref.shape[1:], F32)

    prev = carry_ref[j]
    row = lax.broadcasted_iota(I32, z.shape, 0)
    z1 = jnp.where(row == 0, prev[1:2, :], pltpu.roll(z, 1, axis=0))
    z2 = jnp.where(row == 0, prev[0:1, :],
                   jnp.where(row == 1, prev[1:2, :], pltpu.roll(z, 2, axis=0)))
    w = wconv_ref[...]
    cy_ref[...] = (b * (z2 * w[0:1, :] + z1 * w[1:2, :] + z * w[2:3, :])).astype(BF16)
    tail = z[tm - 8:, :]
    carry_ref[j] = pltpu.roll(tail, 2, axis=0)
    tail_ref[...] = tail


def _col_blocks(d, tn, first_col):
    assert first_col % tn == 0
    return pl.BlockSpec((d, tn), lambda i, j: (0, first_col // tn + j))


def _conv_seq(u, w_all, w_conv, *, tm, tn):
    m, d = u.shape
    dc = w_conv.shape[1]
    return pl.pallas_call(
        _conv_seq_kernel,
        grid=(m // tm, dc // tn),
        in_specs=[
            pl.BlockSpec((tm, d), lambda i, j: (i, 0)),
            _col_blocks(d, tn, 0), _col_blocks(d, tn, dc), _col_blocks(d, tn, 2 * dc),
            pl.BlockSpec((CONV_WIDTH, tn), lambda i, j: (0, j)),
        ],
        out_specs=[
            pl.BlockSpec((tm, tn), lambda i, j: (i, j)),
            pl.BlockSpec((8, tn), lambda i, j: (i, j)),
        ],
        out_shape=[jax.ShapeDtypeStruct((m, dc), BF16), jax.ShapeDtypeStruct((m // tm * 8, dc), F32)],
        scratch_shapes=[pltpu.VMEM((dc // tn, 8, tn), F32)],
        compiler_params=_cparams(("arbitrary", "arbitrary")),
        name="conv_seq",
    )(u, w_all, w_all, w_all, w_conv)


def _conv_step_kernel(u_ref, wx_ref, wb_ref, wc_ref, wconv_ref, s0_ref, s1_ref, cy_ref, z_ref):
    xn = u_ref[...]
    z = _dot(xn, wc_ref[...]) * _dot(xn, wx_ref[...])
    b = _dot(xn, wb_ref[...])
    w = wconv_ref[...]
    cy_ref[...] = (b * (s0_ref[...] * w[0:1, :] + s1_ref[...] * w[1:2, :] + z * w[2:3, :])
                   ).astype(BF16)
    z_ref[...] = z


def _conv_step(u, w_all, w_conv, s0, s1, *, tn):
    m, d = u.shape
    dc = w_conv.shape[1]
    cspec = pl.BlockSpec((m, tn), lambda i, j: (0, j))
    return pl.pallas_call(
        _conv_step_kernel,
        grid=(1, dc // tn),
        in_specs=[
            pl.BlockSpec((m, d), lambda i, j: (0, 0)),
            _col_blocks(d, tn, 0), _col_blocks(d, tn, dc), _col_blocks(d, tn, 2 * dc),
            pl.BlockSpec((CONV_WIDTH, tn), lambda i, j: (0, j)),
            cspec, cspec,
        ],
        out_specs=[cspec, cspec],
        out_shape=[jax.ShapeDtypeStruct((m, dc), BF16), jax.ShapeDtypeStruct((m, dc), F32)],
        compiler_params=_cparams(("arbitrary", "arbitrary")),
        name="conv_step",
    )(u, w_all, w_all, w_all, w_conv, s0, s1)


def _rope_proj_kernel(u_ref, w_ref, a_ref, b_ref, c_ref, o_ref, *, half, scale):
    y = _dot(u_ref[...], w_ref[...])
    a, b, c = a_ref[...], b_ref[...], c_ref[...]
    for g in range(y.shape[1] // LANES):
        sl = slice(g * LANES, (g + 1) * LANES)
        o_ref[:, sl] = (_rope(y[:, sl], a, b, c, half) * scale).astype(o_ref.dtype)


def _rope_proj(u, w_all, tabs, *, first_col, n, half, scale, tm, tn):
    m, d = u.shape
    tspec = pl.BlockSpec((tm, LANES), lambda i, j: (i, 0))
    return pl.pallas_call(
        functools.partial(_rope_proj_kernel, half=half, scale=scale),
        grid=(m // tm, n // tn),
        in_specs=[
            pl.BlockSpec((tm, d), lambda i, j: (i, 0)),
            _col_blocks(d, tn, first_col),
            tspec, tspec, tspec,
        ],
        out_specs=pl.BlockSpec((tm, tn), lambda i, j: (i, j)),
        out_shape=jax.ShapeDtypeStruct((m, n), BF16),
        compiler_params=_cparams(("arbitrary", "arbitrary")),
        name="rope_proj",
    )(u, w_all, *tabs)


def _kv_proj_kernel(u_ref, wk_ref, wv_ref, wm_ref, ka_ref, kb_ref, kc_ref,
                    ia_ref, ib_ref, ic_ref,
                    k_ref, v_ref, misc_ref, kbf_ref, vbf_ref, ika_ref, ikb_ref, *, dkv):
    xn = u_ref[...]
    yk = _dot(xn, wk_ref[...])
    a, b, c = ka_ref[...], kb_ref[...], kc_ref[...]
    for g in range(dkv // LANES):
        sl = slice(g * LANES, (g + 1) * LANES)
        kg = _rope(yk[:, sl], a, b, c, LANES // ROT_DIV // 2)
        k_ref[:, sl] = kg
        kbf_ref[:, sl] = kg.astype(BF16)
    v = _dot(xn, wv_ref[...])
    v_ref[...] = v
    vbf_ref[...] = v.astype(BF16)
    misc = _rope(_dot(xn, wm_ref[...]), ia_ref[...], ib_ref[...], ic_ref[...],
                 IDX_DIM // ROT_DIV // 2)
    misc_ref[...] = misc
    lane = lax.broadcasted_iota(I32, misc.shape, 1)
    ik_lo = jnp.where(lane < IDX_DIM, misc, 0.0)
    ika_ref[...] = ik_lo.astype(BF16)
    ikb_ref[...] = pltpu.roll(ik_lo, IDX_DIM, axis=1).astype(BF16)


def _kv_proj(u, w_all, ktabs, itabs, *, k_col, v_col, misc_col, dkv, tm):
    m, d = u.shape
    assert k_col % dkv == 0 and v_col % dkv == 0 and misc_col % LANES == 0
    tspec = pl.BlockSpec((tm, LANES), lambda i: (i, 0))
    kvspec = pl.BlockSpec((tm, dkv), lambda i: (i, 0))
    mspec = pl.BlockSpec((tm, LANES), lambda i: (i, 0))
    return pl.pallas_call(
        functools.partial(_kv_proj_kernel, dkv=dkv),
        grid=(m // tm,),
        in_specs=[
            pl.BlockSpec((tm, d), lambda i: (i, 0)),
            pl.BlockSpec((d, dkv), lambda i: (0, k_col // dkv)),
            pl.BlockSpec((d, dkv), lambda i: (0, v_col // dkv)),
            pl.BlockSpec((d, LANES), lambda i: (0, misc_col // LANES)),
            tspec, tspec, tspec, tspec, tspec, tspec,
        ],
        out_specs=[kvspec, kvspec, mspec, kvspec, kvspec, mspec, mspec],
        out_shape=[
            jax.ShapeDtypeStruct((m, dkv), F32), jax.ShapeDtypeStruct((m, dkv), F32),
            jax.ShapeDtypeStruct((m, LANES), F32),
            jax.ShapeDtypeStruct((m, dkv), BF16), jax.ShapeDtypeStruct((m, dkv), BF16),
            jax.ShapeDtypeStruct((m, LANES), BF16), jax.ShapeDtypeStruct((m, LANES), BF16),
        ],
        compiler_params=_cparams(("arbitrary",)),
        name="kv_proj",
    )(u, w_all, w_all, w_all, *ktabs, *itabs)


def _gate_proj_kernel(u_ref, wc_ref, wa_ref, gc_ref, ga_ref):
    xn = u_ref[...]
    gc_ref[...] = _sigmoid(_dot(xn, wc_ref[...])).astype(BF16)
    ga_ref[...] = _sigmoid(_dot(xn, wa_ref[...])).astype(BF16)


def _gate_proj(u, w_gates, *, tm, tn):
    m, d = u.shape
    n = w_gates.shape[1] // 2
    nb = n // tn
    ospec = pl.BlockSpec((tm, tn), lambda i, j: (i, j))
    return pl.pallas_call(
        _gate_proj_kernel,
        grid=(m // tm, nb),
        in_specs=[
            pl.BlockSpec((tm, d), lambda i, j: (i, 0)),
            pl.BlockSpec((d, tn), lambda i, j: (0, j)),
            pl.BlockSpec((d, tn), lambda i, j: (0, nb + j)),
        ],
        out_specs=[ospec, ospec],
        out_shape=[jax.ShapeDtypeStruct((m, n), BF16), jax.ShapeDtypeStruct((m, n), BF16)],
        compiler_params=_cparams(("arbitrary", "arbitrary")),
        name="gate_proj",
    )(u, w_gates, w_gates)


def _ordered_key(score):
    bits = pltpu.bitcast(score, I32)
    return bits ^ ((bits >> 31) & INT_MAX)


def _kth_largest(count_ge, k, like):
    def body(b, lo):
        cand = lo + jnp.left_shift(jnp.int32(1), 31 - b)
        return jnp.where(count_ge(cand) >= k, cand, lo)
    return lax.fori_loop(0, 32, body, jnp.full_like(like, INT_MIN))


def _tie_limit(count_eq_before, need, nbits, like):
    def body(b, lo):
        cand = lo + jnp.left_shift(jnp.int32(1), nbits - 1 - b)
        return jnp.where(count_eq_before(cand) < need, cand, lo)
    return lax.fori_loop(0, nbits, body, jnp.zeros_like(like))


def _prompt_attn_kernel(iq_ref, wt_ref, q_ref, ika_ref, ikb_ref, k_ref, vt_ref, o_ref,
                        key_ref, half_ref, lim_ref, q4_ref, m_ref, acc_ref, lga_ref, lgb_ref,
                        *, tq, sc, topk, n_rep, hd, hb):
    i = pl.program_id(0)
    t0 = i * tq
    n_chunks = (t0 + tq + sc - 1) // sc
    n_pairs = iq_ref.shape[1] // LANES
    n_kv = k_ref.shape[1] // hd
    pos_bits = int(k_ref.shape[0] - 1).bit_length()
    i16_min = -2 ** 15

    def rows(c):
        return pl.ds(pl.multiple_of(c * sc, sc), sc)

    def score_chunk(c, carry):
        ka = ika_ref[rows(c), :]
        kb = ikb_ref[rows(c), :]
        acc = jnp.zeros((sc, tq), F32)
        for p in range(n_pairs):
            iq_p = iq_ref[:, p * LANES:(p + 1) * LANES]
            acc += jnp.maximum(_nt_dot(ka, iq_p), 0.0) * wt_ref[2 * p:2 * p + 1, :]
            acc += jnp.maximum(_nt_dot(kb, iq_p), 0.0) * wt_ref[2 * p + 1:2 * p + 2, :]
        spos = c * sc + lax.broadcasted_iota(I32, (sc, tq), 0)
        tpos = t0 + lax.broadcasted_iota(I32, (sc, tq), 1)
        key = _ordered_key(jnp.where(spos <= tpos, acc, -jnp.inf))
        key_ref[rows(c), :] = key
        half_ref[rows(c), :] = (key >> 16).astype(jnp.int16)
        return carry

    lax.fori_loop(0, n_chunks, score_chunk, 0)

    def half_count(cand, strict=False):
        cand16 = cand.astype(jnp.int16)

        def hits(c):
            blk = half_ref[rows(c), :]
            hit = jnp.where(blk > cand16 if strict else blk >= cand16,
                            jnp.bfloat16(1), jnp.bfloat16(0))
            part = hit[0:16]
            for r in range(1, sc // 16):
                part = part + hit[r * 16:(r + 1) * 16]
            return part

        def body2(c2, cnt):
            return cnt + (hits(2 * c2) + hits(2 * c2 + 1)).astype(F32)

        def body1(c, cnt):
            return cnt + hits(c).astype(F32)

        n2 = n_chunks // 2
        cnt = lax.fori_loop(0, n2, body2, jnp.zeros((16, tq), F32))
        cnt = lax.fori_loop(2 * n2, n_chunks, body1, cnt)
        return cnt.sum(axis=0, keepdims=True).astype(I32)

    def half_search(k):
        def body(b, carry):
            lo, cnt_lo = carry
            cand = lo + jnp.left_shift(jnp.int32(1), 15 - b)
            cnt = half_count(cand)
            ok = cnt >= k
            return jnp.where(ok, cand, lo), jnp.where(ok, cnt, cnt_lo)
        init = (jnp.full((1, tq), i16_min, I32), jnp.full((1, tq), n_chunks * sc, I32))
        return lax.fori_loop(0, 16, body, init)

    thr_hi, _ = half_search(topk)
    n_gt_hi = half_count(thr_hi, strict=True)

    def low_chunk(c, carry):
        key = key_ref[rows(c), :]
        low = (key & 0xFFFF) - 2 ** 15
        half_ref[rows(c), :] = jnp.where((key >> 16) == thr_hi, low, i16_min).astype(jnp.int16)
        return carry

    lax.fori_loop(0, n_chunks, low_chunk, 0)
    thr_lo, n_ge_lo = half_search(topk - n_gt_hi)
    thr = thr_hi * 65536 + (thr_lo + 2 ** 15)
    n_ge = n_gt_hi + n_ge_lo
    has_tie = jnp.max(jnp.where((n_ge > topk) & (thr > NEG_INF_KEY), 1.0, 0.0)) > 0.0
    lim_ref[...] = jnp.full(lim_ref.shape, INT_MAX, I32)

    @pl.when(has_tie)
    def _():
        def column_count(pred):
            def body(c, cnt):
                blk = key_ref[rows(c), :]
                spos = c * sc + lax.broadcasted_iota(I32, (sc, tq), 0)
                hit = jnp.where(pred(blk, spos), 1, 0).astype(I32)
                return cnt + hit.reshape(sc // 8, 8, tq).sum(axis=0)
            cnt = lax.fori_loop(0, n_chunks, body, jnp.zeros((8, tq), I32))
            return cnt.astype(F32).sum(axis=0, keepdims=True).astype(I32)

        need = topk - column_count(lambda blk, spos: blk > thr)
        lim = _tie_limit(
            lambda p: column_count(lambda blk, spos: (blk == thr) & (spos < p)),
            need, pos_bits + 1, jnp.zeros((1, tq), I32))
        lim_ref[...] = jnp.broadcast_to(lim, lim_ref.shape)

    lim = lim_ref[0:1, :]

    def bias_chunk(c, carry):
        blk = key_ref[rows(c), :]
        spos = c * sc + lax.broadcasted_iota(I32, (sc, tq), 0)
        sel = ((blk > thr) | ((blk == thr) & (spos <= lim))) & (blk > NEG_INF_KEY)
        key_ref[rows(c), :] = pltpu.bitcast(jnp.where(sel, 0.0, NEG_BIAS).astype(F32), I32)
        return carry

    lax.fori_loop(0, n_chunks, bias_chunk, 0)

    n_heads = n_kv * n_rep
    n_grp = n_heads // hb
    for j in range(n_grp):
        for r in range(hb):
            head = j * hb + r
            q4_ref[j, r * tq:(r + 1) * tq, :] = q_ref[:, head * hd:(head + 1) * hd]
    m_ref[...] = jnp.full(m_ref.shape, NEG_BIAS, F32)
    acc_ref[...] = jnp.zeros(acc_ref.shape, F32)

    def bias_of(c):
        bias = pltpu.bitcast(key_ref[rows(c), :], F32)
        return jnp.concatenate([bias] * hb, axis=1)

    def qk_store(c, j, bias_w, dst_ref):
        g = j * hb // n_rep
        dst_ref[...] = _nt_dot(k_ref[rows(c), g * hd:(g + 1) * hd], q4_ref[j]) + bias_w

    bufs = (lga_ref, lgb_ref)
    qk_store(0, 0, bias_of(0), bufs[0])

    def attn_chunk(c, carry):
        bias_w = bias_of(c)
        c_next = jnp.minimum(c + 1, n_chunks - 1)
        for j in range(n_grp):
            cur, nxt = bufs[j % 2], bufs[(j + 1) % 2]
            if j + 1 < n_grp:
                qk_store(c, j + 1, bias_w, nxt)
            else:
                qk_store(c_next, 0, bias_of(c_next), nxt)
            logit = cur[...]
            m_old = m_ref[j]
            m_new = jnp.maximum(m_old, jnp.max(logit, axis=0, keepdims=True))
            p = jnp.exp2(logit - m_new).astype(BF16)
            acc_ref[j] = (jnp.exp2(m_old - m_new) * acc_ref[j]
                          + _dot(vt_ref[j * hb // n_rep, c], p))
            m_ref[j] = m_new
        return carry

    lax.fori_loop(0, n_chunks, attn_chunk, 0)
    for j in range(n_grp):
        for r in range(hb):
            head = j * hb + r
            cols = slice(r * tq, (r + 1) * tq)
            out_t = acc_ref[j, 0:hd, cols] / acc_ref[j, hd:hd + 1, cols]
            o_ref[:, head * hd:(head + 1) * hd] = out_t.T.astype(o_ref.dtype)


def _prompt_attention(iq, wt, q, ika, ikb, k_bf, vt, *, tq, sc, topk, hd):
    t, dq = q.shape
    n_kv = k_bf.shape[1] // hd
    n_rep = dq // hd // n_kv
    assert tq >= topk and t % tq == 0 and t % sc == 0 and sc % 16 == 0
    hb = 4
    assert (n_kv * n_rep // hb) % 2 == 0
    n_grp = n_kv * n_rep // hb
    resident = functools.partial(pl.BlockSpec, pipeline_mode=pl.Buffered(1))
    return pl.pallas_call(
        functools.partial(_prompt_attn_kernel, tq=tq, sc=sc, topk=topk, n_rep=n_rep, hd=hd, hb=hb),
        grid=(t // tq,),
        in_specs=[
            pl.BlockSpec((tq, iq.shape[1]), lambda i: (i, 0)),
            pl.BlockSpec((wt.shape[0], tq), lambda i: (0, i)),
            pl.BlockSpec((tq, dq), lambda i: (i, 0)),
            resident(ika.shape, lambda i: (0, 0)),
            resident(ikb.shape, lambda i: (0, 0)),
            resident(k_bf.shape, lambda i: (0, 0)),
            resident(vt.shape, lambda i: (0, 0, 0, 0)),
        ],
        out_specs=pl.BlockSpec((tq, dq), lambda i: (i, 0)),
        out_shape=jax.ShapeDtypeStruct((t, dq), BF16),
        scratch_shapes=[
            pltpu.VMEM((t, tq), I32),
            pltpu.VMEM((t, tq), jnp.int16),
            pltpu.VMEM((8, tq), I32),
            pltpu.VMEM((n_grp, hb * tq, hd), BF16),
            pltpu.VMEM((n_grp, 1, hb * tq), F32),
            pltpu.VMEM((n_grp, vt.shape[2], hb * tq), F32),
            pltpu.VMEM((sc, hb * tq), F32),
            pltpu.VMEM((sc, hb * tq), F32),
        ],
        compiler_params=_cparams(("arbitrary",)),
        name="prompt_attention",
    )(iq, wt, q, ika, ikb, k_bf, vt)


def _sample_score_kernel(pt_ref, iq_ref, w_ref, ikn_ref, cache_ref, o_ref, buf_ref, sem_ref,
                         *, n_pages, page):
    b = pl.program_id(0)
    nb = pl.num_programs(0)

    def page_copy(bb, slot, p):
        return pltpu.make_async_copy(cache_ref.at[pt_ref[bb, p]], buf_ref.at[slot, p],
                                     sem_ref.at[slot])

    def start_all(bb, slot):
        def body(p, carry):
            page_copy(bb, slot, p).start()
            return carry
        lax.fori_loop(0, n_pages, body, 0)

    slot = b % 2

    @pl.when(b == 0)
    def _():
        start_all(b, slot)

    @pl.when(b + 1 < nb)
    def _():
        start_all(b + 1, 1 - slot)

    pltpu.make_async_copy(cache_ref.at[pl.ds(0, n_pages)], buf_ref.at[slot], sem_ref.at[slot]).wait()

    iq = iq_ref[0]
    w = w_ref[0]

    def head_sum(keys_t):
        s = jnp.maximum(_dot(iq, keys_t), 0.0) * w
        return jnp.sum(s, axis=0, keepdims=True)

    group = min(32, n_pages)

    def group_body(gi, carry):
        p0 = pl.multiple_of(gi * group, group)
        blk = buf_ref[slot, pl.ds(p0, group)]
        keys_t = jnp.concatenate([blk[r] for r in range(group)], axis=1).astype(BF16)
        s = head_sum(keys_t)
        for r in range(group):
            o_ref[0, pl.ds(p0 + r, 1), :] = s[:, r * page:(r + 1) * page]
        return carry
    lax.fori_loop(0, n_pages // group, group_body, 0)
    own = head_sum(jnp.broadcast_to(ikn_ref[0], (ikn_ref.shape[1], page)))
    lane = lax.broadcasted_iota(I32, (1, page), 1)
    o_ref[0, pl.ds(n_pages, 1), :] = jnp.where(lane == 0, own, -jnp.inf)
    o_ref[0, pl.ds(n_pages + 1, 7), :] = jnp.full((7, page), -jnp.inf, F32)


def _sample_scores(page_table, iq, w, ik_new, cache_ik_t):
    bs, n_pages = page_table.shape
    idim, page = cache_ik_t.shape[1:]
    rows = n_pages + 8
    grid_spec = pltpu.PrefetchScalarGridSpec(
        num_scalar_prefetch=1,
        grid=(bs,),
        in_specs=[
            pl.BlockSpec((1,) + iq.shape[1:], lambda b, pt: (b, 0, 0)),
            pl.BlockSpec((1,) + w.shape[1:], lambda b, pt: (b, 0, 0)),
            pl.BlockSpec((1,) + ik_new.shape[1:], lambda b, pt: (b, 0, 0)),
            pl.BlockSpec(memory_space=pl.ANY),
        ],
        out_specs=pl.BlockSpec((1, rows, page), lambda b, pt: (b, 0, 0)),
        scratch_shapes=[pltpu.VMEM((2, n_pages, idim, page), F32),
                        pltpu.SemaphoreType.DMA((2,))],
    )
    return pl.pallas_call(
        functools.partial(_sample_score_kernel, n_pages=n_pages, page=page),
        grid_spec=grid_spec,
        out_shape=jax.ShapeDtypeStruct((bs, rows, page), F32),
        compiler_params=_cparams(("arbitrary",)),
        name="sample_scores",
    )(page_table, iq, w, ik_new, cache_ik_t)


def _slab_pos(shape):
    nd = len(shape)
    return (lax.broadcasted_iota(I32, shape, nd - 2) * shape[-1]
            + lax.broadcasted_iota(I32, shape, nd - 1))


def _sample_thresh_kernel(s_ref, thr_ref, lim_ref, *, topk):
    key = _ordered_key(s_ref[...])
    bs, rows, page = key.shape
    pos = _slab_pos(key.shape)

    def count(pred):
        c = jnp.sum(jnp.where(pred, 1.0, 0.0), axis=1, keepdims=True)
        return jnp.sum(c, axis=2, keepdims=True).astype(I32)

    like = jnp.zeros((bs, 1, 1), I32)
    thr = _kth_largest(lambda v: count(key >= v), topk, like)
    need = topk - count(key > thr)
    lim = _tie_limit(lambda p: count((key == thr) & (pos < p)), need,
                     int(rows * page - 1).bit_length() + 1, like)
    thr_ref[...] = thr
    lim_ref[...] = lim


def _sample_compact_kernel(s_ref, thr_ref, lim_ref, idx_ref, *, topk):
    key = _ordered_key(s_ref[0])
    rows, page = key.shape
    kpad = 256
    thr, lim = thr_ref[0], lim_ref[0]
    sel = ((key > thr) | ((key == thr) & (_slab_pos(key.shape) <= lim))) & (key > NEG_INF_KEY)
    self32 = jnp.where(sel, 1.0, 0.0)
    zpad = jnp.zeros((kpad - rows, page), F32)

    def ones_where(pred):
        return jnp.where(pred, 1.0, 0.0).astype(BF16)

    upper = lax.broadcasted_iota(I32, (page, page), 0) < lax.broadcasted_iota(I32, (page, page), 1)
    within = _dot(self32.astype(BF16), ones_where(upper))
    tot = jnp.broadcast_to(jnp.sum(self32, axis=1, keepdims=True), (rows, page))
    tot = jnp.concatenate([tot, zpad], axis=0)
    earlier = lax.broadcasted_iota(I32, (kpad, kpad), 1) < lax.broadcasted_iota(I32, (kpad, kpad), 0)
    before = _dot(ones_where(earlier), tot.astype(BF16))

    before_row = before.T[0:1, :]
    ends_row = (before + tot).T[0:1, :]
    slot_r = lax.broadcasted_iota(I32, (topk, kpad), 0).astype(F32)
    owner = ones_where((before_row <= slot_r) & (slot_r < ends_row))
    code = jnp.concatenate([jnp.where(sel, within, -1.0), zpad], axis=0)
    row_id = lax.broadcasted_iota(I32, (kpad, page), 0).astype(F32)
    fetched = _dot(owner, jnp.concatenate([code, before, row_id], axis=1).astype(BF16))
    code_g, before_g, row_g = (fetched[:, 0:page], fetched[:, page:2 * page],
                               fetched[:, 2 * page:3 * page])
    slot = lax.broadcasted_iota(I32, (topk, page), 0).astype(F32)
    lane = lax.broadcasted_iota(I32, (topk, page), 1).astype(F32)
    total = (before + tot)[kpad - 1:kpad, :]
    hit = (code_g == slot - before_g) & (slot < total)
    lane_sel = jnp.max(jnp.where(hit, lane, -1.0), axis=1, keepdims=True)
    pos = jnp.where(lane_sel >= 0.0, row_g[:, 0:1] * page + lane_sel, -1.0)
    idx_ref[0] = pos.astype(I32)


def _sample_select(scores, *, topk):
    bs, rows, page = scores.shape
    assert rows % 8 == 0 and rows <= 256
    one = pl.BlockSpec((1, 1, 1), lambda b: (b, 0, 0))
    thr, lim = pl.pallas_call(
        functools.partial(_sample_thresh_kernel, topk=topk),
        out_shape=[jax.ShapeDtypeStruct((bs, 1, 1), I32)] * 2,
        compiler_params=pltpu.CompilerParams(vmem_limit_bytes=VMEM_LIMIT),
        name="sample_thresh",
    )(scores)
    return pl.pallas_call(
        functools.partial(_sample_compact_kernel, topk=topk),
        grid=(bs,),
        in_specs=[pl.BlockSpec((1, rows, page), lambda b: (b, 0, 0)), one, one],
        out_specs=pl.BlockSpec((1, topk, 1), lambda b: (b, 0, 0)),
        out_shape=jax.ShapeDtypeStruct((bs, topk, 1), I32),
        compiler_params=_cparams(("arbitrary",)),
        name="sample_compact",
    )(scores, thr, lim)


def _sample_attn_kernel(pt_ref, idx_ref, idxv_ref, q_ref, kn_ref, vn_ref, ck_ref, cv_ref, o_ref,
                        kbuf_ref, vbuf_ref, sem_ref, *, page, past, topk, n_rep):
    b = pl.program_id(0)
    nb = pl.num_programs(0)
    n_kv = kn_ref.shape[1]

    def row_copies(bb, slot, j):
        pos = jnp.clip(idx_ref[bb * topk + j], 0, past - 1)
        if page & (page - 1) == 0:
            page_no = lax.shift_right_logical(pos, page.bit_length() - 1)
            off = pos & (page - 1)
        else:
            page_no, off = lax.div(pos, page), lax.rem(pos, page)
        phys = pt_ref[bb * (past // page) + page_no]
        return (pltpu.make_async_copy(ck_ref.at[phys, off], kbuf_ref.at[slot, j], sem_ref.at[0, slot]),
                pltpu.make_async_copy(cv_ref.at[phys, off], vbuf_ref.at[slot, j], sem_ref.at[1, slot]))

    unroll = 8

    def start_all(bb, slot):
        def body(j8, carry):
            for u in range(unroll):
                ck, cv = row_copies(bb, slot, j8 * unroll + u)
                ck.start()
                cv.start()
            return carry
        lax.fori_loop(0, topk // unroll, body, 0)

    slot = b % 2

    @pl.when(b == 0)
    def _():
        start_all(b, slot)

    @pl.when(b + 1 < nb)
    def _():
        start_all(b + 1, 1 - slot)

    for h in range(topk // page):
        slab = pl.ds(h * page, page)
        pltpu.make_async_copy(ck_ref.at[0], kbuf_ref.at[slot, slab], sem_ref.at[0, slot]).wait()
        pltpu.make_async_copy(cv_ref.at[0], vbuf_ref.at[slot, slab], sem_ref.at[1, slot]).wait()

    pos = idxv_ref[0]
    bias = jnp.where((pos >= 0) & (pos < past), 0.0, NEG_BIAS)
    own_sel = jnp.max(jnp.where(pos == past, 1.0, 0.0), axis=1, keepdims=True)
    own_bias = jnp.where(own_sel > 0.0, 0.0, NEG_BIAS)
    k_all = kbuf_ref[slot].astype(BF16)
    v_all = vbuf_ref[slot].astype(BF16)
    for g in range(n_kv):
        rows = slice(g * n_rep, (g + 1) * n_rep)
        qg = q_ref[0, rows, :]
        kg = k_all[:, g, :]
        vg = v_all[:, g, :]
        kn = kn_ref[0, g:g + 1, :].astype(BF16).astype(F32)
        vn = vn_ref[0, g:g + 1, :].astype(BF16).astype(F32)
        logit = _nt_dot(qg, kg) + bias
        own = jnp.sum(qg.astype(F32) * kn, axis=1, keepdims=True) + own_bias
        m = jnp.maximum(jnp.max(logit, axis=1, keepdims=True), own)
        p = jnp.exp2(logit - m)
        p_own = jnp.exp2(own - m)
        denom = jnp.sum(p, axis=1, keepdims=True) + p_own
        num = _dot(p.astype(BF16), vg) + p_own.astype(BF16).astype(F32) * vn
        o_ref[0, rows, :] = num / denom


def _sample_attention(page_table, idx, q, k_new, v_new, cache_k, cache_v, *, topk):
    bs, n_pages = page_table.shape
    page, n_kv, hd = cache_k.shape[1:]
    n_heads = q.shape[1]
    past = n_pages * page
    assert topk % page == 0 and topk % 8 == 0
    grid_spec = pltpu.PrefetchScalarGridSpec(
        num_scalar_prefetch=2,
        grid=(bs,),
        in_specs=[
            pl.BlockSpec((1, 1, topk), lambda b, pt, ix: (b, 0, 0)),
            pl.BlockSpec((1, n_heads, hd), lambda b, pt, ix: (b, 0, 0)),
            pl.BlockSpec((1, n_kv, hd), lambda b, pt, ix: (b, 0, 0)),
            pl.BlockSpec((1, n_kv, hd), lambda b, pt, ix: (b, 0, 0)),
            pl.BlockSpec(memory_space=pl.ANY),
            pl.BlockSpec(memory_space=pl.ANY),
        ],
        out_specs=pl.BlockSpec((1, n_heads, hd), lambda b, pt, ix: (b, 0, 0)),
        scratch_shapes=[pltpu.VMEM((2, topk, n_kv, hd), F32),
                        pltpu.VMEM((2, topk, n_kv, hd), F32),
                        pltpu.SemaphoreType.DMA((2, 2))],
    )
    return pl.pallas_call(
        functools.partial(_sample_attn_kernel, page=page, past=past, topk=topk,
                          n_rep=n_heads // n_kv),
        grid_spec=grid_spec,
        out_shape=jax.ShapeDtypeStruct((bs, n_heads, hd), F32),
        compiler_params=_cparams(("arbitrary",)),
        name="sample_attention",
    )(page_table.reshape(-1), idx.reshape(-1), idx.reshape(bs, 1, topk), q, k_new, v_new,
      cache_k, cache_v)


def _rope_tables(pos, dim, pad_lanes=0):
    rot = dim // ROT_DIV
    half = rot // 2
    inv = ROPE_THETA ** (-np.arange(half, dtype=np.float64) / half)
    ang = np.asarray(pos, np.float64)[:, None] * inv[None, :]
    cos, sin = jnp.asarray(np.cos(ang), F32), jnp.asarray(np.sin(ang), F32)
    width = dim + pad_lanes
    a = jnp.pad(jnp.concatenate([cos, cos], axis=1), ((0, 0), (0, width - rot)), constant_values=1.0)
    b = jnp.pad(-sin, ((0, 0), (0, width - half)))
    c = jnp.pad(sin, ((0, 0), (half, width - rot)))
    reps = LANES // width
    return tuple(jnp.tile(t, (1, reps)) if reps > 1 else t for t in (a, b, c))


def _in_proj_columns(d_model, d_conv, dq, dkv):
    names = ("x", "b", "c", "q", "k", "v", "iq", "ik", "iw", "gc", "ga")
    sizes = (d_conv, d_conv, d_conv, dq, dkv, dkv, IDX_HEADS * IDX_DIM, IDX_DIM, IDX_HEADS,
             d_model, d_model)
    starts = np.concatenate([[0], np.cumsum(sizes)[:-1]])
    return {n: int(s) for n, s in zip(names, starts)}


def _mixer_common(u, w_all, w_gates, cols, ktabs, itabs, *, hd, dq, dkv, tm, tn):
    q = _rope_proj(u, w_all, ktabs, first_col=cols["q"], n=dq, half=hd // ROT_DIV // 2,
                   scale=hd ** -0.5 * LOG2E, tm=tm, tn=tn)
    iq = _rope_proj(u, w_all, itabs[0], first_col=cols["iq"], n=IDX_HEADS * IDX_DIM,
                    half=IDX_DIM // ROT_DIV // 2, scale=IDX_DIM ** -0.5, tm=tm, tn=tn)
    assert cols["iw"] == cols["ik"] + IDX_DIM
    k, v, misc, k_bf, v_bf, ika, ikb = _kv_proj(
        u, w_all, ktabs, itabs[1], k_col=cols["k"], v_col=cols["v"], misc_col=cols["ik"],
        dkv=dkv, tm=tm)
    gc, ga = _gate_proj(u, w_gates, tm=tm, tn=tn)
    return q, iq, k, v, misc, k_bf, v_bf, ika, ikb, gc, ga


def kernel(x_prompt, x_sample, cache_k, cache_v, cache_idx_k, state_conv, page_table,
           norm_ffn1_pre, norm_ffn1_post, w_ffn1_gate_up, w_ffn1_down,
           norm_mix_pre, norm_mix_post, w_in, w_conv, w_conv_out, w_attn_out, w_out,
           norm_ffn2_pre, norm_ffn2_post, w_ffn2_gate_up, w_ffn2_down):
    bp, t, d = x_prompt.shape
    bs, ts, _ = x_sample.shape
    depth = w_in.shape[0]
    page, n_kv, hd = cache_k.shape[2:]
    n_pages = page_table.shape[1]
    past = n_pages * page
    d_conv = w_conv.shape[2]
    dq = w_attn_out.shape[1]
    dkv = n_kv * hd
    assert bp == 1 and ts == 1 and dq == N_HEADS * hd and n_kv == N_KV_HEADS

    tm = min(512, t)
    tm_proj = min(1024, t)
    tf = 512
    tn = 1024
    tq, sc = min(256, t), min(512, t)
    cols = _in_proj_columns(d, d_conv, dq, dkv)
    topk_p = min(TOPK_MAX, t // 4)
    topk_s = min(TOPK_MAX, (past + ts) // 4)

    pos_p = np.arange(t)
    pos_s = np.full((bs,), past)
    ktabs_p, ktabs_s = _rope_tables(pos_p, hd), _rope_tables(pos_s, hd)
    itabs_p = (_rope_tables(pos_p, IDX_DIM), _rope_tables(pos_p, IDX_DIM, LANES - IDX_DIM))
    itabs_s = (_rope_tables(pos_s, IDX_DIM), _rope_tables(pos_s, IDX_DIM, LANES - IDX_DIM))

    hp = x_prompt.reshape(t, d)
    hs = x_sample.reshape(bs, d)
    outs = [[] for _ in range(8)]
    row = lambda a: a.reshape(1, -1)
    for l in range(depth):
        w1gu, w1d = w_ffn1_gate_up[l].astype(BF16), w_ffn1_down[l].astype(BF16)
        w2gu, w2d = w_ffn2_gate_up[l].astype(BF16), w_ffn2_down[l].astype(BF16)
        w_all, w_gates = w_in[l].astype(BF16), w_in[l][:, cols["gc"]:].astype(BF16)
        w_co, w_ao, w_o = (w_conv_out[l].astype(BF16), w_attn_out[l].astype(BF16),
                           w_out[l].astype(BF16))
        gain_mix = row(norm_mix_pre[l])

        hp, up = _ffn_half(hp, row(norm_ffn1_pre[l]), row(norm_ffn1_post[l]), w1gu, w1d,
                           tm=tm, tf=tf, next_gain=gain_mix)
        conv_y, tail = _conv_seq(up, w_all, w_conv[l], tm=tm, tn=tn)
        q, iq, k, v, misc, k_bf, v_bf, ika, ikb, gc, ga = _mixer_common(
            up, w_all, w_gates, cols, ktabs_p, itabs_p, hd=hd, dq=dq, dkv=dkv, tm=tm_proj, tn=tn)
        wt = (misc[:, IDX_DIM:IDX_DIM + IDX_HEADS] * IDX_HEADS ** -0.5).T
        vt = v_bf.reshape(t // sc, sc, n_kv, hd).transpose(2, 0, 3, 1)
        vt = jnp.concatenate([vt, jnp.ones((n_kv, t // sc, 16, sc), BF16)], axis=2)
        attn_o = _prompt_attention(iq, wt, q, ika, ikb, k_bf, vt, tq=tq, sc=sc, topk=topk_p, hd=hd)
        hp = _merge(hp, conv_y, attn_o, gc, ga, row(norm_mix_post[l]), w_co, w_ao, w_o,
                    tm=tm, tn=tn // 2)
        outs[0].append(k.reshape(bp, t // page, page, n_kv, hd))
        outs[1].append(v.reshape(bp, t // page, page, n_kv, hd))
        outs[2].append(misc[:, :IDX_DIM].reshape(bp, t // page, page, IDX_DIM))
        outs[3].append(tail[tail.shape[0] - (CONV_WIDTH - 1):].reshape(bp, CONV_WIDTH - 1, d_conv))
        hp = _ffn_half(hp, row(norm_ffn2_pre[l]), row(norm_ffn2_post[l]), w2gu, w2d, tm=tm, tf=tf)

        hs, us = _ffn_half(hs, row(norm_ffn1_pre[l]), row(norm_ffn1_post[l]), w1gu, w1d,
                           tm=bs, tf=tf, next_gain=gain_mix)
        st = state_conv[l]
        conv_y, z = _conv_step(us, w_all, w_conv[l], st[:, 0, :], st[:, 1, :], tn=tn)
        q, iq, k, v, misc, k_bf, v_bf, ika, ikb, gc, ga = _mixer_common(
            us, w_all, w_gates, cols, ktabs_s, itabs_s, hd=hd, dq=dq, dkv=dkv, tm=bs, tn=tn)
        w_idx = (misc[:, IDX_DIM:IDX_DIM + IDX_HEADS] * IDX_HEADS ** -0.5).reshape(bs, IDX_HEADS, 1)
        scores = _sample_scores(page_table, iq.reshape(bs, IDX_HEADS, IDX_DIM), w_idx,
                                ika[:, :IDX_DIM].reshape(bs, IDX_DIM, 1),
                                cache_idx_k[l].transpose(0, 2, 1))
        idx = _sample_select(scores, topk=topk_s).reshape(bs, topk_s)
        attn_o = _sample_attention(page_table, idx, q.reshape(bs, N_HEADS, hd),
                                   k.reshape(bs, n_kv, hd), v.reshape(bs, n_kv, hd),
                                   cache_k[l], cache_v[l], topk=topk_s)
        hs = _merge(hs, conv_y, attn_o.reshape(bs, dq).astype(BF16), gc, ga, row(norm_mix_post[l]),
                    w_co, w_ao, w_o, tm=bs, tn=tn // 2)
        outs[4].append(k.reshape(bs, ts, n_kv, hd))
        outs[5].append(v.reshape(bs, ts, n_kv, hd))
        outs[6].append(misc[:, :IDX_DIM].reshape(bs, ts, IDX_DIM))
        outs[7].append(jnp.stack([st[:, 1, :], z], axis=1))
        hs = _ffn_half(hs, row(norm_ffn2_pre[l]), row(norm_ffn2_post[l]), w2gu, w2d, tm=bs, tf=tf)

    return (hp.reshape(bp, t, d), hs.reshape(bs, ts, d)) + tuple(jnp.stack(o) for o in outs)
```

```python
import functools

import numpy as np
import jax
import jax.numpy as jnp
from jax import lax
from jax.experimental import pallas as pl
from jax.experimental.pallas import tpu as pltpu

F32 = jnp.float32
BF16 = jnp.bfloat16
I32 = jnp.int32

N_HEADS = 16
N_KV_HEADS = 4
IDX_HEADS = 16
IDX_DIM = 64
TOPK_MAX = 256
CONV_WIDTH = 3
ROPE_THETA = 500000.0
ROT_DIV = 4
EPS = 1e-6

LANES = 128
VMEM_LIMIT = 56 * 1024 * 1024
LOG2E = 1.4426950408889634
NEG_BIAS = -1e30
INT_MIN = -2 ** 31
INT_MAX = 2 ** 31 - 1
NEG_INF_KEY = int(np.int32(np.uint32(0x807FFFFF)))


def _cparams(sem):
    return pltpu.CompilerParams(dimension_semantics=sem, vmem_limit_bytes=VMEM_LIMIT)


def _rms_scale(x):
    return lax.rsqrt(jnp.mean(x * x, axis=-1, keepdims=True) + EPS)


def _nt_dot(a, b):
    return lax.dot_general(a, b, (((1,), (1,)), ((), ())), preferred_element_type=F32)


def _dot(a, b):
    return jnp.dot(a, b, preferred_element_type=F32)


def _sigmoid(x):
    return 0.5 * jnp.tanh(0.5 * x) + 0.5


def _ffn_accumulate(x_ref, pre_ref, wg_ref, wu_ref, wd_ref, xn_ref, acc_ref):
    @pl.when(pl.program_id(1) == 0)
    def _():
        x = x_ref[...]
        xn_ref[...] = (x * _rms_scale(x) * pre_ref[...]).astype(BF16)
        acc_ref[...] = jnp.zeros_like(acc_ref)

    xn = xn_ref[...]
    g = _dot(xn, wg_ref[...])
    u = _dot(xn, wu_ref[...])
    h = (g * _sigmoid(g) * u).astype(BF16)
    acc_ref[...] += _dot(h, wd_ref[...])


def _ffn_result(x_ref, post_ref, acc_ref):
    y = acc_ref[...]
    return x_ref[...] + 0.5 * (y * _rms_scale(y) * post_ref[...])


def _ffn_kernel(x_ref, pre_ref, post_ref, wg_ref, wu_ref, wd_ref, o_ref, xn_ref, acc_ref):
    _ffn_accumulate(x_ref, pre_ref, wg_ref, wu_ref, wd_ref, xn_ref, acc_ref)

    @pl.when(pl.program_id(1) == pl.num_programs(1) - 1)
    def _():
        o_ref[...] = _ffn_result(x_ref, post_ref, acc_ref)


def _ffn_norm_kernel(x_ref, pre_ref, post_ref, ng_ref, wg_ref, wu_ref, wd_ref, o_ref, u_ref,
                     xn_ref, acc_ref):
    _ffn_accumulate(x_ref, pre_ref, wg_ref, wu_ref, wd_ref, xn_ref, acc_ref)

    @pl.when(pl.program_id(1) == pl.num_programs(1) - 1)
    def _():
        out = _ffn_result(x_ref, post_ref, acc_ref)
        o_ref[...] = out
        u_ref[...] = (out * _rms_scale(out) * ng_ref[...]).astype(BF16)


def _ffn_half(x, pre, post, w_gu, w_d, *, tm, tf, next_gain=None):
    m, d = x.shape
    f = w_d.shape[0]
    nf = f // tf
    row_spec = pl.BlockSpec((tm, d), lambda i, j: (i, 0))
    vec_spec = pl.BlockSpec((1, d), lambda i, j: (0, 0))
    w_specs = [
        pl.BlockSpec((d, tf), lambda i, j: (0, j)),
        pl.BlockSpec((d, tf), lambda i, j: (0, nf + j)),
        pl.BlockSpec((tf, d), lambda i, j: (j, 0)),
    ]
    common = dict(
        grid=(m // tm, nf),
        scratch_shapes=[pltpu.VMEM((tm, d), BF16), pltpu.VMEM((tm, d), F32)],
        compiler_params=_cparams(("arbitrary", "arbitrary")),
    )
    if next_gain is None:
        return pl.pallas_call(
            _ffn_kernel,
            in_specs=[row_spec, vec_spec, vec_spec] + w_specs,
            out_specs=row_spec,
            out_shape=jax.ShapeDtypeStruct((m, d), F32),
            name="ffn_half", **common,
        )(x, pre, post, w_gu, w_gu, w_d)
    return pl.pallas_call(
        _ffn_norm_kernel,
        in_specs=[row_spec, vec_spec, vec_spec, vec_spec] + w_specs,
        out_specs=[row_spec, row_spec],
        out_shape=[jax.ShapeDtypeStruct((m, d), F32), jax.ShapeDtypeStruct((m, d), BF16)],
        name="ffn_half_norm", **common,
    )(x, pre, post, next_gain, w_gu, w_gu, w_d)


def _merge_kernel(h_ref, cy_ref, ao_ref, gc_ref, ga_ref, post_ref, wc_ref, wa_ref, wo_ref,
                  o_ref, mix_ref):
    j = pl.program_id(1)
    nj = mix_ref.shape[0]
    mc = _dot(cy_ref[...], wc_ref[...])
    ma = _dot(ao_ref[...], wa_ref[...])
    mix_ref[j] = (gc_ref[...].astype(F32) * mc + ga_ref[...].astype(F32) * ma).astype(BF16)

    @pl.when(j == nj - 1)
    def _():
        mix = jnp.concatenate([mix_ref[jj] for jj in range(nj)], axis=1)
        y = _dot(mix, wo_ref[...])
        o_ref[...] = h_ref[...] + y * _rms_scale(y) * post_ref[...]


def _merge(h, conv_y, attn_o, gc, ga, post, w_co, w_ao, w_out, *, tm, tn):
    m, d = h.shape
    return pl.pallas_call(
        _merge_kernel,
        grid=(m // tm, d // tn),
        in_specs=[
            pl.BlockSpec((tm, d), lambda i, j: (i, 0)),
            pl.BlockSpec((tm, d), lambda i, j: (i, 0)),
            pl.BlockSpec((tm, d), lambda i, j: (i, 0)),
            pl.BlockSpec((tm, tn), lambda i, j: (i, j)),
            pl.BlockSpec((tm, tn), lambda i, j: (i, j)),
            pl.BlockSpec((1, d), lambda i, j: (0, 0)),
            pl.BlockSpec((d, tn), lambda i, j: (0, j)),
            pl.BlockSpec((d, tn), lambda i, j: (0, j)),
            pl.BlockSpec((d, d), lambda i, j: (0, 0), pipeline_mode=pl.Buffered(1)),
        ],
        out_specs=pl.BlockSpec((tm, d), lambda i, j: (i, 0)),
        out_shape=jax.ShapeDtypeStruct((m, d), F32),
        scratch_shapes=[pltpu.VMEM((d // tn, tm, tn), BF16)],
        compiler_params=_cparams(("arbitrary", "arbitrary")),
        name="merge",
    )(h, conv_y, attn_o, gc, ga, post, w_co, w_ao, w_out)


def _rope(x, a, b, c, half):
    return x * a + pltpu.roll(x, LANES - half, axis=1) * b + pltpu.roll(x, half, axis=1) * c


def _conv_seq_kernel(u_ref, wx_ref, wb_ref, wc_ref, wconv_ref, cy_ref, tail_ref, carry_ref):
    i = pl.program_id(0)
    j = pl.program_id(1)
    xn = u_ref[...]
    z = _dot(xn, wc_ref[...]) * _dot(xn, wx_ref[...])
    b = _dot(xn, wb_ref[...])
    tm = z.shape[0]

    @pl.when(i == 0)
    def _():
        carry_ref[j] = jnp.zeros(carry_ref.shape[1:], F32)

    prev = carry_ref[j]
    row = lax.broadcasted_iota(I32, z.shape, 0)
    z1 = jnp.where(row == 0, prev[1:2, :], pltpu.roll(z, 1, axis=0))
    z2 = jnp.where(row == 0, prev[0:1, :],
                   jnp.where(row == 1, prev[1:2, :], pltpu.roll(z, 2, axis=0)))
    w = wconv_ref[...]
    cy_ref[...] = (b * (z2 * w[0:1, :] + z1 * w[1:2, :] + z * w[2:3, :])).astype(BF16)
    tail = z[tm - 8:, :]
    carry_ref[j] = pltpu.roll(tail, 2, axis=0)
    tail_ref[...] = tail


def _col_blocks(d, tn, first_col):
    assert first_col % tn == 0
    return pl.BlockSpec((d, tn), lambda i, j: (0, first_col // tn + j))


def _conv_seq(u, w_all, w_conv, *, tm, tn):
    m, d = u.shape
    dc = w_conv.shape[1]
    return pl.pallas_call(
        _conv_seq_kernel,
        grid=(m // tm, dc // tn),
        in_specs=[
            pl.BlockSpec((tm, d), lambda i, j: (i, 0)),
            _col_blocks(d, tn, 0), _col_blocks(d, tn, dc), _col_blocks(d, tn, 2 * dc),
            pl.BlockSpec((CONV_WIDTH, tn), lambda i, j: (0, j)),
        ],
        out_specs=[
            pl.BlockSpec((tm, tn), lambda i, j: (i, j)),
            pl.BlockSpec((8, tn), lambda i, j: (i, j)),
        ],
        out_shape=[jax.ShapeDtypeStruct((m, dc), BF16), jax.ShapeDtypeStruct((m // tm * 8, dc), F32)],
        scratch_shapes=[pltpu.VMEM((dc // tn, 8, tn), F32)],
        compiler_params=_cparams(("arbitrary", "arbitrary")),
        name="conv_seq",
    )(u, w_all, w_all, w_all, w_conv)


def _conv_step_kernel(u_ref, wx_ref, wb_ref, wc_ref, wconv_ref, s0_ref, s1_ref, cy_ref, z_ref):
    xn = u_ref[...]
    z = _dot(xn, wc_ref[...]) * _dot(xn, wx_ref[...])
    b = _dot(xn, wb_ref[...])
    w = wconv_ref[...]
    cy_ref[...] = (b * (s0_ref[...] * w[0:1, :] + s1_ref[...] * w[1:2, :] + z * w[2:3, :])
                   ).astype(BF16)
    z_ref[...] = z


def _conv_step(u, w_all, w_conv, s0, s1, *, tn):
    m, d = u.shape
    dc = w_conv.shape[1]
    cspec = pl.BlockSpec((m, tn), lambda i, j: (0, j))
    return pl.pallas_call(
        _conv_step_kernel,
        grid=(1, dc // tn),
        in_specs=[
            pl.BlockSpec((m, d), lambda i, j: (0, 0)),
            _col_blocks(d, tn, 0), _col_blocks(d, tn, dc), _col_blocks(d, tn, 2 * dc),
            pl.BlockSpec((CONV_WIDTH, tn), lambda i, j: (0, j)),
            cspec, cspec,
        ],
        out_specs=[cspec, cspec],
        out_shape=[jax.ShapeDtypeStruct((m, dc), BF16), jax.ShapeDtypeStruct((m, dc), F32)],
        compiler_params=_cparams(("arbitrary", "arbitrary")),
        name="conv_step",
    )(u, w_all, w_all, w_all, w_conv, s0, s1)


def _rope_proj_kernel(u_ref, w_ref, a_ref, b_ref, c_ref, o_ref, *, half, scale):
    y = _dot(u_ref[...], w_ref[...])
    a, b, c = a_ref[...], b_ref[...], c_ref[...]
    for g in range(y.shape[1] // LANES):
        sl = slice(g * LANES, (g + 1) * LANES)
        o_ref[:, sl] = (_rope(y[:, sl], a, b, c, half) * scale).astype(o_ref.dtype)


def _rope_proj(u, w_all, tabs, *, first_col, n, half, scale, tm, tn):
    m, d = u.shape
    tspec = pl.BlockSpec((tm, LANES), lambda i, j: (i, 0))
    return pl.pallas_call(
        functools.partial(_rope_proj_kernel, half=half, scale=scale),
        grid=(m // tm, n // tn),
        in_specs=[
            pl.BlockSpec((tm, d), lambda i, j: (i, 0)),
            _col_blocks(d, tn, first_col),
            tspec, tspec, tspec,
        ],
        out_specs=pl.BlockSpec((tm, tn), lambda i, j: (i, j)),
        out_shape=jax.ShapeDtypeStruct((m, n), BF16),
        compiler_params=_cparams(("arbitrary", "arbitrary")),
        name="rope_proj",
    )(u, w_all, *tabs)


def _kv_proj_kernel(u_ref, wk_ref, wv_ref, wm_ref, ka_ref, kb_ref, kc_ref,
                    ia_ref, ib_ref, ic_ref,
                    k_ref, v_ref, misc_ref, kbf_ref, vbf_ref, ika_ref, ikb_ref, *, dkv):
    xn = u_ref[...]
    yk = _dot(xn, wk_ref[...])
    a, b, c = ka_ref[...], kb_ref[...], kc_ref[...]
    for g in range(dkv // LANES):
        sl = slice(g * LANES, (g + 1) * LANES)
        kg = _rope(yk[:, sl], a, b, c, LANES // ROT_DIV // 2)
        k_ref[:, sl] = kg
        kbf_ref[:, sl] = kg.astype(BF16)
    v = _dot(xn, wv_ref[...])
    v_ref[...] = v
    vbf_ref[...] = v.astype(BF16)
    misc = _rope(_dot(xn, wm_ref[...]), ia_ref[...], ib_ref[...], ic_ref[...],
                 IDX_DIM // ROT_DIV // 2)
    misc_ref[...] = misc
    lane = lax.broadcasted_iota(I32, misc.shape, 1)
    ik_lo = jnp.where(lane < IDX_DIM, misc, 0.0)
    ika_ref[...] = ik_lo.astype(BF16)
    ikb_ref[...] = pltpu.roll(ik_lo, IDX_DIM, axis=1).astype(BF16)


def _kv_proj(u, w_all, ktabs, itabs, *, k_col, v_col, misc_col, dkv, tm):
    m, d = u.shape
    assert k_col % dkv == 0 and v_col % dkv == 0 and misc_col % LANES == 0
    tspec = pl.BlockSpec((tm, LANES), lambda i: (i, 0))
    kvspec = pl.BlockSpec((tm, dkv), lambda i: (i, 0))
    mspec = pl.BlockSpec((tm, LANES), lambda i: (i, 0))
    return pl.pallas_call(
        functools.partial(_kv_proj_kernel, dkv=dkv),
        grid=(m // tm,),
        in_specs=[
            pl.BlockSpec((tm, d), lambda i: (i, 0)),
            pl.BlockSpec((d, dkv), lambda i: (0, k_col // dkv)),
            pl.BlockSpec((d, dkv), lambda i: (0, v_col // dkv)),
            pl.BlockSpec((d, LANES), lambda i: (0, misc_col // LANES)),
            tspec, tspec, tspec, tspec, tspec, tspec,
        ],
        out_specs=[kvspec, kvspec, mspec, kvspec, kvspec, mspec, mspec],
        out_shape=[
            jax.ShapeDtypeStruct((m, dkv), F32), jax.ShapeDtypeStruct((m, dkv), F32),
            jax.ShapeDtypeStruct((m, LANES), F32),
            jax.ShapeDtypeStruct((m, dkv), BF16), jax.ShapeDtypeStruct((m, dkv), BF16),
            jax.ShapeDtypeStruct((m, LANES), BF16), jax.ShapeDtypeStruct((m, LANES), BF16),
        ],
        compiler_params=_cparams(("arbitrary",)),
        name="kv_proj",
    )(u, w_all, w_all, w_all, *ktabs, *itabs)


def _gate_proj_kernel(u_ref, wc_ref, wa_ref, gc_ref, ga_ref):
    xn = u_ref[...]
    gc_ref[...] = _sigmoid(_dot(xn, wc_ref[...])).astype(BF16)
    ga_ref[...] = _sigmoid(_dot(xn, wa_ref[...])).astype(BF16)


def _gate_proj(u, w_gates, *, tm, tn):
    m, d = u.shape
    n = w_gates.shape[1] // 2
    nb = n // tn
    ospec = pl.BlockSpec((tm, tn), lambda i, j: (i, j))
    return pl.pallas_call(
        _gate_proj_kernel,
        grid=(m // tm, nb),
        in_specs=[
            pl.BlockSpec((tm, d), lambda i, j: (i, 0)),
            pl.BlockSpec((d, tn), lambda i, j: (0, j)),
            pl.BlockSpec((d, tn), lambda i, j: (0, nb + j)),
        ],
        out_specs=[ospec, ospec],
        out_shape=[jax.ShapeDtypeStruct((m, n), BF16), jax.ShapeDtypeStruct((m, n), BF16)],
        compiler_params=_cparams(("arbitrary", "arbitrary")),
        name="gate_proj",
    )(u, w_gates, w_gates)


def _ordered_key(score):
    bits = pltpu.bitcast(score, I32)
    return bits ^ ((bits >> 31) & INT_MAX)


def _kth_largest(count_ge, k, like):
    def body(b, lo):
        cand = lo + jnp.left_shift(jnp.int32(1), 31 - b)
        return jnp.where(count_ge(cand) >= k, cand, lo)
    return lax.fori_loop(0, 32, body, jnp.full_like(like, INT_MIN))


def _tie_limit(count_eq_before, need, nbits, like):
    def body(b, lo):
        cand = lo + jnp.left_shift(jnp.int32(1), nbits - 1 - b)
        return jnp.where(count_eq_before(cand) < need, cand, lo)
    return lax.fori_loop(0, nbits, body, jnp.zeros_like(like))


def _prompt_attn_kernel(iq_ref, wt_ref, q_ref, ika_ref, ikb_ref, k_ref, vt_ref, o_ref,
                        key_ref, half_ref, lim_ref, q4_ref, m_ref, acc_ref, lga_ref, lgb_ref,
                        *, tq, sc, topk, n_rep, hd, hb):
    i = pl.program_id(0)
    t0 = i * tq
    n_chunks = (t0 + tq + sc - 1) // sc
    n_pairs = iq_ref.shape[1] // LANES
    n_kv = k_ref.shape[1] // hd
    pos_bits = int(k_ref.shape[0] - 1).bit_length()
    i16_min = -2 ** 15

    def rows(c):
        return pl.ds(pl.multiple_of(c * sc, sc), sc)

    def score_chunk(c, carry):
        ka = ika_ref[rows(c), :]
        kb = ikb_ref[rows(c), :]
        acc = jnp.zeros((sc, tq), F32)
        for p in range(n_pairs):
            iq_p = iq_ref[:, p * LANES:(p + 1) * LANES]
            acc += jnp.maximum(_nt_dot(ka, iq_p), 0.0) * wt_ref[2 * p:2 * p + 1, :]
            acc += jnp.maximum(_nt_dot(kb, iq_p), 0.0) * wt_ref[2 * p + 1:2 * p + 2, :]
        spos = c * sc + lax.broadcasted_iota(I32, (sc, tq), 0)
        tpos = t0 + lax.broadcasted_iota(I32, (sc, tq), 1)
        key = _ordered_key(jnp.where(spos <= tpos, acc, -jnp.inf))
        key_ref[rows(c), :] = key
        half_ref[rows(c), :] = (key >> 16).astype(jnp.int16)
        return carry

    lax.fori_loop(0, n_chunks, score_chunk, 0)

    def half_count(cand, strict=False):
        cand16 = cand.astype(jnp.int16)

        def hits(c):
            blk = half_ref[rows(c), :]
            hit = jnp.where(blk > cand16 if strict else blk >= cand16,
                            jnp.bfloat16(1), jnp.bfloat16(0))
            part = hit[0:16]
            for r in range(1, sc // 16):
                part = part + hit[r * 16:(r + 1) * 16]
            return part

        def body2(c2, cnt):
            return cnt + (hits(2 * c2) + hits(2 * c2 + 1)).astype(F32)

        def body1(c, cnt):
            return cnt + hits(c).astype(F32)

        n2 = n_chunks // 2
        cnt = lax.fori_loop(0, n2, body2, jnp.zeros((16, tq), F32))
        cnt = lax.fori_loop(2 * n2, n_chunks, body1, cnt)
        return cnt.sum(axis=0, keepdims=True).astype(I32)

    def half_search(k):
        def body(b, carry):
            lo, cnt_lo = carry
            cand = lo + jnp.left_shift(jnp.int32(1), 15 - b)
            cnt = half_count(cand)
            ok = cnt >= k
            return jnp.where(ok, cand, lo), jnp.where(ok, cnt, cnt_lo)
        init = (jnp.full((1, tq), i16_min, I32), jnp.full((1, tq), n_chunks * sc, I32))
        return lax.fori_loop(0, 16, body, init)

    thr_hi, _ = half_search(topk)
    n_gt_hi = half_count(thr_hi, strict=True)

    def low_chunk(c, carry):
        key = key_ref[rows(c), :]
        low = (key & 0xFFFF) - 2 ** 15
        half_ref[rows(c), :] = jnp.where((key >> 16) == thr_hi, low, i16_min).astype(jnp.int16)
        return carry

    lax.fori_loop(0, n_chunks, low_chunk, 0)
    thr_lo, n_ge_lo = half_search(topk - n_gt_hi)
    thr = thr_hi * 65536 + (thr_lo + 2 ** 15)
    n_ge = n_gt_hi + n_ge_lo
    has_tie = jnp.max(jnp.where((n_ge > topk) & (thr > NEG_INF_KEY), 1.0, 0.0)) > 0.0
    lim_ref[...] = jnp.full(lim_ref.shape, INT_MAX, I32)

    @pl.when(has_tie)
    def _():
        def column_count(pred):
            def body(c, cnt):
                blk = key_ref[rows(c), :]
                spos = c * sc + lax.broadcasted_iota(I32, (sc, tq), 0)
                hit = jnp.where(pred(blk, spos), 1, 0).astype(I32)
                return cnt + hit.reshape(sc // 8, 8, tq).sum(axis=0)
            cnt = lax.fori_loop(0, n_chunks, body, jnp.zeros((8, tq), I32))
            return cnt.astype(F32).sum(axis=0, keepdims=True).astype(I32)

        need = topk - column_count(lambda blk, spos: blk > thr)
        lim = _tie_limit(
            lambda p: column_count(lambda blk, spos: (blk == thr) & (spos < p)),
            need, pos_bits + 1, jnp.zeros((1, tq), I32))
        lim_ref[...] = jnp.broadcast_to(lim, lim_ref.shape)

    lim = lim_ref[0:1, :]

    def bias_chunk(c, carry):
        blk = key_ref[rows(c), :]
        spos = c * sc + lax.broadcasted_iota(I32, (sc, tq), 0)
        sel = ((blk > thr) | ((blk == thr) & (spos <= lim))) & (blk > NEG_INF_KEY)
        key_ref[rows(c), :] = pltpu.bitcast(jnp.where(sel, 0.0, NEG_BIAS).astype(F32), I32)
        return carry

    lax.fori_loop(0, n_chunks, bias_chunk, 0)

    n_heads = n_kv * n_rep
    n_grp = n_heads // hb
    for j in range(n_grp):
        for r in range(hb):
            head = j * hb + r
            q4_ref[j, r * tq:(r + 1) * tq, :] = q_ref[:, head * hd:(head + 1) * hd]
    m_ref[...] = jnp.full(m_ref.shape, NEG_BIAS, F32)
    acc_ref[...] = jnp.zeros(acc_ref.shape, F32)

    def bias_of(c):
        bias = pltpu.bitcast(key_ref[rows(c), :], F32)
        return jnp.concatenate([bias] * hb, axis=1)

    def qk_store(c, j, bias_w, dst_ref):
        g = j * hb // n_rep
        dst_ref[...] = _nt_dot(k_ref[rows(c), g * hd:(g + 1) * hd], q4_ref[j]) + bias_w

    bufs = (lga_ref, lgb_ref)
    qk_store(0, 0, bias_of(0), bufs[0])

    def attn_chunk(c, carry):
        bias_w = bias_of(c)
        c_next = jnp.minimum(c + 1, n_chunks - 1)
        for j in range(n_grp):
            cur, nxt = bufs[j % 2], bufs[(j + 1) % 2]
            if j + 1 < n_grp:
                qk_store(c, j + 1, bias_w, nxt)
            else:
                qk_store(c_next, 0, bias_of(c_next), nxt)
            logit = cur[...]
            m_old = m_ref[j]
            m_new = jnp.maximum(m_old, jnp.max(logit, axis=0, keepdims=True))
            p = jnp.exp2(logit - m_new).astype(BF16)
            acc_ref[j] = (jnp.exp2(m_old - m_new) * acc_ref[j]
                          + _dot(vt_ref[j * hb // n_rep, c], p))
            m_ref[j] = m_new
        return carry

    lax.fori_loop(0, n_chunks, attn_chunk, 0)
    for j in range(n_grp):
        for r in range(hb):
            head = j * hb + r
            cols = slice(r * tq, (r + 1) * tq)
            out_t = acc_ref[j, 0:hd, cols] / acc_ref[j, hd:hd + 1, cols]
            o_ref[:, head * hd:(head + 1) * hd] = out_t.T.astype(o_ref.dtype)


def _prompt_attention(iq, wt, q, ika, ikb, k_bf, vt, *, tq, sc, topk, hd):
    t, dq = q.shape
    n_kv = k_bf.shape[1] // hd
    n_rep = dq // hd // n_kv
    assert tq >= topk and t % tq == 0 and t % sc == 0 and sc % 16 == 0
    hb = 4
    assert (n_kv * n_rep // hb) % 2 == 0
    n_grp = n_kv * n_rep // hb
    resident = functools.partial(pl.BlockSpec, pipeline_mode=pl.Buffered(1))
    return pl.pallas_call(
        functools.partial(_prompt_attn_kernel, tq=tq, sc=sc, topk=topk, n_rep=n_rep, hd=hd, hb=hb),
        grid=(t // tq,),
        in_specs=[
            pl.BlockSpec((tq, iq.shape[1]), lambda i: (i, 0)),
            pl.BlockSpec((wt.shape[0], tq), lambda i: (0, i)),
            pl.BlockSpec((tq, dq), lambda i: (i, 0)),
            resident(ika.shape, lambda i: (0, 0)),
            resident(ikb.shape, lambda i: (0, 0)),
            resident(k_bf.shape, lambda i: (0, 0)),
            resident(vt.shape, lambda i: (0, 0, 0, 0)),
        ],
        out_specs=pl.BlockSpec((tq, dq), lambda i: (i, 0)),
        out_shape=jax.ShapeDtypeStruct((t, dq), BF16),
        scratch_shapes=[
            pltpu.VMEM((t, tq), I32),
            pltpu.VMEM((t, tq), jnp.int16),
            pltpu.VMEM((8, tq), I32),
            pltpu.VMEM((n_grp, hb * tq, hd), BF16),
            pltpu.VMEM((n_grp, 1, hb * tq), F32),
            pltpu.VMEM((n_grp, vt.shape[2], hb * tq), F32),
            pltpu.VMEM((sc, hb * tq), F32),
            pltpu.VMEM((sc, hb * tq), F32),
        ],
        compiler_params=_cparams(("arbitrary",)),
        name="prompt_attention",
    )(iq, wt, q, ika, ikb, k_bf, vt)


def _sample_score_kernel(pt_ref, iq_ref, w_ref, ikn_ref, cache_ref, o_ref, buf_ref, sem_ref,
                         *, n_pages, page):
    b = pl.program_id(0)
    nb = pl.num_programs(0)

    def page_copy(bb, slot, p):
        return pltpu.make_async_copy(cache_ref.at[pt_ref[bb, p]], buf_ref.at[slot, p],
                                     sem_ref.at[slot])

    def start_all(bb, slot):
        def body(p, carry):
            page_copy(bb, slot, p).start()
            return carry
        lax.fori_loop(0, n_pages, body, 0)

    slot = b % 2

    @pl.when(b == 0)
    def _():
        start_all(b, slot)

    @pl.when(b + 1 < nb)
    def _():
        start_all(b + 1, 1 - slot)

    pltpu.make_async_copy(cache_ref.at[pl.ds(0, n_pages)], buf_ref.at[slot], sem_ref.at[slot]).wait()

    iq = iq_ref[0]
    w = w_ref[0]

    def head_sum(keys_t):
        s = jnp.maximum(_dot(iq, keys_t), 0.0) * w
        return jnp.sum(s, axis=0, keepdims=True)

    group = min(32, n_pages)

    def group_body(gi, carry):
        p0 = pl.multiple_of(gi * group, group)
        blk = buf_ref[slot, pl.ds(p0, group)]
        keys_t = jnp.concatenate([blk[r] for r in range(group)], axis=1).astype(BF16)
        s = head_sum(keys_t)
        for r in range(group):
            o_ref[0, pl.ds(p0 + r, 1), :] = s[:, r * page:(r + 1) * page]
        return carry
    lax.fori_loop(0, n_pages // group, group_body, 0)
    own = head_sum(jnp.broadcast_to(ikn_ref[0], (ikn_ref.shape[1], page)))
    lane = lax.broadcasted_iota(I32, (1, page), 1)
    o_ref[0, pl.ds(n_pages, 1), :] = jnp.where(lane == 0, own, -jnp.inf)
    o_ref[0, pl.ds(n_pages + 1, 7), :] = jnp.full((7, page), -jnp.inf, F32)


def _sample_scores(page_table, iq, w, ik_new, cache_ik_t):
    bs, n_pages = page_table.shape
    idim, page = cache_ik_t.shape[1:]
    rows = n_pages + 8
    grid_spec = pltpu.PrefetchScalarGridSpec(
        num_scalar_prefetch=1,
        grid=(bs,),
        in_specs=[
            pl.BlockSpec((1,) + iq.shape[1:], lambda b, pt: (b, 0, 0)),
            pl.BlockSpec((1,) + w.shape[1:], lambda b, pt: (b, 0, 0)),
            pl.BlockSpec((1,) + ik_new.shape[1:], lambda b, pt: (b, 0, 0)),
            pl.BlockSpec(memory_space=pl.ANY),
        ],
        out_specs=pl.BlockSpec((1, rows, page), lambda b, pt: (b, 0, 0)),
        scratch_shapes=[pltpu.VMEM((2, n_pages, idim, page), F32),
                        pltpu.SemaphoreType.DMA((2,))],
    )
    return pl.pallas_call(
        functools.partial(_sample_score_kernel, n_pages=n_pages, page=page),
        grid_spec=grid_spec,
        out_shape=jax.ShapeDtypeStruct((bs, rows, page), F32),
        compiler_params=_cparams(("arbitrary",)),
        name="sample_scores",
    )(page_table, iq, w, ik_new, cache_ik_t)


def _slab_pos(shape):
    nd = len(shape)
    return (lax.broadcasted_iota(I32, shape, nd - 2) * shape[-1]
            + lax.broadcasted_iota(I32, shape, nd - 1))


def _sample_thresh_kernel(s_ref, thr_ref, lim_ref, *, topk):
    key = _ordered_key(s_ref[...])
    bs, rows, page = key.shape
    pos = _slab_pos(key.shape)

    def count(pred):
        c = jnp.sum(jnp.where(pred, 1.0, 0.0), axis=1, keepdims=True)
        return jnp.sum(c, axis=2, keepdims=True).astype(I32)

    like = jnp.zeros((bs, 1, 1), I32)
    thr = _kth_largest(lambda v: count(key >= v), topk, like)
    need = topk - count(key > thr)
    lim = _tie_limit(lambda p: count((key == thr) & (pos < p)), need,
                     int(rows * page - 1).bit_length() + 1, like)
    thr_ref[...] = thr
    lim_ref[...] = lim


def _sample_compact_kernel(s_ref, thr_ref, lim_ref, idx_ref, *, topk):
    key = _ordered_key(s_ref[0])
    rows, page = key.shape
    kpad = 256
    thr, lim = thr_ref[0], lim_ref[0]
    sel = ((key > thr) | ((key == thr) & (_slab_pos(key.shape) <= lim))) & (key > NEG_INF_KEY)
    self32 = jnp.where(sel, 1.0, 0.0)
    zpad = jnp.zeros((kpad - rows, page), F32)

    def ones_where(pred):
        return jnp.where(pred, 1.0, 0.0).astype(BF16)

    upper = lax.broadcasted_iota(I32, (page, page), 0) < lax.broadcasted_iota(I32, (page, page), 1)
    within = _dot(self32.astype(BF16), ones_where(upper))
    tot = jnp.broadcast_to(jnp.sum(self32, axis=1, keepdims=True), (rows, page))
    tot = jnp.concatenate([tot, zpad], axis=0)
    earlier = lax.broadcasted_iota(I32, (kpad, kpad), 1) < lax.broadcasted_iota(I32, (kpad, kpad), 0)
    before = _dot(ones_where(earlier), tot.astype(BF16))

    before_row = before.T[0:1, :]
    ends_row = (before + tot).T[0:1, :]
    slot_r = lax.broadcasted_iota(I32, (topk, kpad), 0).astype(F32)
    owner = ones_where((before_row <= slot_r) & (slot_r < ends_row))
    code = jnp.concatenate([jnp.where(sel, within, -1.0), zpad], axis=0)
    row_id = lax.broadcasted_iota(I32, (kpad, page), 0).astype(F32)
    fetched = _dot(owner, jnp.concatenate([code, before, row_id], axis=1).astype(BF16))
    code_g, before_g, row_g = (fetched[:, 0:page], fetched[:, page:2 * page],
                               fetched[:, 2 * page:3 * page])
    slot = lax.broadcasted_iota(I32, (topk, page), 0).astype(F32)
    lane = lax.broadcasted_iota(I32, (topk, page), 1).astype(F32)
    total = (before + tot)[kpad - 1:kpad, :]
    hit = (code_g == slot - before_g) & (slot < total)
    lane_sel = jnp.max(jnp.where(hit, lane, -1.0), axis=1, keepdims=True)
    pos = jnp.where(lane_sel >= 0.0, row_g[:, 0:1] * page + lane_sel, -1.0)
    idx_ref[0] = pos.astype(I32)


def _sample_select(scores, *, topk):
    bs, rows, page = scores.shape
    assert rows % 8 == 0 and rows <= 256
    one = pl.BlockSpec((1, 1, 1), lambda b: (b, 0, 0))
    thr, lim = pl.pallas_call(
        functools.partial(_sample_thresh_kernel, topk=topk),
        out_shape=[jax.ShapeDtypeStruct((bs, 1, 1), I32)] * 2,
        compiler_params=pltpu.CompilerParams(vmem_limit_bytes=VMEM_LIMIT),
        name="sample_thresh",
    )(scores)
    return pl.pallas_call(
        functools.partial(_sample_compact_kernel, topk=topk),
        grid=(bs,),
        in_specs=[pl.BlockSpec((1, rows, page), lambda b: (b, 0, 0)), one, one],
        out_specs=pl.BlockSpec((1, topk, 1), lambda b: (b, 0, 0)),
        out_shape=jax.ShapeDtypeStruct((bs, topk, 1), I32),
        compiler_params=_cparams(("arbitrary",)),
        name="sample_compact",
    )(scores, thr, lim)


def _sample_attn_kernel(pt_ref, idx_ref, idxv_ref, q_ref, kn_ref, vn_ref, ck_ref, cv_ref, o_ref,
                        kbuf_ref, vbuf_ref, sem_ref, *, page, past, topk, n_rep):
    b = pl.program_id(0)
    nb = pl.num_programs(0)
    n_kv = kn_ref.shape[1]

    def row_copies(bb, slot, j):
        pos = jnp.clip(idx_ref[bb * topk + j], 0, past - 1)
        if page & (page - 1) == 0:
            page_no = lax.shift_right_logical(pos, page.bit_length() - 1)
            off = pos & (page - 1)
        else:
            page_no, off = lax.div(pos, page), lax.rem(pos, page)
        phys = pt_ref[bb * (past // page) + page_no]
        return (pltpu.make_async_copy(ck_ref.at[phys, off], kbuf_ref.at[slot, j], sem_ref.at[0, slot]),
                pltpu.make_async_copy(cv_ref.at[phys, off], vbuf_ref.at[slot, j], sem_ref.at[1, slot]))

    unroll = 8

    def start_all(bb, slot):
        def body(j8, carry):
            for u in range(unroll):
                ck, cv = row_copies(bb, slot, j8 * unroll + u)
                ck.start()
                cv.start()
            return carry
        lax.fori_loop(0, topk // unroll, body, 0)

    slot = b % 2

    @pl.when(b == 0)
    def _():
        start_all(b, slot)

    @pl.when(b + 1 < nb)
    def _():
        start_all(b + 1, 1 - slot)

    for h in range(topk // page):
        slab = pl.ds(h * page, page)
        pltpu.make_async_copy(ck_ref.at[0], kbuf_ref.at[slot, slab], sem_ref.at[0, slot]).wait()
        pltpu.make_async_copy(cv_ref.at[0], vbuf_ref.at[slot, slab], sem_ref.at[1, slot]).wait()

    pos = idxv_ref[0]
    bias = jnp.where((pos >= 0) & (pos < past), 0.0, NEG_BIAS)
    own_sel = jnp.max(jnp.where(pos == past, 1.0, 0.0), axis=1, keepdims=True)
    own_bias = jnp.where(own_sel > 0.0, 0.0, NEG_BIAS)
    k_all = kbuf_ref[slot].astype(BF16)
    v_all = vbuf_ref[slot].astype(BF16)
    for g in range(n_kv):
        rows = slice(g * n_rep, (g + 1) * n_rep)
        qg = q_ref[0, rows, :]
        kg = k_all[:, g, :]
        vg = v_all[:, g, :]
        kn = kn_ref[0, g:g + 1, :].astype(BF16).astype(F32)
        vn = vn_ref[0, g:g + 1, :].astype(BF16).astype(F32)
        logit = _nt_dot(qg, kg) + bias
        own = jnp.sum(qg.astype(F32) * kn, axis=1, keepdims=True) + own_bias
        m = jnp.maximum(jnp.max(logit, axis=1, keepdims=True), own)
        p = jnp.exp2(logit - m)
        p_own = jnp.exp2(own - m)
        denom = jnp.sum(p, axis=1, keepdims=True) + p_own
        num = _dot(p.astype(BF16), vg) + p_own.astype(BF16).astype(F32) * vn
        o_ref[0, rows, :] = num / denom


def _sample_attention(page_table, idx, q, k_new, v_new, cache_k, cache_v, *, topk):
    bs, n_pages = page_table.shape
    page, n_kv, hd = cache_k.shape[1:]
    n_heads = q.shape[1]
    past = n_pages * page
    assert topk % page == 0 and topk % 8 == 0
    grid_spec = pltpu.PrefetchScalarGridSpec(
        num_scalar_prefetch=2,
        grid=(bs,),
        in_specs=[
            pl.BlockSpec((1, 1, topk), lambda b, pt, ix: (b, 0, 0)),
            pl.BlockSpec((1, n_heads, hd), lambda b, pt, ix: (b, 0, 0)),
            pl.BlockSpec((1, n_kv, hd), lambda b, pt, ix: (b, 0, 0)),
            pl.BlockSpec((1, n_kv, hd), lambda b, pt, ix: (b, 0, 0)),
            pl.BlockSpec(memory_space=pl.ANY),
            pl.BlockSpec(memory_space=pl.ANY),
        ],
        out_specs=pl.BlockSpec((1, n_heads, hd), lambda b, pt, ix: (b, 0, 0)),
        scratch_shapes=[pltpu.VMEM((2, topk, n_kv, hd), F32),
                        pltpu.VMEM((2, topk, n_kv, hd), F32),
                        pltpu.SemaphoreType.DMA((2, 2))],
    )
    return pl.pallas_call(
        functools.partial(_sample_attn_kernel, page=page, past=past, topk=topk,
                          n_rep=n_heads // n_kv),
        grid_spec=grid_spec,
        out_shape=jax.ShapeDtypeStruct((bs, n_heads, hd), F32),
        compiler_params=_cparams(("arbitrary",)),
        name="sample_attention",
    )(page_table.reshape(-1), idx.reshape(-1), idx.reshape(bs, 1, topk), q, k_new, v_new,
      cache_k, cache_v)


def _rope_tables(pos, dim, pad_lanes=0):
    rot = dim // ROT_DIV
    half = rot // 2
    inv = ROPE_THETA ** (-np.arange(half, dtype=np.float64) / half)
    ang = np.asarray(pos, np.float64)[:, None] * inv[None, :]
    cos, sin = jnp.asarray(np.cos(ang), F32), jnp.asarray(np.sin(ang), F32)
    width = dim + pad_lanes
    a = jnp.pad(jnp.concatenate([cos, cos], axis=1), ((0, 0), (0, width - rot)), constant_values=1.0)
    b = jnp.pad(-sin, ((0, 0), (0, width - half)))
    c = jnp.pad(sin, ((0, 0), (half, width - rot)))
    reps = LANES // width
    return tuple(jnp.tile(t, (1, reps)) if reps > 1 else t for t in (a, b, c))


def _in_proj_columns(d_model, d_conv, dq, dkv):
    names = ("x", "b", "c", "q", "k", "v", "iq", "ik", "iw", "gc", "ga")
    sizes = (d_conv, d_conv, d_conv, dq, dkv, dkv, IDX_HEADS * IDX_DIM, IDX_DIM, IDX_HEADS,
             d_model, d_model)
    starts = np.concatenate([[0], np.cumsum(sizes)[:-1]])
    return {n: int(s) for n, s in zip(names, starts)}


def _mixer_common(u, w_all, w_gates, cols, ktabs, itabs, *, hd, dq, dkv, tm, tn):
    q = _rope_proj(u, w_all, ktabs, first_col=cols["q"], n=dq, half=hd // ROT_DIV // 2,
                   scale=hd ** -0.5 * LOG2E, tm=tm, tn=tn)
    iq = _rope_proj(u, w_all, itabs[0], first_col=cols["iq"], n=IDX_HEADS * IDX_DIM,
                    half=IDX_DIM // ROT_DIV // 2, scale=IDX_DIM ** -0.5, tm=tm, tn=tn)
    assert cols["iw"] == cols["ik"] + IDX_DIM
    k, v, misc, k_bf, v_bf, ika, ikb = _kv_proj(
        u, w_all, ktabs, itabs[1], k_col=cols["k"], v_col=cols["v"], misc_col=cols["ik"],
        dkv=dkv, tm=tm)
    gc, ga = _gate_proj(u, w_gates, tm=tm, tn=tn)
    return q, iq, k, v, misc, k_bf, v_bf, ika, ikb, gc, ga


def kernel(x_prompt, x_sample, cache_k, cache_v, cache_idx_k, state_conv, page_table,
           norm_ffn1_pre, norm_ffn1_post, w_ffn1_gate_up, w_ffn1_down,
           norm_mix_pre, norm_mix_post, w_in, w_conv, w_conv_out, w_attn_out, w_out,
           norm_ffn2_pre, norm_ffn2_post, w_ffn2_gate_up, w_ffn2_down):
    bp, t, d = x_prompt.shape
    bs, ts, _ = x_sample.shape
    depth = w_in.shape[0]
    page, n_kv, hd = cache_k.shape[2:]
    n_pages = page_table.shape[1]
    past = n_pages * page
    d_conv = w_conv.shape[2]
    dq = w_attn_out.shape[1]
    dkv = n_kv * hd
    assert bp == 1 and ts == 1 and dq == N_HEADS * hd and n_kv == N_KV_HEADS

    tm = min(512, t)
    tm_proj = min(1024, t)
    tf = 512
    tn = 1024
    tq, sc = min(256, t), min(512, t)
    cols = _in_proj_columns(d, d_conv, dq, dkv)
    topk_p = min(TOPK_MAX, t // 4)
    topk_s = min(TOPK_MAX, (past + ts) // 4)

    pos_p = np.arange(t)
    pos_s = np.full((bs,), past)
    ktabs_p, ktabs_s = _rope_tables(pos_p, hd), _rope_tables(pos_s, hd)
    itabs_p = (_rope_tables(pos_p, IDX_DIM), _rope_tables(pos_p, IDX_DIM, LANES - IDX_DIM))
    itabs_s = (_rope_tables(pos_s, IDX_DIM), _rope_tables(pos_s, IDX_DIM, LANES - IDX_DIM))

    hp = x_prompt.reshape(t, d)
    hs = x_sample.reshape(bs, d)
    outs = [[] for _ in range(8)]
    row = lambda a: a.reshape(1, -1)
    for l in range(depth):
        w1gu, w1d = w_ffn1_gate_up[l].astype(BF16), w_ffn1_down[l].astype(BF16)
        w2gu, w2d = w_ffn2_gate_up[l].astype(BF16), w_ffn2_down[l].astype(BF16)
        w_all, w_gates = w_in[l].astype(BF16), w_in[l][:, cols["gc"]:].astype(BF16)
        w_co, w_ao, w_o = (w_conv_out[l].astype(BF16), w_attn_out[l].astype(BF16),
                           w_out[l].astype(BF16))
        gain_mix = row(norm_mix_pre[l])

        hp, up = _ffn_half(hp, row(norm_ffn1_pre[l]), row(norm_ffn1_post[l]), w1gu, w1d,
                           tm=tm, tf=tf, next_gain=gain_mix)
        conv_y, tail = _conv_seq(up, w_all, w_conv[l], tm=tm_proj, tn=tn)
        q, iq, k, v, misc, k_bf, v_bf, ika, ikb, gc, ga = _mixer_common(
            up, w_all, w_gates, cols, ktabs_p, itabs_p, hd=hd, dq=dq, dkv=dkv, tm=tm_proj, tn=tn)
        wt = (misc[:, IDX_DIM:IDX_DIM + IDX_HEADS] * IDX_HEADS ** -0.5).T
        vt = v_bf.reshape(t // sc, sc, n_kv, hd).transpose(2, 0, 3, 1)
        vt = jnp.concatenate([vt, jnp.ones((n_kv, t // sc, 16, sc), BF16)], axis=2)
        attn_o = _prompt_attention(iq, wt, q, ika, ikb, k_bf, vt, tq=tq, sc=sc, topk=topk_p, hd=hd)
        hp = _merge(hp, conv_y, attn_o, gc, ga, row(norm_mix_post[l]), w_co, w_ao, w_o,
                    tm=tm, tn=tn // 2)
        outs[0].append(k.reshape(bp, t // page, page, n_kv, hd))
        outs[1].append(v.reshape(bp, t // page, page, n_kv, hd))
        outs[2].append(misc[:, :IDX_DIM].reshape(bp, t // page, page, IDX_DIM))
        outs[3].append(tail[tail.shape[0] - (CONV_WIDTH - 1):].reshape(bp, CONV_WIDTH - 1, d_conv))
        hp = _ffn_half(hp, row(norm_ffn2_pre[l]), row(norm_ffn2_post[l]), w2gu, w2d, tm=tm, tf=tf)

        hs, us = _ffn_half(hs, row(norm_ffn1_pre[l]), row(norm_ffn1_post[l]), w1gu, w1d,
                           tm=bs, tf=tf, next_gain=gain_mix)
        st = state_conv[l]
        conv_y, z = _conv_step(us, w_all, w_conv[l], st[:, 0, :], st[:, 1, :], tn=tn)
        q, iq, k, v, misc, k_bf, v_bf, ika, ikb, gc, ga = _mixer_common(
            us, w_all, w_gates, cols, ktabs_s, itabs_s, hd=hd, dq=dq, dkv=dkv, tm=bs, tn=tn)
        w_idx = (misc[:, IDX_DIM:IDX_DIM + IDX_HEADS] * IDX_HEADS ** -0.5).reshape(bs, IDX_HEADS, 1)
        scores = _sample_scores(page_table, iq.reshape(bs, IDX_HEADS, IDX_DIM), w_idx,
                                ika[:, :IDX_DIM].reshape(bs, IDX_DIM, 1),
                                cache_idx_k[l].transpose(0, 2, 1))
        idx = _sample_select(scores, topk=topk_s).reshape(bs, topk_s)
        attn_o = _sample_attention(page_table, idx, q.reshape(bs, N_HEADS, hd),
                                   k.reshape(bs, n_kv, hd), v.reshape(bs, n_kv, hd),
                                   cache_k[l], cache_v[l], topk=topk_s)
        hs = _merge(hs, conv_y, attn_o.reshape(bs, dq).astype(BF16), gc, ga, row(norm_mix_post[l]),
                    w_co, w_ao, w_o, tm=bs, tn=tn // 2)
        outs[4].append(k.reshape(bs, ts, n_kv, hd))
        outs[5].append(v.reshape(bs, ts, n_kv, hd))
        outs[6].append(misc[:, :IDX_DIM].reshape(bs, ts, IDX_DIM))
        outs[7].append(jnp.stack([st[:, 1, :], z], axis=1))
        hs = _ffn_half(hs, row(norm_ffn2_pre[l]), row(norm_ffn2_post[l]), w2gu, w2d, tm=bs, tf=tf)

    return (hp.reshape(bp, t, d), hs.reshape(bs, ts, d)) + tuple(jnp.stack(o) for o in outs)
```

```python
import functools

import numpy as np
import jax
import jax.numpy as jnp
from jax import lax
from jax.experimental import pallas as pl
from jax.experimental.pallas import tpu as pltpu

F32 = jnp.float32
BF16 = jnp.bfloat16
I32 = jnp.int32

N_HEADS = 16
N_KV_HEADS = 4
IDX_HEADS = 16
IDX_DIM = 64
TOPK_MAX = 256
CONV_WIDTH = 3
ROPE_THETA = 500000.0
ROT_DIV = 4
EPS = 1e-6

LANES = 128
VMEM_LIMIT = 56 * 1024 * 1024
LOG2E = 1.4426950408889634
NEG_BIAS = -1e30
INT_MIN = -2 ** 31
INT_MAX = 2 ** 31 - 1
NEG_INF_KEY = int(np.int32(np.uint32(0x807FFFFF)))


def _cparams(sem):
    return pltpu.CompilerParams(dimension_semantics=sem, vmem_limit_bytes=VMEM_LIMIT)


def _rms_scale(x):
    return lax.rsqrt(jnp.mean(x * x, axis=-1, keepdims=True) + EPS)


def _nt_dot(a, b):
    return lax.dot_general(a, b, (((1,), (1,)), ((), ())), preferred_element_type=F32)


def _dot(a, b):
    return jnp.dot(a, b, preferred_element_type=F32)


def _sigmoid(x):
    return 0.5 * jnp.tanh(0.5 * x) + 0.5


def _ffn_accumulate(x_ref, pre_ref, wg_ref, wu_ref, wd_ref, xn_ref, acc_ref):
    @pl.when(pl.program_id(1) == 0)
    def _():
        x = x_ref[...]
        xn_ref[...] = (x * _rms_scale(x) * pre_ref[...]).astype(BF16)
        acc_ref[...] = jnp.zeros_like(acc_ref)

    xn = xn_ref[...]
    g = _dot(xn, wg_ref[...])
    u = _dot(xn, wu_ref[...])
    h = (g * _sigmoid(g) * u).astype(BF16)
    acc_ref[...] += _dot(h, wd_ref[...])


def _ffn_result(x_ref, post_ref, acc_ref):
    y = acc_ref[...]
    return x_ref[...] + 0.5 * (y * _rms_scale(y) * post_ref[...])


def _ffn_kernel(x_ref, pre_ref, post_ref, wg_ref, wu_ref, wd_ref, o_ref, xn_ref, acc_ref):
    _ffn_accumulate(x_ref, pre_ref, wg_ref, wu_ref, wd_ref, xn_ref, acc_ref)

    @pl.when(pl.program_id(1) == pl.num_programs(1) - 1)
    def _():
        o_ref[...] = _ffn_result(x_ref, post_ref, acc_ref)


def _ffn_norm_kernel(x_ref, pre_ref, post_ref, ng_ref, wg_ref, wu_ref, wd_ref, o_ref, u_ref,
                     xn_ref, acc_ref):
    _ffn_accumulate(x_ref, pre_ref, wg_ref, wu_ref, wd_ref, xn_ref, acc_ref)

    @pl.when(pl.program_id(1) == pl.num_programs(1) - 1)
    def _():
        out = _ffn_result(x_ref, post_ref, acc_ref)
        o_ref[...] = out
        u_ref[...] = (out * _rms_scale(out) * ng_ref[...]).astype(BF16)


def _ffn_half(x, pre, post, w_gu, w_d, *, tm, tf, next_gain=None):
    m, d = x.shape
    f = w_d.shape[0]
    nf = f // tf
    row_spec = pl.BlockSpec((tm, d), lambda i, j: (i, 0))
    vec_spec = pl.BlockSpec((1, d), lambda i, j: (0, 0))
    w_specs = [
        pl.BlockSpec((d, tf), lambda i, j: (0, j)),
        pl.BlockSpec((d, tf), lambda i, j: (0, nf + j)),
        pl.BlockSpec((tf, d), lambda i, j: (j, 0)),
    ]
    common = dict(
        grid=(m // tm, nf),
        scratch_shapes=[pltpu.VMEM((tm, d), BF16), pltpu.VMEM((tm, d), F32)],
        compiler_params=_cparams(("arbitrary", "arbitrary")),
    )
    if next_gain is None:
        return pl.pallas_call(
            _ffn_kernel,
            in_specs=[row_spec, vec_spec, vec_spec] + w_specs,
            out_specs=row_spec,
            out_shape=jax.ShapeDtypeStruct((m, d), F32),
            name="ffn_half", **common,
        )(x, pre, post, w_gu, w_gu, w_d)
    return pl.pallas_call(
        _ffn_norm_kernel,
        in_specs=[row_spec, vec_spec, vec_spec, vec_spec] + w_specs,
        out_specs=[row_spec, row_spec],
        out_shape=[jax.ShapeDtypeStruct((m, d), F32), jax.ShapeDtypeStruct((m, d), BF16)],
        name="ffn_half_norm", **common,
    )(x, pre, post, next_gain, w_gu, w_gu, w_d)


def _merge_kernel(h_ref, cy_ref, ao_ref, gc_ref, ga_ref, post_ref, wc_ref, wa_ref, wo_ref,
                  o_ref, mix_ref):
    j = pl.program_id(1)
    nj = mix_ref.shape[0]
    mc = _dot(cy_ref[...], wc_ref[...])
    ma = _dot(ao_ref[...], wa_ref[...])
    mix_ref[j] = (gc_ref[...].astype(F32) * mc + ga_ref[...].astype(F32) * ma).astype(BF16)

    @pl.when(j == nj - 1)
    def _():
        mix = jnp.concatenate([mix_ref[jj] for jj in range(nj)], axis=1)
        y = _dot(mix, wo_ref[...])
        o_ref[...] = h_ref[...] + y * _rms_scale(y) * post_ref[...]


def _merge(h, conv_y, attn_o, gc, ga, post, w_co, w_ao, w_out, *, tm, tn):
    m, d = h.shape
    return pl.pallas_call(
        _merge_kernel,
        grid=(m // tm, d // tn),
        in_specs=[
            pl.BlockSpec((tm, d), lambda i, j: (i, 0)),
            pl.BlockSpec((tm, d), lambda i, j: (i, 0)),
            pl.BlockSpec((tm, d), lambda i, j: (i, 0)),
            pl.BlockSpec((tm, tn), lambda i, j: (i, j)),
            pl.BlockSpec((tm, tn), lambda i, j: (i, j)),
            pl.BlockSpec((1, d), lambda i, j: (0, 0)),
            pl.BlockSpec((d, tn), lambda i, j: (0, j)),
            pl.BlockSpec((d, tn), lambda i, j: (0, j)),
            pl.BlockSpec((d, d), lambda i, j: (0, 0), pipeline_mode=pl.Buffered(1)),
        ],
        out_specs=pl.BlockSpec((tm, d), lambda i, j: (i, 0)),
        out_shape=jax.ShapeDtypeStruct((m, d), F32),
        scratch_shapes=[pltpu.VMEM((d // tn, tm, tn), BF16)],
        compiler_params=_cparams(("arbitrary", "arbitrary")),
        name="merge",
    )(h, conv_y, attn_o, gc, ga, post, w_co, w_ao, w_out)


def _rope(x, a, b, c, half):
    return x * a + pltpu.roll(x, LANES - half, axis=1) * b + pltpu.roll(x, half, axis=1) * c


def _conv_seq_kernel(u_ref, wx_ref, wb_ref, wc_ref, wconv_ref, cy_ref, tail_ref, carry_ref):
    i = pl.program_id(0)
    j = pl.program_id(1)
    xn = u_ref[...]
    z = _dot(xn, wc_ref[...]) * _dot(xn, wx_ref[...])
    b = _dot(xn, wb_ref[...])
    tm = z.shape[0]

    @pl.when(i == 0)
    def _():
        carry_ref[j] = jnp.zeros(carry_ref.shape[1:], F32)

    prev = carry_ref[j]
    row = lax.broadcasted_iota(I32, z.shape, 0)
    z1 = jnp.where(row == 0, prev[1:2, :], pltpu.roll(z, 1, axis=0))
    z2 = jnp.where(row == 0, prev[0:1, :],
                   jnp.where(row == 1, prev[1:2, :], pltpu.roll(z, 2, axis=0)))
    w = wconv_ref[...]
    cy_ref[...] = (b * (z2 * w[0:1, :] + z1 * w[1:2, :] + z * w[2:3, :])).astype(BF16)
    tail = z[tm - 8:, :]
    carry_ref[j] = pltpu.roll(tail, 2, axis=0)
    tail_ref[...] = tail


def _col_blocks(d, tn, first_col):
    assert first_col % tn == 0
    return pl.BlockSpec((d, tn), lambda i, j: (0, first_col // tn + j))


def _conv_seq(u, w_all, w_conv, *, tm, tn):
    m, d = u.shape
    dc = w_conv.shape[1]
    return pl.pallas_call(
        _conv_seq_kernel,
        grid=(m // tm, dc // tn),
        in_specs=[
            pl.BlockSpec((tm, d), lambda i, j: (i, 0)),
            _col_blocks(d, tn, 0), _col_blocks(d, tn, dc), _col_blocks(d, tn, 2 * dc),
            pl.BlockSpec((CONV_WIDTH, tn), lambda i, j: (0, j)),
        ],
        out_specs=[
            pl.BlockSpec((tm, tn), lambda i, j: (i, j)),
            pl.BlockSpec((8, tn), lambda i, j: (i, j)),
        ],
        out_shape=[jax.ShapeDtypeStruct((m, dc), BF16), jax.ShapeDtypeStruct((m // tm * 8, dc), F32)],
        scratch_shapes=[pltpu.VMEM((dc // tn, 8, tn), F32)],
        compiler_params=_cparams(("arbitrary", "arbitrary")),
        name="conv_seq",
    )(u, w_all, w_all, w_all, w_conv)


def _conv_step_kernel(u_ref, wx_ref, wb_ref, wc_ref, wconv_ref, s0_ref, s1_ref, cy_ref, z_ref):
    xn = u_ref[...]
    z = _dot(xn, wc_ref[...]) * _dot(xn, wx_ref[...])
    b = _dot(xn, wb_ref[...])
    w = wconv_ref[...]
    cy_ref[...] = (b * (s0_ref[...] * w[0:1, :] + s1_ref[...] * w[1:2, :] + z * w[2:3, :])
                   ).astype(BF16)
    z_ref[...] = z


def _conv_step(u, w_all, w_conv, s0, s1, *, tn):
    m, d = u.shape
    dc = w_conv.shape[1]
    cspec = pl.BlockSpec((m, tn), lambda i, j: (0, j))
    return pl.pallas_call(
        _conv_step_kernel,
        grid=(1, dc // tn),
        in_specs=[
            pl.BlockSpec((m, d), lambda i, j: (0, 0)),
            _col_blocks(d, tn, 0), _col_blocks(d, tn, dc), _col_blocks(d, tn, 2 * dc),
            pl.BlockSpec((CONV_WIDTH, tn), lambda i, j: (0, j)),
            cspec, cspec,
        ],
        out_specs=[cspec, cspec],
        out_shape=[jax.ShapeDtypeStruct((m, dc), BF16), jax.ShapeDtypeStruct((m, dc), F32)],
        compiler_params=_cparams(("arbitrary", "arbitrary")),
        name="conv_step",
    )(u, w_all, w_all, w_all, w_conv, s0, s1)


def _rope_proj_kernel(u_ref, w_ref, a_ref, b_ref, c_ref, o_ref, *, half, scale):
    y = _dot(u_ref[...], w_ref[...])
    a, b, c = a_ref[...], b_ref[...], c_ref[...]
    for g in range(y.shape[1] // LANES):
        sl = slice(g * LANES, (g + 1) * LANES)
        o_ref[:, sl] = (_rope(y[:, sl], a, b, c, half) * scale).astype(o_ref.dtype)


def _rope_proj(u, w_all, tabs, *, first_col, n, half, scale, tm, tn):
    m, d = u.shape
    tspec = pl.BlockSpec((tm, LANES), lambda i, j: (i, 0))
    return pl.pallas_call(
        functools.partial(_rope_proj_kernel, half=half, scale=scale),
        grid=(m // tm, n // tn),
        in_specs=[
            pl.BlockSpec((tm, d), lambda i, j: (i, 0)),
            _col_blocks(d, tn, first_col),
            tspec, tspec, tspec,
        ],
        out_specs=pl.BlockSpec((tm, tn), lambda i, j: (i, j)),
        out_shape=jax.ShapeDtypeStruct((m, n), BF16),
        compiler_params=_cparams(("arbitrary", "arbitrary")),
        name="rope_proj",
    )(u, w_all, *tabs)


def _kv_proj_kernel(u_ref, wk_ref, wv_ref, wm_ref, ka_ref, kb_ref, kc_ref,
                    ia_ref, ib_ref, ic_ref,
                    k_ref, v_ref, misc_ref, kbf_ref, vbf_ref, ika_ref, ikb_ref, *, dkv):
    xn = u_ref[...]
    yk = _dot(xn, wk_ref[...])
    a, b, c = ka_ref[...], kb_ref[...], kc_ref[...]
    for g in range(dkv // LANES):
        sl = slice(g * LANES, (g + 1) * LANES)
        kg = _rope(yk[:, sl], a, b, c, LANES // ROT_DIV // 2)
        k_ref[:, sl] = kg
        kbf_ref[:, sl] = kg.astype(BF16)
    v = _dot(xn, wv_ref[...])
    v_ref[...] = v
    vbf_ref[...] = v.astype(BF16)
    misc = _rope(_dot(xn, wm_ref[...]), ia_ref[...], ib_ref[...], ic_ref[...],
                 IDX_DIM // ROT_DIV // 2)
    misc_ref[...] = misc
    lane = lax.broadcasted_iota(I32, misc.shape, 1)
    ik_lo = jnp.where(lane < IDX_DIM, misc, 0.0)
    ika_ref[...] = ik_lo.astype(BF16)
    ikb_ref[...] = pltpu.roll(ik_lo, IDX_DIM, axis=1).astype(BF16)


def _kv_proj(u, w_all, ktabs, itabs, *, k_col, v_col, misc_col, dkv, tm):
    m, d = u.shape
    assert k_col % dkv == 0 and v_col % dkv == 0 and misc_col % LANES == 0
    tspec = pl.BlockSpec((tm, LANES), lambda i: (i, 0))
    kvspec = pl.BlockSpec((tm, dkv), lambda i: (i, 0))
    mspec = pl.BlockSpec((tm, LANES), lambda i: (i, 0))
    return pl.pallas_call(
        functools.partial(_kv_proj_kernel, dkv=dkv),
        grid=(m // tm,),
        in_specs=[
            pl.BlockSpec((tm, d), lambda i: (i, 0)),
            pl.BlockSpec((d, dkv), lambda i: (0, k_col // dkv)),
            pl.BlockSpec((d, dkv), lambda i: (0, v_col // dkv)),
            pl.BlockSpec((d, LANES), lambda i: (0, misc_col // LANES)),
            tspec, tspec, tspec, tspec, tspec, tspec,
        ],
        out_specs=[kvspec, kvspec, mspec, kvspec, kvspec, mspec, mspec],
        out_shape=[
            jax.ShapeDtypeStruct((m, dkv), F32), jax.ShapeDtypeStruct((m, dkv), F32),
            jax.ShapeDtypeStruct((m, LANES), F32),
            jax.ShapeDtypeStruct((m, dkv), BF16), jax.ShapeDtypeStruct((m, dkv), BF16),
            jax.ShapeDtypeStruct((m, LANES), BF16), jax.ShapeDtypeStruct((m, LANES), BF16),
        ],
        compiler_params=_cparams(("arbitrary",)),
        name="kv_proj",
    )(u, w_all, w_all, w_all, *ktabs, *itabs)


def _gate_proj_kernel(u_ref, wc_ref, wa_ref, gc_ref, ga_ref):
    xn = u_ref[...]
    gc_ref[...] = _sigmoid(_dot(xn, wc_ref[...])).astype(BF16)
    ga_ref[...] = _sigmoid(_dot(xn, wa_ref[...])).astype(BF16)


def _gate_proj(u, w_gates, *, tm, tn):
    m, d = u.shape
    n = w_gates.shape[1] // 2
    nb = n // tn
    ospec = pl.BlockSpec((tm, tn), lambda i, j: (i, j))
    return pl.pallas_call(
        _gate_proj_kernel,
        grid=(m // tm, nb),
        in_specs=[
            pl.BlockSpec((tm, d), lambda i, j: (i, 0)),
            pl.BlockSpec((d, tn), lambda i, j: (0, j)),
            pl.BlockSpec((d, tn), lambda i, j: (0, nb + j)),
        ],
        out_specs=[ospec, ospec],
        out_shape=[jax.ShapeDtypeStruct((m, n), BF16), jax.ShapeDtypeStruct((m, n), BF16)],
        compiler_params=_cparams(("arbitrary", "arbitrary")),
        name="gate_proj",
    )(u, w_gates, w_gates)


def _ordered_key(score):
    bits = pltpu.bitcast(score, I32)
    return bits ^ ((bits >> 31) & INT_MAX)


def _kth_largest(count_ge, k, like):
    def body(b, lo):
        cand = lo + jnp.left_shift(jnp.int32(1), 31 - b)
        return jnp.where(count_ge(cand) >= k, cand, lo)
    return lax.fori_loop(0, 32, body, jnp.full_like(like, INT_MIN))


def _tie_limit(count_eq_before, need, nbits, like):
    def body(b, lo):
        cand = lo + jnp.left_shift(jnp.int32(1), nbits - 1 - b)
        return jnp.where(count_eq_before(cand) < need, cand, lo)
    return lax.fori_loop(0, nbits, body, jnp.zeros_like(like))


def _prompt_attn_kernel(iq_ref, wt_ref, q_ref, ika_ref, ikb_ref, k_ref, vt_ref, o_ref,
                        key_ref, half_ref, lim_ref, q4_ref, m_ref, acc_ref, lga_ref, lgb_ref,
                        *, tq, sc, topk, n_rep, hd, hb):
    i = pl.program_id(0)
    t0 = i * tq
    n_chunks = (t0 + tq + sc - 1) // sc
    n_pairs = iq_ref.shape[1] // LANES
    n_kv = k_ref.shape[1] // hd
    pos_bits = int(k_ref.shape[0] - 1).bit_length()
    i16_min = -2 ** 15

    def rows(c):
        return pl.ds(pl.multiple_of(c * sc, sc), sc)

    def score_chunk(c, carry):
        ka = ika_ref[rows(c), :]
        kb = ikb_ref[rows(c), :]
        acc = jnp.zeros((sc, tq), F32)
        for p in range(n_pairs):
            iq_p = iq_ref[:, p * LANES:(p + 1) * LANES]
            acc += jnp.maximum(_nt_dot(ka, iq_p), 0.0) * wt_ref[2 * p:2 * p + 1, :]
            acc += jnp.maximum(_nt_dot(kb, iq_p), 0.0) * wt_ref[2 * p + 1:2 * p + 2, :]
        spos = c * sc + lax.broadcasted_iota(I32, (sc, tq), 0)
        tpos = t0 + lax.broadcasted_iota(I32, (sc, tq), 1)
        key = _ordered_key(jnp.where(spos <= tpos, acc, -jnp.inf))
        key_ref[rows(c), :] = key
        half_ref[rows(c), :] = (key >> 16).astype(jnp.int16)
        return carry

    lax.fori_loop(0, n_chunks, score_chunk, 0)

    def half_count(cand, strict=False):
        cand16 = cand.astype(jnp.int16)

        def hits(c):
            blk = half_ref[rows(c), :]
            hit = jnp.where(blk > cand16 if strict else blk >= cand16,
                            jnp.bfloat16(1), jnp.bfloat16(0))
            part = hit[0:16]
            for r in range(1, sc // 16):
                part = part + hit[r * 16:(r + 1) * 16]
            return part

        def body2(c2, cnt):
            return cnt + (hits(2 * c2) + hits(2 * c2 + 1)).astype(F32)

        def body1(c, cnt):
            return cnt + hits(c).astype(F32)

        n2 = n_chunks // 2
        cnt = lax.fori_loop(0, n2, body2, jnp.zeros((16, tq), F32))
        cnt = lax.fori_loop(2 * n2, n_chunks, body1, cnt)
        return cnt.sum(axis=0, keepdims=True).astype(I32)

    def half_search(k):
        def body(b, carry):
            lo, cnt_lo = carry
            cand = lo + jnp.left_shift(jnp.int32(1), 15 - b)
            cnt = half_count(cand)
            ok = cnt >= k
            return jnp.where(ok, cand, lo), jnp.where(ok, cnt, cnt_lo)
        init = (jnp.full((1, tq), i16_min, I32), jnp.full((1, tq), n_chunks * sc, I32))
        return lax.fori_loop(0, 16, body, init)

    thr_hi, _ = half_search(topk)
    n_gt_hi = half_count(thr_hi, strict=True)

    def low_chunk(c, carry):
        key = key_ref[rows(c), :]
        low = (key & 0xFFFF) - 2 ** 15
        half_ref[rows(c), :] = jnp.where((key >> 16) == thr_hi, low, i16_min).astype(jnp.int16)
        return carry

    lax.fori_loop(0, n_chunks, low_chunk, 0)
    thr_lo, n_ge_lo = half_search(topk - n_gt_hi)
    thr = thr_hi * 65536 + (thr_lo + 2 ** 15)
    n_ge = n_gt_hi + n_ge_lo
    has_tie = jnp.max(jnp.where((n_ge > topk) & (thr > NEG_INF_KEY), 1.0, 0.0)) > 0.0
    lim_ref[...] = jnp.full(lim_ref.shape, INT_MAX, I32)

    @pl.when(has_tie)
    def _():
        def column_count(pred):
            def body(c, cnt):
                blk = key_ref[rows(c), :]
                spos = c * sc + lax.broadcasted_iota(I32, (sc, tq), 0)
                hit = jnp.where(pred(blk, spos), 1, 0).astype(I32)
                return cnt + hit.reshape(sc // 8, 8, tq).sum(axis=0)
            cnt = lax.fori_loop(0, n_chunks, body, jnp.zeros((8, tq), I32))
            return cnt.astype(F32).sum(axis=0, keepdims=True).astype(I32)

        need = topk - column_count(lambda blk, spos: blk > thr)
        lim = _tie_limit(
            lambda p: column_count(lambda blk, spos: (blk == thr) & (spos < p)),
            need, pos_bits + 1, jnp.zeros((1, tq), I32))
        lim_ref[...] = jnp.broadcast_to(lim, lim_ref.shape)

    lim = lim_ref[0:1, :]

    def bias_chunk(c, carry):
        blk = key_ref[rows(c), :]
        spos = c * sc + lax.broadcasted_iota(I32, (sc, tq), 0)
        sel = ((blk > thr) | ((blk == thr) & (spos <= lim))) & (blk > NEG_INF_KEY)
        key_ref[rows(c), :] = pltpu.bitcast(jnp.where(sel, 0.0, NEG_BIAS).astype(F32), I32)
        return carry

    lax.fori_loop(0, n_chunks, bias_chunk, 0)

    n_heads = n_kv * n_rep
    n_grp = n_heads // hb
    for j in range(n_grp):
        for r in range(hb):
            head = j * hb + r
            q4_ref[j, r * tq:(r + 1) * tq, :] = q_ref[:, head * hd:(head + 1) * hd]
    m_ref[...] = jnp.full(m_ref.shape, NEG_BIAS, F32)
    acc_ref[...] = jnp.zeros(acc_ref.shape, F32)

    def bias_of(c):
        bias = pltpu.bitcast(key_ref[rows(c), :], F32)
        return jnp.concatenate([bias] * hb, axis=1)

    def qk_store(c, j, bias_w, dst_ref):
        g = j * hb // n_rep
        dst_ref[...] = _nt_dot(k_ref[rows(c), g * hd:(g + 1) * hd], q4_ref[j]) + bias_w

    bufs = (lga_ref, lgb_ref)
    qk_store(0, 0, bias_of(0), bufs[0])

    def attn_chunk(c, carry):
        bias_w = bias_of(c)
        c_next = jnp.minimum(c + 1, n_chunks - 1)
        for j in range(n_grp):
            cur, nxt = bufs[j % 2], bufs[(j + 1) % 2]
            if j + 1 < n_grp:
                qk_store(c, j + 1, bias_w, nxt)
            else:
                qk_store(c_next, 0, bias_of(c_next), nxt)
            logit = cur[...]
            m_old = m_ref[j]
            m_new = jnp.maximum(m_old, jnp.max(logit, axis=0, keepdims=True))
            p = jnp.exp2(logit - m_new).astype(BF16)
            acc_ref[j] = (jnp.exp2(m_old - m_new) * acc_ref[j]
                          + _dot(vt_ref[j * hb // n_rep, c], p))
            m_ref[j] = m_new
        return carry

    lax.fori_loop(0, n_chunks, attn_chunk, 0)
    for j in range(n_grp):
        for r in range(hb):
            head = j * hb + r
            cols = slice(r * tq, (r + 1) * tq)
            out_t = acc_ref[j, 0:hd, cols] / acc_ref[j, hd:hd + 1, cols]
            o_ref[:, head * hd:(head + 1) * hd] = out_t.T.astype(o_ref.dtype)


def _prompt_attention(iq, wt, q, ika, ikb, k_bf, vt, *, tq, sc, topk, hd):
    t, dq = q.shape
    n_kv = k_bf.shape[1] // hd
    n_rep = dq // hd // n_kv
    assert tq >= topk and t % tq == 0 and t % sc == 0 and sc % 16 == 0
    hb = 4
    assert (n_kv * n_rep // hb) % 2 == 0
    n_grp = n_kv * n_rep // hb
    resident = functools.partial(pl.BlockSpec, pipeline_mode=pl.Buffered(1))
    return pl.pallas_call(
        functools.partial(_prompt_attn_kernel, tq=tq, sc=sc, topk=topk, n_rep=n_rep, hd=hd, hb=hb),
        grid=(t // tq,),
        in_specs=[
            pl.BlockSpec((tq, iq.shape[1]), lambda i: (i, 0)),
            pl.BlockSpec((wt.shape[0], tq), lambda i: (0, i)),
            pl.BlockSpec((tq, dq), lambda i: (i, 0)),
            resident(ika.shape, lambda i: (0, 0)),
            resident(ikb.shape, lambda i: (0, 0)),
            resident(k_bf.shape, lambda i: (0, 0)),
            resident(vt.shape, lambda i: (0, 0, 0, 0)),
        ],
        out_specs=pl.BlockSpec((tq, dq), lambda i: (i, 0)),
        out_shape=jax.ShapeDtypeStruct((t, dq), BF16),
        scratch_shapes=[
            pltpu.VMEM((t, tq), I32),
            pltpu.VMEM((t, tq), jnp.int16),
            pltpu.VMEM((8, tq), I32),
            pltpu.VMEM((n_grp, hb * tq, hd), BF16),
            pltpu.VMEM((n_grp, 1, hb * tq), F32),
            pltpu.VMEM((n_grp, vt.shape[2], hb * tq), F32),
            pltpu.VMEM((sc, hb * tq), F32),
            pltpu.VMEM((sc, hb * tq), F32),
        ],
        compiler_params=_cparams(("arbitrary",)),
        name="prompt_attention",
    )(iq, wt, q, ika, ikb, k_bf, vt)


def _sample_score_kernel(pt_ref, iq_ref, w_ref, ikn_ref, cache_ref, o_ref, buf_ref, sem_ref,
                         *, n_pages, page):
    b = pl.program_id(0)
    nb = pl.num_programs(0)

    def page_copy(bb, slot, p):
        return pltpu.make_async_copy(cache_ref.at[pt_ref[bb, p]], buf_ref.at[slot, p],
                                     sem_ref.at[slot])

    def start_all(bb, slot):
        def body(p, carry):
            page_copy(bb, slot, p).start()
            return carry
        lax.fori_loop(0, n_pages, body, 0)

    slot = b % 2

    @pl.when(b == 0)
    def _():
        start_all(b, slot)

    @pl.when(b + 1 < nb)
    def _():
        start_all(b + 1, 1 - slot)

    pltpu.make_async_copy(cache_ref.at[pl.ds(0, n_pages)], buf_ref.at[slot], sem_ref.at[slot]).wait()

    iq = iq_ref[0]
    w = w_ref[0]

    def head_sum(keys_t):
        s = jnp.maximum(_dot(iq, keys_t), 0.0) * w
        return jnp.sum(s, axis=0, keepdims=True)

    group = min(32, n_pages)

    def group_body(gi, carry):
        p0 = pl.multiple_of(gi * group, group)
        blk = buf_ref[slot, pl.ds(p0, group)]
        keys_t = jnp.concatenate([blk[r] for r in range(group)], axis=1).astype(BF16)
        s = head_sum(keys_t)
        for r in range(group):
            o_ref[0, pl.ds(p0 + r, 1), :] = s[:, r * page:(r + 1) * page]
        return carry
    lax.fori_loop(0, n_pages // group, group_body, 0)
    own = head_sum(jnp.broadcast_to(ikn_ref[0], (ikn_ref.shape[1], page)))
    lane = lax.broadcasted_iota(I32, (1, page), 1)
    o_ref[0, pl.ds(n_pages, 1), :] = jnp.where(lane == 0, own, -jnp.inf)
    o_ref[0, pl.ds(n_pages + 1, 7), :] = jnp.full((7, page), -jnp.inf, F32)


def _sample_scores(page_table, iq, w, ik_new, cache_ik_t):
    bs, n_pages = page_table.shape
    idim, page = cache_ik_t.shape[1:]
    rows = n_pages + 8
    grid_spec = pltpu.PrefetchScalarGridSpec(
        num_scalar_prefetch=1,
        grid=(bs,),
        in_specs=[
            pl.BlockSpec((1,) + iq.shape[1:], lambda b, pt: (b, 0, 0)),
            pl.BlockSpec((1,) + w.shape[1:], lambda b, pt: (b, 0, 0)),
            pl.BlockSpec((1,) + ik_new.shape[1:], lambda b, pt: (b, 0, 0)),
            pl.BlockSpec(memory_space=pl.ANY),
        ],
        out_specs=pl.BlockSpec((1, rows, page), lambda b, pt: (b, 0, 0)),
        scratch_shapes=[pltpu.VMEM((2, n_pages, idim, page), F32),
                        pltpu.SemaphoreType.DMA((2,))],
    )
    return pl.pallas_call(
        functools.partial(_sample_score_kernel, n_pages=n_pages, page=page),
        grid_spec=grid_spec,
        out_shape=jax.ShapeDtypeStruct((bs, rows, page), F32),
        compiler_params=_cparams(("arbitrary",)),
        name="sample_scores",
    )(page_table, iq, w, ik_new, cache_ik_t)


def _slab_pos(shape):
    nd = len(shape)
    return (lax.broadcasted_iota(I32, shape, nd - 2) * shape[-1]
            + lax.broadcasted_iota(I32, shape, nd - 1))


def _sample_thresh_kernel(s_ref, thr_ref, lim_ref, *, topk):
    key = _ordered_key(s_ref[...])
    bs, rows, page = key.shape
    pos = _slab_pos(key.shape)

    def count(pred):
        c = jnp.sum(jnp.where(pred, 1.0, 0.0), axis=1, keepdims=True)
        return jnp.sum(c, axis=2, keepdims=True).astype(I32)

    like = jnp.zeros((bs, 1, 1), I32)
    thr = _kth_largest(lambda v: count(key >= v), topk, like)
    need = topk - count(key > thr)
    lim = _tie_limit(lambda p: count((key == thr) & (pos < p)), need,
                     int(rows * page - 1).bit_length() + 1, like)
    thr_ref[...] = thr
    lim_ref[...] = lim


def _sample_compact_kernel(s_ref, thr_ref, lim_ref, idx_ref, *, topk):
    key = _ordered_key(s_ref[0])
    rows, page = key.shape
    kpad = 256
    thr, lim = thr_ref[0], lim_ref[0]
    sel = ((key > thr) | ((key == thr) & (_slab_pos(key.shape) <= lim))) & (key > NEG_INF_KEY)
    self32 = jnp.where(sel, 1.0, 0.0)
    zpad = jnp.zeros((kpad - rows, page), F32)

    def ones_where(pred):
        return jnp.where(pred, 1.0, 0.0).astype(BF16)

    upper = lax.broadcasted_iota(I32, (page, page), 0) < lax.broadcasted_iota(I32, (page, page), 1)
    within = _dot(self32.astype(BF16), ones_where(upper))
    tot = jnp.broadcast_to(jnp.sum(self32, axis=1, keepdims=True), (rows, page))
    tot = jnp.concatenate([tot, zpad], axis=0)
    earlier = lax.broadcasted_iota(I32, (kpad, kpad), 1) < lax.broadcasted_iota(I32, (kpad, kpad), 0)
    before = _dot(ones_where(earlier), tot.astype(BF16))

    before_row = before.T[0:1, :]
    ends_row = (before + tot).T[0:1, :]
    slot_r = lax.broadcasted_iota(I32, (topk, kpad), 0).astype(F32)
    owner = ones_where((before_row <= slot_r) & (slot_r < ends_row))
    code = jnp.concatenate([jnp.where(sel, within, -1.0), zpad], axis=0)
    row_id = lax.broadcasted_iota(I32, (kpad, page), 0).astype(F32)
    fetched = _dot(owner, jnp.concatenate([code, before, row_id], axis=1).astype(BF16))
    code_g, before_g, row_g = (fetched[:, 0:page], fetched[:, page:2 * page],
                               fetched[:, 2 * page:3 * page])
    slot = lax.broadcasted_iota(I32, (topk, page), 0).astype(F32)
    lane = lax.broadcasted_iota(I32, (topk, page), 1).astype(F32)
    total = (before + tot)[kpad - 1:kpad, :]
    hit = (code_g == slot - before_g) & (slot < total)
    lane_sel = jnp.max(jnp.where(hit, lane, -1.0), axis=1, keepdims=True)
    pos = jnp.where(lane_sel >= 0.0, row_g[:, 0:1] * page + lane_sel, -1.0)
    idx_ref[0] = pos.astype(I32)


def _sample_select(scores, *, topk):
    bs, rows, page = scores.shape
    assert rows % 8 == 0 and rows <= 256
    one = pl.BlockSpec((1, 1, 1), lambda b: (b, 0, 0))
    thr, lim = pl.pallas_call(
        functools.partial(_sample_thresh_kernel, topk=topk),
        out_shape=[jax.ShapeDtypeStruct((bs, 1, 1), I32)] * 2,
        compiler_params=pltpu.CompilerParams(vmem_limit_bytes=VMEM_LIMIT),
        name="sample_thresh",
    )(scores)
    return pl.pallas_call(
        functools.partial(_sample_compact_kernel, topk=topk),
        grid=(bs,),
        in_specs=[pl.BlockSpec((1, rows, page), lambda b: (b, 0, 0)), one, one],
        out_specs=pl.BlockSpec((1, topk, 1), lambda b: (b, 0, 0)),
        out_shape=jax.ShapeDtypeStruct((bs, topk, 1), I32),
        compiler_params=_cparams(("arbitrary",)),
        name="sample_compact",
    )(scores, thr, lim)


def _sample_attn_kernel(pt_ref, idx_ref, idxv_ref, q_ref, kn_ref, vn_ref, ck_ref, cv_ref, o_ref,
                        kbuf_ref, vbuf_ref, sem_ref, *, page, past, topk, n_rep):
    b = pl.program_id(0)
    nb = pl.num_programs(0)
    n_kv = kn_ref.shape[1]

    def row_copies(bb, slot, j):
        pos = jnp.clip(idx_ref[bb * topk + j], 0, past - 1)
        if page & (page - 1) == 0:
            page_no = lax.shift_right_logical(pos, page.bit_length() - 1)
            off = pos & (page - 1)
        else:
            page_no, off = lax.div(pos, page), lax.rem(pos, page)
        phys = pt_ref[bb * (past // page) + page_no]
        return (pltpu.make_async_copy(ck_ref.at[phys, off], kbuf_ref.at[slot, j], sem_ref.at[0, slot]),
                pltpu.make_async_copy(cv_ref.at[phys, off], vbuf_ref.at[slot, j], sem_ref.at[1, slot]))

    unroll = 8

    def start_all(bb, slot):
        def body(j8, carry):
            for u in range(unroll):
                ck, cv = row_copies(bb, slot, j8 * unroll + u)
                ck.start(priority=u % 2)
                cv.start(priority=(u + 1) % 2)
            return carry
        lax.fori_loop(0, topk // unroll, body, 0)

    slot = b % 2

    @pl.when(b == 0)
    def _():
        start_all(b, slot)

    @pl.when(b + 1 < nb)
    def _():
        start_all(b + 1, 1 - slot)

    for h in range(topk // page):
        slab = pl.ds(h * page, page)
        pltpu.make_async_copy(ck_ref.at[0], kbuf_ref.at[slot, slab], sem_ref.at[0, slot]).wait()
        pltpu.make_async_copy(cv_ref.at[0], vbuf_ref.at[slot, slab], sem_ref.at[1, slot]).wait()

    pos = idxv_ref[0]
    bias = jnp.where((pos >= 0) & (pos < past), 0.0, NEG_BIAS)
    own_sel = jnp.max(jnp.where(pos == past, 1.0, 0.0), axis=1, keepdims=True)
    own_bias = jnp.where(own_sel > 0.0, 0.0, NEG_BIAS)
    k_all = kbuf_ref[slot].astype(BF16)
    v_all = vbuf_ref[slot].astype(BF16)
    for g in range(n_kv):
        rows = slice(g * n_rep, (g + 1) * n_rep)
        qg = q_ref[0, rows, :]
        kg = k_all[:, g, :]
        vg = v_all[:, g, :]
        kn = kn_ref[0, g:g + 1, :].astype(BF16).astype(F32)
        vn = vn_ref[0, g:g + 1, :].astype(BF16).astype(F32)
        logit = _nt_dot(qg, kg) + bias
        own = jnp.sum(qg.astype(F32) * kn, axis=1, keepdims=True) + own_bias
        m = jnp.maximum(jnp.max(logit, axis=1, keepdims=True), own)
        p = jnp.exp2(logit - m)
        p_own = jnp.exp2(own - m)
        denom = jnp.sum(p, axis=1, keepdims=True) + p_own
        num = _dot(p.astype(BF16), vg) + p_own.astype(BF16).astype(F32) * vn
        o_ref[0, rows, :] = num / denom


def _sample_attention(page_table, idx, q, k_new, v_new, cache_k, cache_v, *, topk):
    bs, n_pages = page_table.shape
    page, n_kv, hd = cache_k.shape[1:]
    n_heads = q.shape[1]
    past = n_pages * page
    assert topk % page == 0 and topk % 8 == 0
    grid_spec = pltpu.PrefetchScalarGridSpec(
        num_scalar_prefetch=2,
        grid=(bs,),
        in_specs=[
            pl.BlockSpec((1, 1, topk), lambda b, pt, ix: (b, 0, 0)),
            pl.BlockSpec((1, n_heads, hd), lambda b, pt, ix: (b, 0, 0)),
            pl.BlockSpec((1, n_kv, hd), lambda b, pt, ix: (b, 0, 0)),
            pl.BlockSpec((1, n_kv, hd), lambda b, pt, ix: (b, 0, 0)),
            pl.BlockSpec(memory_space=pl.ANY),
            pl.BlockSpec(memory_space=pl.ANY),
        ],
        out_specs=pl.BlockSpec((1, n_heads, hd), lambda b, pt, ix: (b, 0, 0)),
        scratch_shapes=[pltpu.VMEM((2, topk, n_kv, hd), F32),
                        pltpu.VMEM((2, topk, n_kv, hd), F32),
                        pltpu.SemaphoreType.DMA((2, 2))],
    )
    return pl.pallas_call(
        functools.partial(_sample_attn_kernel, page=page, past=past, topk=topk,
                          n_rep=n_heads // n_kv),
        grid_spec=grid_spec,
        out_shape=jax.ShapeDtypeStruct((bs, n_heads, hd), F32),
        compiler_params=_cparams(("arbitrary",)),
        name="sample_attention",
    )(page_table.reshape(-1), idx.reshape(-1), idx.reshape(bs, 1, topk), q, k_new, v_new,
      cache_k, cache_v)


def _rope_tables(pos, dim, pad_lanes=0):
    rot = dim // ROT_DIV
    half = rot // 2
    inv = ROPE_THETA ** (-np.arange(half, dtype=np.float64) / half)
    ang = np.asarray(pos, np.float64)[:, None] * inv[None, :]
    cos, sin = jnp.asarray(np.cos(ang), F32), jnp.asarray(np.sin(ang), F32)
    width = dim + pad_lanes
    a = jnp.pad(jnp.concatenate([cos, cos], axis=1), ((0, 0), (0, width - rot)), constant_values=1.0)
    b = jnp.pad(-sin, ((0, 0), (0, width - half)))
    c = jnp.pad(sin, ((0, 0), (half, width - rot)))
    reps = LANES // width
    return tuple(jnp.tile(t, (1, reps)) if reps > 1 else t for t in (a, b, c))


def _in_proj_columns(d_model, d_conv, dq, dkv):
    names = ("x", "b", "c", "q", "k", "v", "iq", "ik", "iw", "gc", "ga")
    sizes = (d_conv, d_conv, d_conv, dq, dkv, dkv, IDX_HEADS * IDX_DIM, IDX_DIM, IDX_HEADS,
             d_model, d_model)
    starts = np.concatenate([[0], np.cumsum(sizes)[:-1]])
    return {n: int(s) for n, s in zip(names, starts)}


def _mixer_common(u, w_all, w_gates, cols, ktabs, itabs, *, hd, dq, dkv, tm, tn):
    q = _rope_proj(u, w_all, ktabs, first_col=cols["q"], n=dq, half=hd // ROT_DIV // 2,
                   scale=hd ** -0.5 * LOG2E, tm=tm, tn=tn)
    iq = _rope_proj(u, w_all, itabs[0], first_col=cols["iq"], n=IDX_HEADS * IDX_DIM,
                    half=IDX_DIM // ROT_DIV // 2, scale=IDX_DIM ** -0.5, tm=tm, tn=tn)
    assert cols["iw"] == cols["ik"] + IDX_DIM
    k, v, misc, k_bf, v_bf, ika, ikb = _kv_proj(
        u, w_all, ktabs, itabs[1], k_col=cols["k"], v_col=cols["v"], misc_col=cols["ik"],
        dkv=dkv, tm=tm)
    gc, ga = _gate_proj(u, w_gates, tm=tm, tn=tn)
    return q, iq, k, v, misc, k_bf, v_bf, ika, ikb, gc, ga


def kernel(x_prompt, x_sample, cache_k, cache_v, cache_idx_k, state_conv, page_table,
           norm_ffn1_pre, norm_ffn1_post, w_ffn1_gate_up, w_ffn1_down,
           norm_mix_pre, norm_mix_post, w_in, w_conv, w_conv_out, w_attn_out, w_out,
           norm_ffn2_pre, norm_ffn2_post, w_ffn2_gate_up, w_ffn2_down):
    bp, t, d = x_prompt.shape
    bs, ts, _ = x_sample.shape
    depth = w_in.shape[0]
    page, n_kv, hd = cache_k.shape[2:]
    n_pages = page_table.shape[1]
    past = n_pages * page
    d_conv = w_conv.shape[2]
    dq = w_attn_out.shape[1]
    dkv = n_kv * hd
    assert bp == 1 and ts == 1 and dq == N_HEADS * hd and n_kv == N_KV_HEADS

    tm = min(512, t)
    tm_proj = min(1024, t)
    tf = 512
    tn = 1024
    tq, sc = min(256, t), min(512, t)
    cols = _in_proj_columns(d, d_conv, dq, dkv)
    topk_p = min(TOPK_MAX, t // 4)
    topk_s = min(TOPK_MAX, (past + ts) // 4)

    pos_p = np.arange(t)
    pos_s = np.full((bs,), past)
    ktabs_p, ktabs_s = _rope_tables(pos_p, hd), _rope_tables(pos_s, hd)
    itabs_p = (_rope_tables(pos_p, IDX_DIM), _rope_tables(pos_p, IDX_DIM, LANES - IDX_DIM))
    itabs_s = (_rope_tables(pos_s, IDX_DIM), _rope_tables(pos_s, IDX_DIM, LANES - IDX_DIM))

    hp = x_prompt.reshape(t, d)
    hs = x_sample.reshape(bs, d)
    outs = [[] for _ in range(8)]
    row = lambda a: a.reshape(1, -1)
    for l in range(depth):
        w1gu, w1d = w_ffn1_gate_up[l].astype(BF16), w_ffn1_down[l].astype(BF16)
        w2gu, w2d = w_ffn2_gate_up[l].astype(BF16), w_ffn2_down[l].astype(BF16)
        w_all, w_gates = w_in[l].astype(BF16), w_in[l][:, cols["gc"]:].astype(BF16)
        w_co, w_ao, w_o = (w_conv_out[l].astype(BF16), w_attn_out[l].astype(BF16),
                           w_out[l].astype(BF16))
        gain_mix = row(norm_mix_pre[l])

        hp, up = _ffn_half(hp, row(norm_ffn1_pre[l]), row(norm_ffn1_post[l]), w1gu, w1d,
                           tm=tm, tf=tf, next_gain=gain_mix)
        conv_y, tail = _conv_seq(up, w_all, w_conv[l], tm=tm, tn=tn)
        q, iq, k, v, misc, k_bf, v_bf, ika, ikb, gc, ga = _mixer_common(
            up, w_all, w_gates, cols, ktabs_p, itabs_p, hd=hd, dq=dq, dkv=dkv, tm=tm_proj, tn=tn)
        wt = (misc[:, IDX_DIM:IDX_DIM + IDX_HEADS] * IDX_HEADS ** -0.5).T
        vt = v_bf.reshape(t // sc, sc, n_kv, hd).transpose(2, 0, 3, 1)
        vt = jnp.concatenate([vt, jnp.ones((n_kv, t // sc, 16, sc), BF16)], axis=2)
        attn_o = _prompt_attention(iq, wt, q, ika, ikb, k_bf, vt, tq=tq, sc=sc, topk=topk_p, hd=hd)
        hp = _merge(hp, conv_y, attn_o, gc, ga, row(norm_mix_post[l]), w_co, w_ao, w_o,
                    tm=tm, tn=tn // 2)
        outs[0].append(k.reshape(bp, t // page, page, n_kv, hd))
        outs[1].append(v.reshape(bp, t // page, page, n_kv, hd))
        outs[2].append(misc[:, :IDX_DIM].reshape(bp, t // page, page, IDX_DIM))
        outs[3].append(tail[tail.shape[0] - (CONV_WIDTH - 1):].reshape(bp, CONV_WIDTH - 1, d_conv))
        hp = _ffn_half(hp, row(norm_ffn2_pre[l]), row(norm_ffn2_post[l]), w2gu, w2d, tm=tm, tf=tf)

        hs, us = _ffn_half(hs, row(norm_ffn1_pre[l]), row(norm_ffn1_post[l]), w1gu, w1d,
                           tm=bs, tf=tf, next_gain=gain_mix)
        st = state_conv[l]
        conv_y, z = _conv_step(us, w_all, w_conv[l], st[:, 0, :], st[:, 1, :], tn=tn)
        q, iq, k, v, misc, k_bf, v_bf, ika, ikb, gc, ga = _mixer_common(
            us, w_all, w_gates, cols, ktabs_s, itabs_s, hd=hd, dq=dq, dkv=dkv, tm=bs, tn=tn)
        w_idx = (misc[:, IDX_DIM:IDX_DIM + IDX_HEADS] * IDX_HEADS ** -0.5).reshape(bs, IDX_HEADS, 1)
        scores = _sample_scores(page_table, iq.reshape(bs, IDX_HEADS, IDX_DIM), w_idx,
                                ika[:, :IDX_DIM].reshape(bs, IDX_DIM, 1),
                                cache_idx_k[l].transpose(0, 2, 1))
        idx = _sample_select(scores, topk=topk_s).reshape(bs, topk_s)
        attn_o = _sample_attention(page_table, idx, q.reshape(bs, N_HEADS, hd),
                                   k.reshape(bs, n_kv, hd), v.reshape(bs, n_kv, hd),
                                   cache_k[l], cache_v[l], topk=topk_s)
        hs = _merge(hs, conv_y, attn_o.reshape(bs, dq).astype(BF16), gc, ga, row(norm_mix_post[l]),
                    w_co, w_ao, w_o, tm=bs, tn=tn // 2)
        outs[4].append(k.reshape(bs, ts, n_kv, hd))
        outs[5].append(v.reshape(bs, ts, n_kv, hd))
        outs[6].append(misc[:, :IDX_DIM].reshape(bs, ts, IDX_DIM))
        outs[7].append(jnp.stack([st[:, 1, :], z], axis=1))
        hs = _ffn_half(hs, row(norm_ffn2_pre[l]), row(norm_ffn2_post[l]), w2gu, w2d, tm=bs, tf=tf)

    return (hp.reshape(bp, t, d), hs.reshape(bs, ts, d)) + tuple(jnp.stack(o) for o in outs)
```
